```python
import math
import jax, jax.numpy as jnp
from jax import lax
import numpy as np

D_MODEL = 1024
BATCH = 4
SEQ = 4096
DEPTH = 2
DEC_BATCH = 128
DEC_SEQ = 1
PAST_LEN = 2048
PAGE_SIZE = 128

PLE_DIM = 256
N_EVEN = (DEPTH + 1) // 2
N_ODD = DEPTH // 2

GLA_HEADS = 4
GLA_DV = D_MODEL // 2 // GLA_HEADS
GLA_DK = GLA_DV // 2
GLA_LOWRANK = 16
GLA_TAU = 16.0
GLA_CHUNK = 64
POOL_WINDOWS = (2, 4, 8, 16)
POOL_CH = D_MODEL // 2
POOL_GROUP = POOL_CH // len(POOL_WINDOWS)
POOL_HIST = max(POOL_WINDOWS) - 1
QK_A = GLA_HEADS * GLA_DK
V_A = GLA_HEADS * GLA_DV
IN_AB = 2 * QK_A + 2 * V_A + GLA_LOWRANK + POOL_CH
MIX_AB = V_A + POOL_CH

MOBA_HEADS = 16
MOBA_DH = D_MODEL // MOBA_HEADS
MOBA_BLOCK = 256
MOBA_TOPK = 3
MOBA_QCHUNK = 16
MIX_C = MOBA_HEADS * MOBA_DH

N_EXPERTS = 32
TOP_K = 4
D_EXPERT = D_MODEL
SWIGLU_LIMIT = 7.0
SWIGLU_ALPHA = 1.702
MOE_BLOCK = 128

DEEPNORM_ALPHA = (2 * DEPTH) ** 0.25
DEEPNORM_BETA = (8 * DEPTH) ** -0.25
LN_EPS = 1e-5

kernel_name = 'hybrid_gla_pool_moba_moe_decode_step'


def layer_norm(x, g, b):
    xf = x.astype(jnp.float32)
    mu = jnp.mean(xf, axis=-1, keepdims=True)
    xc = xf - mu
    var = jnp.mean(xc * xc, axis=-1, keepdims=True)
    return (xc * lax.rsqrt(var + LN_EPS) * g + b).astype(x.dtype)


def alibi_slopes(h):
    return jnp.exp2(-8.0 * jnp.arange(1, h + 1, dtype=jnp.float32) / h)


def gla_recurrence(q, k, v, log_a, s0):
    B, T, H, DK = q.shape
    DV = v.shape[-1]
    C = GLA_CHUNK if T % GLA_CHUNK == 0 else T
    n = T // C

    def chunks(t):
        return t.astype(jnp.float32).reshape(B, n, C, H, t.shape[-1]).transpose(1, 0, 3, 2, 4)

    causal = jnp.tril(jnp.ones((C, C), bool))[:, :, None]

    def step(S, inp):
        qc, kc, vc, ac = inp
        b = jnp.cumsum(ac, axis=2)
        diff = jnp.where(causal, b[:, :, :, None, :] - b[:, :, None, :, :], -jnp.inf)
        attn = jnp.einsum('bhid,bhjd,bhijd->bhij', qc, kc, jnp.exp(diff))
        o = (jnp.einsum('bhid,bhde->bhie', qc * jnp.exp(b), S)
             + jnp.einsum('bhij,bhje->bhie', attn, vc))
        b_last = b[:, :, -1:, :]
        S = (jnp.exp(b_last)[:, :, 0, :, None] * S
             + jnp.einsum('bhjd,bhje->bhde', kc * jnp.exp(b_last - b), vc))
        return S, o

    S, o = lax.scan(step, s0.astype(jnp.float32), (chunks(q), chunks(k), chunks(v), chunks(log_a)))
    return o.transpose(1, 0, 3, 2, 4).reshape(B, T, H, DV), S


def pool_mix(u, prefix, pos0, w_pool, scale):
    B, T, _ = u.shape
    z = jnp.concatenate([prefix.astype(u.dtype), u], axis=1)
    cs = jnp.cumsum(z.astype(jnp.float32), axis=1)
    cs = jnp.concatenate([jnp.zeros((B, 1, POOL_CH), jnp.float32), cs], axis=1)
    pos = pos0 + jnp.arange(T, dtype=jnp.int32)
    outs = []
    for g, w in enumerate(POOL_WINDOWS):
        sl = slice(g * POOL_GROUP, (g + 1) * POOL_GROUP)
        win = cs[:, POOL_HIST + 1:POOL_HIST + 1 + T, sl] - cs[:, POOL_HIST + 1 - w:POOL_HIST + 1 - w + T, sl]
        cnt = jnp.minimum(w, pos + 1).astype(jnp.float32)[None, :, None]
        outs.append(win / cnt - z[:, POOL_HIST:, sl].astype(jnp.float32))
    d = jnp.stack(outs, axis=2)
    y = jnp.einsum('btgc,gcd->btgd', d, w_pool.astype(jnp.float32)).reshape(B, T, POOL_CH) * scale
    return y, z[:, -POOL_HIST:]


def mixer_ab(x, w_in, w_alpha, b_alpha, norm_g, w_pool, pool_scale, w_out, s0, pool_prefix, pos0):
    B, T, _ = x.shape
    h = x @ w_in
    cuts = np.cumsum([QK_A, QK_A, V_A, V_A, GLA_LOWRANK]).tolist()
    q, k, v, g, a_lr, u = jnp.split(h, cuts, axis=-1)
    q = q.reshape(B, T, GLA_HEADS, GLA_DK) * (GLA_DK ** -0.5)
    k = k.reshape(B, T, GLA_HEADS, GLA_DK)
    v = v.reshape(B, T, GLA_HEADS, GLA_DV)
    log_a = jax.nn.log_sigmoid((a_lr @ w_alpha + b_alpha).astype(jnp.float32)) / GLA_TAU
    o, s_new = gla_recurrence(q, k, v, log_a.reshape(B, T, GLA_HEADS, GLA_DK), s0)
    o = o * lax.rsqrt(jnp.mean(o * o, axis=-1, keepdims=True) + LN_EPS) * norm_g
    o = o.reshape(B, T, V_A) * jax.nn.silu(g.astype(jnp.float32))
    pooled, pool_state = pool_mix(u, pool_prefix, pos0, w_pool, pool_scale)
    y = jnp.concatenate([o, pooled], axis=-1).astype(x.dtype) @ w_out
    return y, s_new, pool_state


def to_blocks(parts):
    B = parts[0].shape[0]
    L = sum(p.shape[1] for p in parts)
    pad = -L % MOBA_BLOCK
    if pad:
        parts = parts + [jnp.zeros((B, pad) + parts[0].shape[2:], parts[0].dtype)]
    full = jnp.concatenate(parts, axis=1) if len(parts) > 1 else parts[0]
    return full.reshape(B, -1, MOBA_BLOCK, MOBA_HEADS, MOBA_DH)


def moba_attend(q, kb, vb, pos0):
    B, Tq, H, DH = q.shape
    NB = kb.shape[1]
    topk = min(MOBA_TOPK, NB)
    n_sel = topk + 1
    k_mean = jnp.mean(kb, axis=2, dtype=jnp.float32)
    slopes = alibi_slopes(H)
    Qc = MOBA_QCHUNK if Tq % MOBA_QCHUNK == 0 else Tq
    nq = Tq // Qc
    q_chunks = q.reshape(B, nq, Qc, H, DH).transpose(1, 0, 3, 2, 4)
    pos_chunks = (pos0 + jnp.arange(Tq, dtype=jnp.int32)).reshape(nq, Qc)
    bi = jnp.arange(B)[:, None, None, None]
    hi = jnp.arange(H)[None, :, None, None]
    offs = jnp.arange(MOBA_BLOCK, dtype=jnp.int32)
    scale = DH ** -0.5

    def attend_chunk(args):
        qc, pos = args
        cur = pos // MOBA_BLOCK
        gate = jnp.einsum('bhqd,bnhd->bhqn', qc, k_mean, preferred_element_type=jnp.float32)
        gate = jnp.where(jnp.arange(NB)[None, :] < cur[:, None], gate, -jnp.inf)
        _, sel = lax.top_k(gate, topk)
        blocks = jnp.concatenate([sel, jnp.broadcast_to(cur[None, None, :, None], (B, H, Qc, 1))], axis=-1)
        slot_ok = jnp.concatenate([jnp.arange(topk)[None, :] < cur[:, None], jnp.ones((Qc, 1), bool)], axis=-1)
        kg = kb[bi, blocks, :, hi, :]
        vg = vb[bi, blocks, :, hi, :]
        kpos = blocks[..., None] * MOBA_BLOCK + offs
        dist = (pos[:, None, None] - kpos).astype(jnp.float32)
        s = (jnp.einsum('bhqd,bhqnkd->bhqnk', qc, kg, preferred_element_type=jnp.float32) * scale
             - slopes[:, None, None, None] * dist)
        mask = slot_ok[:, :, None] & (kpos <= pos[:, None, None])
        s = jnp.where(mask, s, -jnp.inf).reshape(B, H, Qc, n_sel * MOBA_BLOCK)
        p = jax.nn.softmax(s, axis=-1)
        return jnp.einsum('bhqm,bhqmd->bhqd', p, vg.reshape(B, H, Qc, n_sel * MOBA_BLOCK, DH))

    o = lax.map(attend_chunk, (q_chunks, pos_chunks))
    return o.transpose(1, 0, 3, 2, 4).reshape(B, Tq, H, DH)


def mixer_c(x, w_qkv, w_out, k_past, v_past, pos0):
    B, T, _ = x.shape
    q, k, v = jnp.split(x @ w_qkv, 3, axis=-1)
    q = q.reshape(B, T, MOBA_HEADS, MOBA_DH)
    k = k.reshape(B, T, MOBA_HEADS, MOBA_DH)
    v = v.reshape(B, T, MOBA_HEADS, MOBA_DH)
    kb = to_blocks([k] if k_past is None else [k_past, k])
    vb = to_blocks([v] if v_past is None else [v_past, v])
    o = moba_attend(q, kb, vb, pos0)
    return o.reshape(B, T, MIX_C).astype(x.dtype) @ w_out, k, v


def moe(x, w_router, b_router, w_gu, b_gu, w_down, b_down):
    shp = x.shape
    x2 = x.reshape(-1, D_MODEL)
    N = x2.shape[0]
    NK = N * TOP_K
    logits = (x2 @ w_router).astype(jnp.float32) + b_router
    top_val, top_idx = lax.top_k(logits, TOP_K)
    gates = jax.nn.softmax(top_val, axis=-1)
    flat_e = top_idx.reshape(NK)
    order = jnp.argsort(flat_e)
    e_sorted = flat_e[order]
    tok_sorted = (order // TOP_K).astype(jnp.int32)
    gate_sorted = gates.reshape(NK)[order]
    counts = jnp.zeros(N_EXPERTS, jnp.int32).at[flat_e].add(1)
    padded = (counts + MOE_BLOCK - 1) // MOE_BLOCK * MOE_BLOCK
    pend = jnp.cumsum(padded)
    pstart = pend - padded
    start = jnp.cumsum(counts) - counts
    dest = pstart[e_sorted] + jnp.arange(NK, dtype=jnp.int32) - start[e_sorted]
    n_blocks = (NK + N_EXPERTS * (MOE_BLOCK - 1)) // MOE_BLOCK + 1
    P = n_blocks * MOE_BLOCK
    row_tok = jnp.zeros(P, jnp.int32).at[dest].set(tok_sorted)
    row_gate = jnp.zeros(P, jnp.float32).at[dest].set(gate_sorted)
    block_e = jnp.minimum(jnp.searchsorted(pend, jnp.arange(n_blocks, dtype=jnp.int32) * MOE_BLOCK, side='right'),
                          N_EXPERTS - 1)
    xs = x2[row_tok].reshape(n_blocks, MOE_BLOCK, D_MODEL)

    def expert_block(args):
        xb, e = args
        h = (xb @ w_gu[e]).astype(jnp.float32) + b_gu[e]
        gate, up = h[:, :D_EXPERT], h[:, D_EXPERT:]
        gate = jnp.minimum(gate, SWIGLU_LIMIT)
        up = jnp.clip(up, -SWIGLU_LIMIT, SWIGLU_LIMIT)
        act = (up + 1.0) * gate * jax.nn.sigmoid(SWIGLU_ALPHA * gate)
        return act.astype(xb.dtype) @ w_down[e] + b_down[e]

    ys = lax.map(expert_block, (xs, block_e)).reshape(P, D_MODEL)
    out = jnp.zeros((N, D_MODEL), jnp.float32).at[row_tok].add(ys.astype(jnp.float32) * row_gate[:, None])
    return out.astype(x.dtype).reshape(shp)


def finish_layer(x, h, p, ln_g, ln_b, w_router, b_router, w_gu, b_gu, w_down, b_down, w_pg, w_pp):
    x = layer_norm(DEEPNORM_ALPHA * x + h, ln_g[0], ln_b[0])
    x = layer_norm(DEEPNORM_ALPHA * x + moe(x, w_router, b_router, w_gu, b_gu, w_down, b_down), ln_g[1], ln_b[1])
    return x + jax.nn.sigmoid(x @ w_pg) * (p.astype(x.dtype) @ w_pp)


def setup_inputs(seed: int = 0) -> dict:
    key = jax.random.key(seed)
    ks = jax.random.split(key, 32)
    f32 = jnp.float32

    def nrm(k, shape, s):
        return jax.random.normal(k, shape, f32) * s

    n_pages = PAST_LEN // PAGE_SIZE
    n_phys = (DEC_BATCH * n_pages * 5) // 4
    page_table = jax.random.permutation(ks[7], n_phys)[:DEC_BATCH * n_pages].reshape(DEC_BATCH, n_pages).astype(jnp.int32)
    return {
        'x_prompt': nrm(ks[0], (BATCH, SEQ, D_MODEL), 1.0),
        'x_sample': nrm(ks[1], (DEC_BATCH, DEC_SEQ, D_MODEL), 1.0),
        'state_gla': nrm(ks[2], (N_EVEN, DEC_BATCH, GLA_HEADS, GLA_DK, GLA_DV), 1.0),
        'state_pool': nrm(ks[3], (N_EVEN, DEC_BATCH, POOL_HIST, POOL_CH), 1.0),
        'cache_k': nrm(ks[4], (N_ODD, n_phys, PAGE_SIZE, MOBA_HEADS, MOBA_DH), 1.0),
        'cache_v': nrm(ks[5], (N_ODD, n_phys, PAGE_SIZE, MOBA_HEADS, MOBA_DH), 1.0),
        'page_table': page_table,
        'p_prompt': nrm(ks[8], (DEPTH, BATCH, SEQ, PLE_DIM), 1.0),
        'p_sample': nrm(ks[9], (DEPTH, DEC_BATCH, DEC_SEQ, PLE_DIM), 1.0),
        'w_in_ab': nrm(ks[10], (N_EVEN, D_MODEL, IN_AB), D_MODEL ** -0.5),
        'gla_w_alpha': nrm(ks[11], (N_EVEN, GLA_LOWRANK, QK_A), GLA_LOWRANK ** -0.5),
        'gla_b_alpha': nrm(ks[12], (N_EVEN, QK_A), 0.1),
        'gla_norm_g': 1.0 + nrm(ks[13], (N_EVEN, GLA_DV), 0.02),
        'pool_w': nrm(ks[14], (N_EVEN, len(POOL_WINDOWS), POOL_GROUP, POOL_GROUP), POOL_GROUP ** -0.5),
        'pool_scale': 1.0 + nrm(ks[15], (N_EVEN, POOL_CH), 0.05),
        'w_out_ab': nrm(ks[16], (N_EVEN, MIX_AB, D_MODEL), MIX_AB ** -0.5 * DEEPNORM_BETA),
        'w_qkv_c': nrm(ks[17], (N_ODD, D_MODEL, 3 * MIX_C), D_MODEL ** -0.5),
        'w_out_c': nrm(ks[18], (N_ODD, MIX_C, D_MODEL), MIX_C ** -0.5 * DEEPNORM_BETA),
        'ln_g': 1.0 + nrm(ks[19], (DEPTH, 2, D_MODEL), 0.02),
        'ln_b': nrm(ks[20], (DEPTH, 2, D_MODEL), 0.02),
        'moe_w_router': nrm(ks[21], (DEPTH, D_MODEL, N_EXPERTS), D_MODEL ** -0.5),
        'moe_b_router': nrm(ks[22], (DEPTH, N_EXPERTS), 0.01),
        'moe_w_gu': nrm(ks[23], (DEPTH, N_EXPERTS, D_MODEL, 2 * D_EXPERT), D_MODEL ** -0.5),
        'moe_b_gu': nrm(ks[24], (DEPTH, N_EXPERTS, 2 * D_EXPERT), 0.01),
        'moe_w_down': nrm(ks[25], (DEPTH, N_EXPERTS, D_EXPERT, D_MODEL), D_EXPERT ** -0.5 * DEEPNORM_BETA),
        'moe_b_down': nrm(ks[26], (DEPTH, N_EXPERTS, D_MODEL), 0.01),
        'ple_w_gate': nrm(ks[27], (DEPTH, D_MODEL, D_MODEL), D_MODEL ** -0.5),
        'ple_w_proj': nrm(ks[28], (DEPTH, PLE_DIM, D_MODEL), PLE_DIM ** -0.5),
    }


def reference(x_prompt, x_sample, state_gla, state_pool, cache_k, cache_v, page_table, p_prompt, p_sample,
              w_in_ab, gla_w_alpha, gla_b_alpha, gla_norm_g, pool_w, pool_scale, w_out_ab, w_qkv_c, w_out_c,
              ln_g, ln_b, moe_w_router, moe_b_router, moe_w_gu, moe_b_gu, moe_w_down, moe_b_down,
              ple_w_gate, ple_w_proj):
    xp, xs = x_prompt, x_sample
    Bp, Bs = xp.shape[0], xs.shape[0]
    gla_p, gla_s, pool_p, pool_s = [], [], [], []
    kp_l, vp_l, ks_l, vs_l = [], [], [], []
    for i in range(DEPTH):
        j = i // 2
        if i % 2 == 0:
            ab = (w_in_ab[j], gla_w_alpha[j], gla_b_alpha[j], gla_norm_g[j], pool_w[j], pool_scale[j], w_out_ab[j])
            hp, sp, pp = mixer_ab(xp, *ab, jnp.zeros((Bp, GLA_HEADS, GLA_DK, GLA_DV), jnp.float32),
                                  jnp.zeros((Bp, POOL_HIST, POOL_CH), xp.dtype), 0)
            hs, ss, ps = mixer_ab(xs, *ab, state_gla[j], state_pool[j], PAST_LEN)
            gla_p.append(sp)
            gla_s.append(ss)
            pool_p.append(pp)
            pool_s.append(ps)
        else:
            k_past = cache_k[j][page_table].reshape(Bs, -1, MOBA_HEADS, MOBA_DH)
            v_past = cache_v[j][page_table].reshape(Bs, -1, MOBA_HEADS, MOBA_DH)
            hp, kp, vp = mixer_c(xp, w_qkv_c[j], w_out_c[j], None, None, 0)
            hs, kn, vn = mixer_c(xs, w_qkv_c[j], w_out_c[j], k_past, v_past, PAST_LEN)
            kp_l.append(kp)
            vp_l.append(vp)
            ks_l.append(kn)
            vs_l.append(vn)
        shared = (ln_g[i], ln_b[i], moe_w_router[i], moe_b_router[i], moe_w_gu[i], moe_b_gu[i],
                  moe_w_down[i], moe_b_down[i], ple_w_gate[i], ple_w_proj[i])
        xp = finish_layer(xp, hp, p_prompt[i], *shared)
        xs = finish_layer(xs, hs, p_sample[i], *shared)
    return (xp, xs, jnp.stack(gla_p), jnp.stack(gla_s), jnp.stack(pool_p), jnp.stack(pool_s),
            jnp.stack(kp_l), jnp.stack(vp_l), jnp.stack(ks_l), jnp.stack(vs_l))
```

```python
import functools
import math

import numpy as np
import jax
import jax.numpy as jnp
from jax import lax
from jax.experimental import pallas as pl
from jax.experimental.pallas import tpu as pltpu

F32 = jnp.float32
BF16 = jnp.bfloat16
HI = lax.Precision.HIGHEST

D_MODEL = 1024
GLA_HEADS = 4
GLA_DK = 64
GLA_DV = 128
GLA_LOWRANK = 16
GLA_TAU = 16.0
QK_A = GLA_HEADS * GLA_DK
V_A = GLA_HEADS * GLA_DV
POOL_WINDOWS = (2, 4, 8, 16)
POOL_CH = 512
POOL_GROUP = 128
POOL_HIST = 15
MOBA_HEADS = 16
MOBA_DH = 64
MOBA_BLOCK = 256
MOBA_TOPK = 3
PAGE_SIZE = 128
N_EXPERTS = 32
TOP_K = 4
D_EXPERT = 1024
SWIGLU_LIMIT = 7.0
SWIGLU_ALPHA = 1.702
DEPTH = 2
DEEPNORM_ALPHA = (2 * DEPTH) ** 0.25
LN_EPS = 1e-5

ROW_TILE = 512
GLA_CHUNK = 128
GLA_LEVELS = 7
MOE_BLOCK = 512
RANK_TILE = 384
VMEM_LIMIT = 56 * 1024 * 1024


def _params(n_axes, vmem=VMEM_LIMIT):
    return pltpu.CompilerParams(dimension_semantics=("arbitrary",) * n_axes, vmem_limit_bytes=vmem)


def _bdot(a, b):
    return jnp.dot(a.astype(BF16), b.astype(BF16), preferred_element_type=F32)


def _layer_norm(y, g, b):
    mu = jnp.mean(y, axis=-1, keepdims=True)
    yc = y - mu
    var = jnp.mean(yc * yc, axis=-1, keepdims=True)
    return yc * lax.rsqrt(var + LN_EPS) * g + b


def _full(shape):
    n = len(shape)
    return pl.BlockSpec(shape, lambda *_: (0,) * n)


def _proj_ab_kernel(x_ref, wm_ref, wa_ref, walpha_ref, balpha_ref,
                    qkl_ref, v_ref, sg_ref, u_ref, wm_bf, wa_bf):
    @pl.when(pl.program_id(0) == 0)
    def _():
        wm_bf[...] = wm_ref[...].astype(BF16)
        wa_bf[...] = wa_ref[...].astype(BF16)

    xb = x_ref[...].astype(BF16)
    qk = jnp.dot(xb, wm_bf[:, 0:2 * QK_A], preferred_element_type=F32)
    qkl_ref[:, 0:QK_A] = qk[:, 0:QK_A] * (GLA_DK ** -0.5)
    qkl_ref[:, QK_A:2 * QK_A] = qk[:, QK_A:2 * QK_A]
    a_lr = jnp.dot(xb, wa_bf[...], preferred_element_type=F32)
    z = _bdot(a_lr, walpha_ref[...]) + balpha_ref[...]
    log_sig = jnp.minimum(z, 0.0) - jnp.log1p(jnp.exp(-jnp.abs(z)))
    qkl_ref[:, 2 * QK_A:3 * QK_A] = log_sig / GLA_TAU
    c0 = 2 * QK_A
    v_ref[...] = jnp.dot(xb, wm_bf[:, c0:c0 + V_A], preferred_element_type=F32)
    g = jnp.dot(xb, wm_bf[:, c0 + V_A:c0 + 2 * V_A], preferred_element_type=F32)
    sg_ref[...] = g * jax.nn.sigmoid(g)
    u_ref[...] = jnp.dot(xb, wm_bf[:, c0 + 2 * V_A:c0 + 2 * V_A + POOL_CH], preferred_element_type=F32)


def _proj_ab(x, w_main, w_a, w_alpha, b_alpha, tm):
    m = x.shape[0]
    nmain = w_main.shape[1]
    row = lambda w: pl.BlockSpec((tm, w), lambda i: (i, 0))
    return pl.pallas_call(
        _proj_ab_kernel,
        grid=(m // tm,),
        in_specs=[row(D_MODEL), _full((D_MODEL, nmain)), _full((D_MODEL, GLA_LOWRANK)),
                  _full((GLA_LOWRANK, QK_A)), _full((1, QK_A))],
        out_specs=[row(3 * QK_A), row(V_A), row(V_A), row(POOL_CH)],
        out_shape=[jax.ShapeDtypeStruct((m, 3 * QK_A), F32), jax.ShapeDtypeStruct((m, V_A), F32),
                   jax.ShapeDtypeStruct((m, V_A), F32), jax.ShapeDtypeStruct((m, POOL_CH), F32)],
        scratch_shapes=[pltpu.VMEM((D_MODEL, nmain), BF16), pltpu.VMEM((D_MODEL, GLA_LOWRANK), BF16)],
        compiler_params=_params(1),
        name="proj_ab",
    )(x, w_main, w_a, w_alpha, b_alpha)


def _gla_tables():
    c = GLA_CHUNK
    i = np.arange(c)[:, None]
    s = np.arange(c)[None, :]
    mats = [(s <= i), (s > i)]
    for lev in range(GLA_LEVELS):
        p = GLA_LEVELS - 1 - lev
        half = 1 << p
        start = (i >> (p + 1)) << (p + 1)
        mid = start + half - 1
        upper = i >= start + half
        mats.append(np.where(upper, (s > mid) & (s <= i), (s > i) & (s <= mid)))
    seg = np.concatenate(mats, axis=0).astype(np.float32)
    j = np.arange(c)[None, :]
    x = i ^ j
    lvl = np.full((c, c), GLA_LEVELS + 1, np.int32)
    lvl[np.arange(c), np.arange(c)] = GLA_LEVELS
    for lev in range(GLA_LEVELS):
        p = GLA_LEVELS - 1 - lev
        lvl = np.where(((x >> p) == 1) & (((i >> p) & 1) == 1), lev, lvl)
    lvl4 = np.tile(lvl, (GLA_HEADS, 1)).astype(np.int32)
    lane_head = (np.arange(QK_A) // GLA_DK)[None, :]
    row_head = (np.arange(GLA_HEADS * c) // c)[:, None]
    hm4 = (lane_head == row_head).astype(np.float32)
    return seg, lvl4, hm4


def _gla_chunk(q, k, la, v, s_all, seg, lvl4, hm4):
    c = GLA_CHUNK
    e = jnp.dot(seg, la, precision=HI, preferred_element_type=F32)
    w = jnp.exp(e)
    w_b = w[0:c]
    w_k = w[c:2 * c]

    def stack_heads(t):
        return (jnp.concatenate([t] * GLA_HEADS, axis=0) * hm4).astype(BF16)

    o_inter = jnp.dot(stack_heads(q * w_b), s_all.astype(BF16), preferred_element_type=F32)
    a = jnp.zeros((GLA_HEADS * c, c), F32)
    for lev in range(GLA_LEVELS + 1):
        if lev < GLA_LEVELS:
            w_l = w[(2 + lev) * c:(3 + lev) * c]
            ql, kl = q * w_l, k * w_l
        else:
            ql, kl = q, k
        p_l = lax.dot_general(stack_heads(ql), kl.astype(BF16), (((1,), (1,)), ((), ())),
                              preferred_element_type=F32)
        a = jnp.where(lvl4 == lev, p_l, a)
    a = a.astype(BF16)
    outs = []
    for h in range(GLA_HEADS):
        v_h = v[:, h * GLA_DV:(h + 1) * GLA_DV].astype(BF16)
        o_h = o_inter[h * c:(h + 1) * c] + jnp.dot(a[h * c:(h + 1) * c], v_h, preferred_element_type=F32)
        outs.append(o_h)
    ks_t = (k * w_k).T.astype(BF16)
    kv = jnp.dot(ks_t, v.astype(BF16), preferred_element_type=F32)
    dec = jnp.exp(jnp.sum(la.T, axis=1, keepdims=True))
    new_rows = []
    for h in range(GLA_HEADS):
        rows = slice(h * GLA_DK, (h + 1) * GLA_DK)
        new_rows.append(dec[rows] * s_all[rows] + kv[rows, h * GLA_DV:(h + 1) * GLA_DV])
    return outs, jnp.concatenate(new_rows, axis=0)


def _gla_finish(o_h, norm_g, sg_h):
    o_h = o_h * lax.rsqrt(jnp.mean(o_h * o_h, axis=-1, keepdims=True) + LN_EPS) * norm_g
    return o_h * sg_h


def _gla_prompt_kernel(qkl_ref, v_ref, sg_ref, ng_ref, seg_ref, lvl_ref, hm_ref,
                       mix_ref, state_ref, s_scr):
    t = pl.program_id(1)

    @pl.when(t == 0)
    def _():
        s_scr[...] = jnp.zeros_like(s_scr)

    seg = seg_ref[...]
    lvl4 = lvl_ref[...]
    hm4 = hm_ref[...]
    norm_g = ng_ref[...]
    n_chunks = qkl_ref.shape[0] // GLA_CHUNK

    def body(ci, carry):
        r0 = pl.multiple_of(ci * GLA_CHUNK, GLA_CHUNK)
        rows = pl.ds(r0, GLA_CHUNK)
        q = qkl_ref[rows, 0:QK_A]
        k = qkl_ref[rows, QK_A:2 * QK_A]
        la = qkl_ref[rows, 2 * QK_A:3 * QK_A]
        v = v_ref[rows, :]
        outs, s_new = _gla_chunk(q, k, la, v, s_scr[...], seg, lvl4, hm4)
        s_scr[...] = s_new
        for h in range(GLA_HEADS):
            cols = slice(h * GLA_DV, (h + 1) * GLA_DV)
            mix_ref[rows, cols] = _gla_finish(outs[h], norm_g, sg_ref[rows, cols])
        return carry

    lax.fori_loop(0, n_chunks, body, 0)

    @pl.when(t == pl.num_programs(1) - 1)
    def _():
        state_ref[0] = s_scr[...]


def _gla_prompt(qkl, v, sg, norm_g, batch, seq, tm):
    m = qkl.shape[0]
    nt = seq // tm
    seg, lvl4, hm4 = _gla_tables()
    row = lambda w: pl.BlockSpec((tm, w), lambda b, t: (b * nt + t, 0))
    return pl.pallas_call(
        _gla_prompt_kernel,
        grid=(batch, nt),
        in_specs=[row(3 * QK_A), row(V_A), row(V_A), _full((1, GLA_DV)),
                  _full(seg.shape), _full(lvl4.shape), _full(hm4.shape)],
        out_specs=[pl.BlockSpec((tm, V_A), lambda b, t: (b * nt + t, 0)),
                   pl.BlockSpec((1, QK_A, GLA_DV), lambda b, t: (b, 0, 0))],
        out_shape=[jax.ShapeDtypeStruct((m, D_MODEL), F32),
                   jax.ShapeDtypeStruct((batch, QK_A, GLA_DV), F32)],
        scratch_shapes=[pltpu.VMEM((QK_A, GLA_DV), F32)],
        compiler_params=_params(2),
        name="gla_prompt",
    )(qkl, v, sg, norm_g, jnp.asarray(seg), jnp.asarray(lvl4), jnp.asarray(hm4))


GLA_SAMPLE_ROWS = 8


def _gla_sample_kernel(qkl_ref, v_ref, sg_ref, ng_ref, s0_ref, mix_ref, s1_ref):
    ones = jnp.ones((QK_A, GLA_DV), F32)
    r_i = lax.broadcasted_iota(jnp.int32, (QK_A, QK_A), 0)
    c_i = lax.broadcasted_iota(jnp.int32, (QK_A, QK_A), 1)
    eye = r_i == c_i
    norm_g = ng_ref[...]

    def col_bcast(row):
        diag = jnp.where(eye, jnp.broadcast_to(row, (QK_A, QK_A)), 0.0)
        return jnp.dot(diag, ones, precision=HI, preferred_element_type=F32)

    for r in range(GLA_SAMPLE_ROWS):
        q = col_bcast(qkl_ref[r:r + 1, 0:QK_A])
        k = col_bcast(qkl_ref[r:r + 1, QK_A:2 * QK_A])
        dec = jnp.exp(col_bcast(qkl_ref[r:r + 1, 2 * QK_A:3 * QK_A]))
        for h in range(GLA_HEADS):
            rows = slice(h * GLA_DK, (h + 1) * GLA_DK)
            cols = slice(h * GLA_DV, (h + 1) * GLA_DV)
            s_new = dec[rows] * s0_ref[r, rows, :] + k[rows] * v_ref[r:r + 1, cols]
            s1_ref[r, rows, :] = s_new
            o_h = jnp.sum(q[rows] * s_new, axis=0, keepdims=True)
            mix_ref[r:r + 1, cols] = _gla_finish(o_h, norm_g, sg_ref[r:r + 1, cols])


def _gla_sample(qkl, v, sg, norm_g, s0):
    m = qkl.shape[0]
    rb = GLA_SAMPLE_ROWS
    row = lambda w: pl.BlockSpec((rb, w), lambda i: (i, 0))
    st = pl.BlockSpec((rb, QK_A, GLA_DV), lambda i: (i, 0, 0))
    return pl.pallas_call(
        _gla_sample_kernel,
        grid=(m // rb,),
        in_specs=[row(3 * QK_A), row(V_A), row(V_A), _full((1, GLA_DV)), st],
        out_specs=[pl.BlockSpec((rb, V_A), lambda i: (i, 0)), st],
        out_shape=[jax.ShapeDtypeStruct((m, D_MODEL), F32),
                   jax.ShapeDtypeStruct((m, QK_A, GLA_DV), F32)],
        compiler_params=_params(1),
        name="gla_sample",
    )(qkl, v, sg, norm_g, s0)


def _pool_project(d_groups, wp_ref, scale_ref, mix_ref):
    for g in range(len(POOL_WINDOWS)):
        cols = slice(g * POOL_GROUP, (g + 1) * POOL_GROUP)
        mix_ref[:, cols] = _bdot(d_groups[g], wp_ref[g]) * scale_ref[:, cols]


def _pool_prompt_kernel(u_ref, prev_ref, wp_ref, scale_ref, mixin_ref, mix_ref):
    del mixin_ref
    t = pl.program_id(1)
    tm = u_ref.shape[0]
    hist = prev_ref.shape[0]
    prev = jnp.where(t > 0, prev_ref[...], 0.0)
    z = jnp.concatenate([prev, u_ref[...]], axis=0)
    pos = t * tm + lax.broadcasted_iota(jnp.int32, (tm, 1), 0)
    d_groups = []
    for g, w in enumerate(POOL_WINDOWS):
        cols = slice(g * POOL_GROUP, (g + 1) * POOL_GROUP)
        s = z[:, cols]
        shift = 1
        while shift < w:
            s = s + pltpu.roll(s, shift, 0)
            shift *= 2
        cnt = jnp.minimum(w, pos + 1).astype(F32)
        d_groups.append(s[hist:] / cnt - z[hist:, cols])
    _pool_project(d_groups, wp_ref, scale_ref, mix_ref)


def _pool_prompt(u, mix, w_pool, scale, batch, seq, tm):
    nt = seq // tm
    hist = 16
    assert hist > POOL_HIST and tm % hist == 0
    per = tm // hist
    return pl.pallas_call(
        _pool_prompt_kernel,
        grid=(batch, nt),
        in_specs=[pl.BlockSpec((tm, POOL_CH), lambda b, t: (b * nt + t, 0)),
                  pl.BlockSpec((hist, POOL_CH), lambda b, t: (jnp.maximum((b * nt + t) * per - 1, 0), 0)),
                  _full(w_pool.shape), _full((1, POOL_CH)),
                  pl.BlockSpec(memory_space=pl.ANY)],
        out_specs=pl.BlockSpec((tm, POOL_CH), lambda b, t: (b * nt + t, 1)),
        out_shape=jax.ShapeDtypeStruct(mix.shape, F32),
        input_output_aliases={4: 0},
        compiler_params=_params(2),
        name="pool_prompt",
    )(u, u, w_pool, scale, mix)


def _pool_sample_kernel(u_ref, st_ref, wp_ref, scale_ref, mixin_ref, mix_ref, st_out_ref):
    del mixin_ref
    u = u_ref[...]
    d_groups = []
    for g, w in enumerate(POOL_WINDOWS):
        cols = slice(g * POOL_GROUP, (g + 1) * POOL_GROUP)
        win = u[:, cols] + jnp.sum(st_ref[:, POOL_HIST - (w - 1):POOL_HIST, cols], axis=1)
        d_groups.append(win / float(w) - u[:, cols])
    _pool_project(d_groups, wp_ref, scale_ref, mix_ref)
    st_out_ref[:, 0:POOL_HIST - 1, :] = st_ref[:, 1:POOL_HIST, :]
    st_out_ref[:, POOL_HIST - 1:POOL_HIST, :] = u[:, None, :]


def _pool_sample(u, st, mix, w_pool, scale, pos0):
    m = u.shape[0]
    assert pos0 + 1 >= max(POOL_WINDOWS)
    return pl.pallas_call(
        _pool_sample_kernel,
        grid=(1,),
        in_specs=[_full((m, POOL_CH)), _full(st.shape), _full(w_pool.shape), _full((1, POOL_CH)),
                  pl.BlockSpec(memory_space=pl.ANY)],
        out_specs=[pl.BlockSpec((m, POOL_CH), lambda i: (0, 1)), _full(st.shape)],
        out_shape=[jax.ShapeDtypeStruct(mix.shape, F32), jax.ShapeDtypeStruct(st.shape, F32)],
        input_output_aliases={4: 0},
        compiler_params=_params(1),
        name="pool_sample",
    )(u, st, w_pool, scale, mix)


def _mix_kernel(x_ref, mix_ref, w_ref, lng_ref, lnb_ref, wr_ref, br_ref,
                x1_ref, x1b_ref, idx_ref, gate_ref, w_bf):
    @pl.when(pl.program_id(0) == 0)
    def _():
        w_bf[...] = w_ref[...].astype(BF16)

    h = jnp.dot(mix_ref[...].astype(BF16), w_bf[...], preferred_element_type=F32)
    x1 = _layer_norm(DEEPNORM_ALPHA * x_ref[...] + h, lng_ref[...], lnb_ref[...])
    x1_ref[...] = x1
    x1b_ref[...] = x1.astype(BF16)
    logits = jnp.dot(x1, wr_ref[...], precision=HI, preferred_element_type=F32) + br_ref[...]
    tm = logits.shape[0]
    lane = lax.broadcasted_iota(jnp.int32, (tm, N_EXPERTS), 1)
    lane_k = lax.broadcasted_iota(jnp.int32, (tm, TOP_K), 1)
    idx_out = jnp.zeros((tm, TOP_K), jnp.int32)
    val_out = jnp.zeros((tm, TOP_K), F32)
    cur = logits
    for kk in range(TOP_K):
        mval = jnp.max(cur, axis=1, keepdims=True)
        midx = jnp.min(jnp.where(cur == mval, lane.astype(F32), float(N_EXPERTS)), axis=1,
                       keepdims=True).astype(jnp.int32)
        idx_out = jnp.where(lane_k == kk, midx, idx_out)
        val_out = jnp.where(lane_k == kk, mval, val_out)
        cur = jnp.where(lane == midx, -jnp.inf, cur)
    ex = jnp.exp(val_out - val_out[:, 0:1])
    idx_ref[...] = idx_out
    gate_ref[...] = ex / jnp.sum(ex, axis=1, keepdims=True)


def _mix(x, mix, w_out, ln_g, ln_b, w_router, b_router, tm):
    m = x.shape[0]
    row = lambda w: pl.BlockSpec((tm, w), lambda i: (i, 0))
    return pl.pallas_call(
        _mix_kernel,
        grid=(m // tm,),
        in_specs=[row(D_MODEL), row(D_MODEL), _full((D_MODEL, D_MODEL)), _full((1, D_MODEL)),
                  _full((1, D_MODEL)), _full((D_MODEL, N_EXPERTS)), _full((1, N_EXPERTS))],
        out_specs=[row(D_MODEL), row(D_MODEL), row(TOP_K), row(TOP_K)],
        out_shape=[jax.ShapeDtypeStruct((m, D_MODEL), F32), jax.ShapeDtypeStruct((m, D_MODEL), BF16),
                   jax.ShapeDtypeStruct((m, TOP_K), jnp.int32), jax.ShapeDtypeStruct((m, TOP_K), F32)],
        scratch_shapes=[pltpu.VMEM((D_MODEL, D_MODEL), BF16)],
        compiler_params=_params(1),
        name="mix",
    )(x, mix, w_out, ln_g, ln_b, w_router, b_router)


def _rank_kernel(idx_ref, rank_ref, cnt_ref, carry):
    i = pl.program_id(0)

    @pl.when(i == 0)
    def _():
        carry[...] = jnp.zeros_like(carry)

    idx = idx_ref[...]
    tm = idx.shape[0]
    lane = lax.broadcasted_iota(jnp.int32, (tm, N_EXPERTS), 1)
    onehots = [(idx[:, kk:kk + 1] == lane) for kk in range(TOP_K)]
    member = sum(jnp.where(o, 1.0, 0.0) for o in onehots)
    r_i = lax.broadcasted_iota(jnp.int32, (tm, tm), 0)
    c_i = lax.broadcasted_iota(jnp.int32, (tm, tm), 1)
    strict_lower = jnp.where(c_i < r_i, 1.0, 0.0).astype(BF16)
    before = jnp.dot(strict_lower, member.astype(BF16), preferred_element_type=F32) + carry[...]
    lane_k = lax.broadcasted_iota(jnp.int32, (tm, TOP_K), 1)
    rank = jnp.zeros((tm, TOP_K), F32)
    for kk in range(TOP_K):
        r_k = jnp.sum(jnp.where(onehots[kk], before, 0.0), axis=1, keepdims=True)
        rank = jnp.where(lane_k == kk, r_k, rank)
    rank_ref[...] = rank.astype(jnp.int32)
    carry[...] = carry[...] + jnp.sum(member, axis=0, keepdims=True)
    cnt_ref[...] = carry[...].astype(jnp.int32)


def _rank(idx):
    n = idx.shape[0]
    tm = RANK_TILE
    return pl.pallas_call(
        _rank_kernel,
        grid=(n // tm,),
        in_specs=[pl.BlockSpec((tm, TOP_K), lambda i: (i, 0))],
        out_specs=[pl.BlockSpec((tm, TOP_K), lambda i: (i, 0)), _full((1, N_EXPERTS))],
        out_shape=[jax.ShapeDtypeStruct((n, TOP_K), jnp.int32),
                   jax.ShapeDtypeStruct((1, N_EXPERTS), jnp.int32)],
        scratch_shapes=[pltpu.VMEM((1, N_EXPERTS), F32)],
        compiler_params=_params(1),
        name="moe_rank",
    )(idx)


def _expert_kernel(be_ref, nu_ref, xs_ref, wgu_ref, bgu_ref, wd_ref, bd_ref, ys_ref, wgu_bf, wd_bf):
    blk = pl.program_id(0)
    prev = be_ref[jnp.maximum(blk - 1, 0)]
    fresh = jnp.logical_or(blk == 0, be_ref[blk] != prev)
    used = blk < nu_ref[0]

    @pl.when(jnp.logical_and(fresh, used))
    def _():
        wgu_bf[...] = wgu_ref[0].astype(BF16)
        wd_bf[...] = wd_ref[0].astype(BF16)

    @pl.when(used)
    def _():
        xb = xs_ref[...]
        acc = jnp.zeros(ys_ref.shape, F32) + bd_ref[0]
        half = D_EXPERT // 2
        for c in range(2):
            cg = slice(c * half, (c + 1) * half)
            cu = slice(D_EXPERT + c * half, D_EXPERT + (c + 1) * half)
            gate = jnp.dot(xb, wgu_bf[:, cg], preferred_element_type=F32) + bgu_ref[0, :, cg]
            up = jnp.dot(xb, wgu_bf[:, cu], preferred_element_type=F32) + bgu_ref[0, :, cu]
            gate = jnp.minimum(gate, SWIGLU_LIMIT)
            up = jnp.clip(up, -SWIGLU_LIMIT, SWIGLU_LIMIT)
            act = (up + 1.0) * gate * jax.nn.sigmoid(SWIGLU_ALPHA * gate)
            acc = acc + jnp.dot(act.astype(BF16), wd_bf[cg, :], preferred_element_type=F32)
        ys_ref[...] = acc

    @pl.when(jnp.logical_not(used))
    def _():
        ys_ref[...] = jnp.zeros_like(ys_ref)


def _experts(block_e, n_used, xs, w_gu, b_gu, w_down, b_down):
    p = xs.shape[0]
    bm = MOE_BLOCK
    nb = p // bm
    last = lambda blk, nu: jnp.minimum(blk, nu[0] - 1)
    grid_spec = pltpu.PrefetchScalarGridSpec(
        num_scalar_prefetch=2,
        grid=(nb,),
        in_specs=[pl.BlockSpec((bm, D_MODEL), lambda blk, be, nu: (last(blk, nu), 0)),
                  pl.BlockSpec((1, D_MODEL, 2 * D_EXPERT), lambda blk, be, nu: (be[blk], 0, 0)),
                  pl.BlockSpec((1, 1, 2 * D_EXPERT), lambda blk, be, nu: (be[blk], 0, 0)),
                  pl.BlockSpec((1, D_EXPERT, D_MODEL), lambda blk, be, nu: (be[blk], 0, 0)),
                  pl.BlockSpec((1, 1, D_MODEL), lambda blk, be, nu: (be[blk], 0, 0))],
        out_specs=pl.BlockSpec((bm, D_MODEL), lambda blk, be, nu: (blk, 0)),
        scratch_shapes=[pltpu.VMEM((D_MODEL, 2 * D_EXPERT), BF16), pltpu.VMEM((D_EXPERT, D_MODEL), BF16)],
    )
    return pl.pallas_call(
        _expert_kernel,
        grid_spec=grid_spec,
        out_shape=jax.ShapeDtypeStruct((p, D_MODEL), F32),
        compiler_params=_params(1),
        name="moe_experts",
    )(block_e, n_used, xs, w_gu, b_gu.reshape(N_EXPERTS, 1, 2 * D_EXPERT), w_down,
      b_down.reshape(N_EXPERTS, 1, D_MODEL))


def _finish_kernel(x1_ref, y0_ref, y1_ref, y2_ref, y3_ref, gate_ref, p_ref, lng_ref, lnb_ref,
                   wpg_ref, wpp_ref, out_ref, wpg_bf):
    @pl.when(pl.program_id(0) == 0)
    def _():
        wpg_bf[...] = wpg_ref[...].astype(BF16)

    gates = gate_ref[...]
    moe = y0_ref[...] * gates[:, 0:1]
    for kk, y_ref in enumerate((y1_ref, y2_ref, y3_ref), start=1):
        moe = moe + y_ref[...] * gates[:, kk:kk + 1]
    x2 = _layer_norm(DEEPNORM_ALPHA * x1_ref[...] + moe, lng_ref[...], lnb_ref[...])
    pg = jax.nn.sigmoid(jnp.dot(x2.astype(BF16), wpg_bf[...], preferred_element_type=F32))
    pp = _bdot(p_ref[...], wpp_ref[...])
    out_ref[...] = x2 + pg * pp


def _finish(x1, ysg, gates, p, ln_g, ln_b, w_pg, w_pp, tm):
    m = x1.shape[0]
    nt = m // tm
    row = lambda w: pl.BlockSpec((tm, w), lambda i: (i, 0))
    ple = p.shape[1]
    y_specs = [pl.BlockSpec((tm, D_MODEL), lambda i, kk=kk: (kk * nt + i, 0)) for kk in range(TOP_K)]
    return pl.pallas_call(
        _finish_kernel,
        grid=(nt,),
        in_specs=[row(D_MODEL)] + y_specs + [row(TOP_K), row(ple), _full((1, D_MODEL)), _full((1, D_MODEL)),
                                             _full((D_MODEL, D_MODEL)), _full((ple, D_MODEL))],
        out_specs=row(D_MODEL),
        out_shape=jax.ShapeDtypeStruct((m, D_MODEL), F32),
        scratch_shapes=[pltpu.VMEM((D_MODEL, D_MODEL), BF16)],
        compiler_params=_params(1),
        name="finish",
    )(x1, ysg, ysg, ysg, ysg, gates, p, ln_g, ln_b, w_pg, w_pp)


def _qkv_kernel(x_ref, w_ref, q_ref, k_ref, v_ref, w_bf):
    @pl.when(pl.program_id(0) == 0)
    def _():
        w_bf[...] = w_ref[...].astype(BF16)

    xb = x_ref[...].astype(BF16)
    q_ref[...] = jnp.dot(xb, w_bf[:, 0:D_MODEL], preferred_element_type=F32) * (MOBA_DH ** -0.5)
    k_ref[...] = jnp.dot(xb, w_bf[:, D_MODEL:2 * D_MODEL], preferred_element_type=F32)
    v_ref[...] = jnp.dot(xb, w_bf[:, 2 * D_MODEL:3 * D_MODEL], preferred_element_type=F32)


def _qkv(x, w_qkv, tm):
    m = x.shape[0]
    row = pl.BlockSpec((tm, D_MODEL), lambda i: (i, 0))
    shp = jax.ShapeDtypeStruct((m, D_MODEL), F32)
    return pl.pallas_call(
        _qkv_kernel,
        grid=(m // tm,),
        in_specs=[row, _full((D_MODEL, 3 * D_MODEL))],
        out_specs=[row, row, row],
        out_shape=[shp, shp, shp],
        scratch_shapes=[pltpu.VMEM((D_MODEL, 3 * D_MODEL), BF16)],
        compiler_params=_params(1),
        name="qkv",
    )(x, w_qkv)


def _topk_rows(gate_t, n_valid):
    nb, qn = gate_t.shape
    row = lax.broadcasted_iota(jnp.int32, (nb, qn), 0)
    valid = row < n_valid
    sel_rows = []
    for n in range(nb):
        g_n = gate_t[n:n + 1, :]
        beats = jnp.logical_or(gate_t > g_n, jnp.logical_and(gate_t == g_n, row < n))
        cnt = jnp.sum(jnp.where(jnp.logical_and(beats, valid), 1.0, 0.0), axis=0, keepdims=True)
        sel_rows.append(jnp.where(jnp.logical_and(cnt < MOBA_TOPK, n < n_valid), 1.0, 0.0))
    return jnp.concatenate(sel_rows, axis=0)


def _moba_prompt_kernel(q_ref, k_ref, v_ref, slope_ref, o_ref, qt_scr, vt_scr, kb_scr, km_scr, sel_scr):
    hp = pl.program_id(1)
    qi = pl.program_id(2)
    nb = qt_scr.shape[0]
    blk = MOBA_BLOCK
    pair = 2 * MOBA_DH

    @pl.when(qi == 0)
    def _():
        for n in range(nb):
            rows = slice(n * blk, (n + 1) * blk)
            qt_scr[n] = q_ref[rows, :].T.astype(BF16)
            vt_scr[n] = v_ref[rows, :].T.astype(BF16)
            kblk = k_ref[rows, :]
            kb_scr[rows, :] = kblk.astype(BF16)
            km_scr[n:n + 1, :] = jnp.mean(kblk, axis=0, keepdims=True)

    key_i = lax.broadcasted_iota(jnp.int32, (blk, blk), 0)
    qry_i = lax.broadcasted_iota(jnp.int32, (blk, blk), 1)
    dist0 = (qry_i - key_i).astype(F32)
    causal = key_i <= qry_i
    sub = lax.broadcasted_iota(jnp.int32, (pair, blk), 0)
    q_t = qt_scr[qi]
    own = pl.ds(pl.multiple_of(qi * blk, blk), blk)
    k_d = kb_scr[own, :]
    v_t_d = vt_scr[qi]
    q_f = q_ref[own, :]
    lane = lax.broadcasted_iota(jnp.int32, (nb, pair), 1)
    outs = []
    for hh in range(2):
        in_head = jnp.logical_and(sub >= hh * MOBA_DH, sub < (hh + 1) * MOBA_DH)
        q_h = jnp.where(in_head, q_t, jnp.zeros_like(q_t))
        slope = slope_ref[pl.ds(2 * hp + hh, 1), :]
        km_h = jnp.where(jnp.logical_and(lane >= hh * MOBA_DH, lane < (hh + 1) * MOBA_DH), km_scr[...], 0.0)
        gate_t = lax.dot_general(km_h, q_f, (((1,), (1,)), ((), ())),
                                 precision=HI, preferred_element_type=F32)
        sel_scr[...] = _topk_rows(gate_t, qi)
        slope_d = slope * dist0
        hrows = slice(hh * MOBA_DH, (hh + 1) * MOBA_DH)

        s = jnp.dot(k_d, q_h, preferred_element_type=F32) - slope_d
        s = jnp.where(causal, s, -jnp.inf)
        m0 = jnp.max(s, axis=0, keepdims=True)
        p = jnp.exp(s - m0)
        l0 = jnp.sum(p, axis=0, keepdims=True)
        acc0 = jnp.dot(v_t_d[hrows, :], p.astype(BF16), preferred_element_type=F32)

        def body(j, carry):
            m, l, acc = carry
            k_j = kb_scr[pl.ds(pl.multiple_of(j * blk, blk), blk), :]
            off = (jnp.zeros((1, blk), jnp.int32) + (qi - j) * blk).astype(F32)
            s = jnp.dot(k_j, q_h, preferred_element_type=F32) - slope_d - slope * off
            s = jnp.where(sel_scr[pl.ds(j, 1), :] > 0.0, s, -jnp.inf)
            m_new = jnp.maximum(m, jnp.max(s, axis=0, keepdims=True))
            alpha = jnp.exp(m - m_new)
            p = jnp.exp(s - m_new)
            l = alpha * l + jnp.sum(p, axis=0, keepdims=True)
            acc = alpha * acc + jnp.dot(vt_scr[j][hrows, :], p.astype(BF16), preferred_element_type=F32)
            return m_new, l, acc

        m, l, acc = lax.fori_loop(0, qi, body, (m0, l0, acc0))
        outs.append(acc / l)
    o_ref[...] = jnp.concatenate(outs, axis=0).T


def _moba_prompt(q, k, v, slopes, batch, seq):
    m = q.shape[0]
    nb = seq // MOBA_BLOCK
    pair = 2 * MOBA_DH
    n_pairs = MOBA_HEADS // 2
    slope_rows = jnp.broadcast_to(slopes[:, None], (MOBA_HEADS, MOBA_BLOCK))
    seq_spec = pl.BlockSpec((seq, pair), lambda b, hp, qi: (b, hp))
    return pl.pallas_call(
        _moba_prompt_kernel,
        grid=(batch, n_pairs, nb),
        in_specs=[seq_spec, seq_spec, seq_spec, _full((MOBA_HEADS, MOBA_BLOCK))],
        out_specs=pl.BlockSpec((MOBA_BLOCK, pair), lambda b, hp, qi: (b * nb + qi, hp)),
        out_shape=jax.ShapeDtypeStruct((m, D_MODEL), F32),
        scratch_shapes=[pltpu.VMEM((nb, pair, MOBA_BLOCK), BF16), pltpu.VMEM((nb, pair, MOBA_BLOCK), BF16),
                        pltpu.VMEM((seq, pair), BF16), pltpu.VMEM((nb, pair), F32),
                        pltpu.VMEM((nb, MOBA_BLOCK), F32)],
        compiler_params=_params(3),
        name="moba_prompt",
    )(q, k, v, slope_rows)


def _moba_sample_kernel(n_pages, pos0, pt_ref, q_ref, kn_ref, vn_ref, slope_ref, *refs):
    del pt_ref
    k_pages = refs[:n_pages]
    v_pages = refs[n_pages:2 * n_pages]
    o_ref = refs[2 * n_pages]
    per_blk = MOBA_BLOCK // PAGE_SIZE
    n_blk = n_pages // per_blk
    q = q_ref[0]
    d_i = lax.broadcasted_iota(jnp.int32, (D_MODEL, MOBA_HEADS), 0)
    h_i = lax.broadcasted_iota(jnp.int32, (D_MODEL, MOBA_HEADS), 1)
    seg = jnp.where(d_i // MOBA_DH == h_i, 1.0, 0.0)
    d_t = lax.broadcasted_iota(jnp.int32, (MOBA_HEADS, D_MODEL), 1)
    h_t = lax.broadcasted_iota(jnp.int32, (MOBA_HEADS, D_MODEL), 0)
    seg_t = jnp.where(d_t // MOBA_DH == h_t, 1.0, 0.0)
    slope = slope_ref[...]

    means = []
    for n in range(n_blk):
        tot = sum(jnp.sum(k_pages[n * per_blk + i][...], axis=0, keepdims=True) for i in range(per_blk))
        means.append(tot / float(MOBA_BLOCK))
    means = jnp.concatenate(means, axis=0)
    gate = jnp.dot(means * q, seg, precision=HI, preferred_element_type=F32)
    sel = _topk_rows(gate, n_blk)

    row = lax.broadcasted_iota(jnp.int32, (PAGE_SIZE, MOBA_HEADS), 0)
    scores = []
    for pg in range(n_pages):
        s = _bdot(k_pages[pg][...] * q, seg)
        dist = (pos0 - pg * PAGE_SIZE - row).astype(F32)
        s = s - slope * dist
        n = pg // per_blk
        scores.append(jnp.where(sel[n:n + 1, :] > 0.0, s, -jnp.inf))
    s_new = _bdot(kn_ref[0] * q, seg)
    m = s_new
    for s in scores:
        m = jnp.maximum(m, jnp.max(s, axis=0, keepdims=True))
    p_new = jnp.exp(s_new - m)
    l = p_new
    acc = jnp.zeros((8, D_MODEL), F32)
    for pg in range(n_pages):
        p = jnp.exp(scores[pg] - m)
        l = l + jnp.sum(p, axis=0, keepdims=True)
        pv = _bdot(p, seg_t) * v_pages[pg][...]
        acc = acc + jnp.sum(pv.reshape(PAGE_SIZE // 8, 8, D_MODEL), axis=0)
    o = jnp.sum(acc, axis=0, keepdims=True) + _bdot(p_new, seg_t) * vn_ref[0]
    o_ref[0] = o / jnp.dot(l, seg_t, precision=HI, preferred_element_type=F32)


def _moba_sample(q, k_new, v_new, cache_k, cache_v, page_table, slopes, pos0):
    m = q.shape[0]
    n_pages = page_table.shape[1]
    assert pos0 == n_pages * PAGE_SIZE and pos0 % MOBA_BLOCK == 0
    n_phys = cache_k.shape[0]
    ck = cache_k.reshape(n_phys, PAGE_SIZE, D_MODEL)
    cv = cache_v.reshape(n_phys, PAGE_SIZE, D_MODEL)
    vec = pl.BlockSpec((1, 1, D_MODEL), lambda b, pt: (b, 0, 0))
    page_specs = [pl.BlockSpec((None, PAGE_SIZE, D_MODEL), lambda b, pt, pg=pg: (pt[b, pg], 0, 0))
                  for pg in range(n_pages)]
    grid_spec = pltpu.PrefetchScalarGridSpec(
        num_scalar_prefetch=1,
        grid=(m,),
        in_specs=[vec, vec, vec, pl.BlockSpec((1, MOBA_HEADS), lambda b, pt: (0, 0))] + page_specs + page_specs,
        out_specs=vec,
    )
    out = pl.pallas_call(
        functools.partial(_moba_sample_kernel, n_pages, pos0),
        grid_spec=grid_spec,
        out_shape=jax.ShapeDtypeStruct((m, 1, D_MODEL), F32),
        compiler_params=_params(1),
        name="moba_sample",
    )(page_table, q.reshape(m, 1, D_MODEL), k_new.reshape(m, 1, D_MODEL), v_new.reshape(m, 1, D_MODEL),
      slopes.reshape(1, MOBA_HEADS), *([ck] * n_pages), *([cv] * n_pages))
    return out.reshape(m, D_MODEL)


def _layer_tail(xp, xs, mix_p, mix_s, p_p, p_s, w_out, ln_g, ln_b, w_router, b_router,
                w_gu, b_gu, w_down, b_down, w_pg, w_pp):
    n_p, n_s = xp.shape[0], xs.shape[0]
    lg0, lb0 = ln_g[0:1], ln_b[0:1]
    lg1, lb1 = ln_g[1:2], ln_b[1:2]
    br = b_router.reshape(1, N_EXPERTS)
    x1p, x1bp, idxp, gatep = _mix(xp, mix_p, w_out, lg0, lb0, w_router, br, ROW_TILE)
    x1s, x1bs, idxs, gates = _mix(xs, mix_s, w_out, lg0, lb0, w_router, br, n_s)

    idx = jnp.concatenate([idxp, idxs], axis=0)
    rank, counts = _rank(idx)
    counts = counts[0]
    bm = MOE_BLOCK
    n_tok = n_p + n_s
    n_blocks = (n_tok * TOP_K + N_EXPERTS * (bm - 1)) // bm
    padded = (counts + bm - 1) // bm * bm
    pend = jnp.cumsum(padded)
    pstart = pend - padded
    dest = pstart[idx] + rank
    n_used = (pend[-1] // bm).astype(jnp.int32)
    blk_ids = jnp.minimum(jnp.arange(n_blocks, dtype=jnp.int32), n_used - 1)
    block_e = jnp.minimum(jnp.searchsorted(pend, blk_ids * bm, side='right'), N_EXPERTS - 1).astype(jnp.int32)
    tok = jnp.broadcast_to(jnp.arange(n_tok, dtype=jnp.int32)[:, None], (n_tok, TOP_K))
    row_tok = jnp.zeros((n_blocks * bm,), jnp.int32).at[dest.reshape(-1)].set(tok.reshape(-1))
    x1b = jnp.concatenate([x1bp, x1bs], axis=0)
    xs_rows = jnp.take(x1b, row_tok, axis=0)
    ys = _experts(block_e, n_used.reshape(1), xs_rows, w_gu, b_gu, w_down, b_down)
    ysg_p = jnp.take(ys, dest[:n_p].T.reshape(-1), axis=0)
    ysg_s = jnp.take(ys, dest[n_p:].T.reshape(-1), axis=0)
    yp = _finish(x1p, ysg_p, gatep, p_p, lg1, lb1, w_pg, w_pp, ROW_TILE)
    ys_out = _finish(x1s, ysg_s, gates, p_s, lg1, lb1, w_pg, w_pp, n_s)
    return yp, ys_out


def kernel(x_prompt, x_sample, state_gla, state_pool, cache_k, cache_v, page_table, p_prompt, p_sample, w_in_ab, gla_w_alpha, gla_b_alpha, gla_norm_g, pool_w, pool_scale, w_out_ab, w_qkv_c, w_out_c, ln_g, ln_b, moe_w_router, moe_b_router, moe_w_gu, moe_b_gu, moe_w_down, moe_b_down, ple_w_gate, ple_w_proj):
    batch, seq, _ = x_prompt.shape
    n_s = x_sample.shape[0]
    n_p = batch * seq
    pos0 = page_table.shape[1] * PAGE_SIZE
    xp = x_prompt.reshape(n_p, D_MODEL)
    xs = x_sample.reshape(n_s, D_MODEL)

    def tail(i, xp, xs, mix_p, mix_s, w_out):
        return _layer_tail(xp, xs, mix_p, mix_s, p_prompt[i].reshape(n_p, -1), p_sample[i].reshape(n_s, -1),
                           w_out, ln_g[i], ln_b[i], moe_w_router[i], moe_b_router[i], moe_w_gu[i],
                           moe_b_gu[i], moe_w_down[i], moe_b_down[i], ple_w_gate[i], ple_w_proj[i])

    w_in = w_in_ab[0]
    c_a = 2 * QK_A + 2 * V_A
    w_main = jnp.concatenate([w_in[:, :c_a], w_in[:, c_a + GLA_LOWRANK:]], axis=1)
    w_a = w_in[:, c_a:c_a + GLA_LOWRANK]
    b_alpha = gla_b_alpha[0].reshape(1, QK_A)
    norm_g = gla_norm_g[0].reshape(1, GLA_DV)
    scale = pool_scale[0].reshape(1, POOL_CH)
    qkl_p, v_p, sg_p, u_p = _proj_ab(xp, w_main, w_a, gla_w_alpha[0], b_alpha, ROW_TILE)
    qkl_s, v_s, sg_s, u_s = _proj_ab(xs, w_main, w_a, gla_w_alpha[0], b_alpha, n_s)
    mix_p, gla_p = _gla_prompt(qkl_p, v_p, sg_p, norm_g, batch, seq, ROW_TILE)
    mix_p = _pool_prompt(u_p, mix_p, pool_w[0], scale, batch, seq, ROW_TILE)
    mix_s, gla_s = _gla_sample(qkl_s, v_s, sg_s, norm_g, state_gla[0].reshape(n_s, QK_A, GLA_DV))
    mix_s, pool_s = _pool_sample(u_s, state_pool[0], mix_s, pool_w[0], scale, pos0)
    pool_p = u_p.reshape(batch, seq, POOL_CH)[:, seq - POOL_HIST:]
    xp, xs = tail(0, xp, xs, mix_p, mix_s, w_out_ab[0])

    slopes = jnp.exp2(-8.0 * jnp.arange(1, MOBA_HEADS + 1, dtype=F32) / MOBA_HEADS)
    q_p, k_p, v_p2 = _qkv(xp, w_qkv_c[0], ROW_TILE)
    q_s, k_s, v_s2 = _qkv(xs, w_qkv_c[0], n_s)
    o_p = _moba_prompt(q_p, k_p, v_p2, slopes, batch, seq)
    o_s = _moba_sample(q_s, k_s, v_s2, cache_k[0], cache_v[0], page_table, slopes, pos0)
    xp, xs = tail(1, xp, xs, o_p, o_s, w_out_c[0])

    hd = (MOBA_HEADS, MOBA_DH)
    return (xp.reshape(batch, seq, D_MODEL), xs.reshape(n_s, 1, D_MODEL),
            gla_p.reshape(1, batch, GLA_HEADS, GLA_DK, GLA_DV), gla_s.reshape(1, n_s, GLA_HEADS, GLA_DK, GLA_DV),
            pool_p[None], pool_s[None],
            k_p.reshape(1, batch, seq, *hd), v_p2.reshape(1, batch, seq, *hd),
            k_s.reshape(1, n_s, 1, *hd), v_s2.reshape(1, n_s, 1, *hd))
```

```python
import functools
import math

import numpy as np
import jax
import jax.numpy as jnp
from jax import lax
from jax.experimental import pallas as pl
from jax.experimental.pallas import tpu as pltpu

F32 = jnp.float32
BF16 = jnp.bfloat16
HI = lax.Precision.HIGHEST

D_MODEL = 1024
GLA_HEADS = 4
GLA_DK = 64
GLA_DV = 128
GLA_LOWRANK = 16
GLA_TAU = 16.0
QK_A = GLA_HEADS * GLA_DK
V_A = GLA_HEADS * GLA_DV
POOL_WINDOWS = (2, 4, 8, 16)
POOL_CH = 512
POOL_GROUP = 128
POOL_HIST = 15
MOBA_HEADS = 16
MOBA_DH = 64
MOBA_BLOCK = 256
MOBA_TOPK = 3
PAGE_SIZE = 128
N_EXPERTS = 32
TOP_K = 4
D_EXPERT = 1024
SWIGLU_LIMIT = 7.0
SWIGLU_ALPHA = 1.702
DEPTH = 2
DEEPNORM_ALPHA = (2 * DEPTH) ** 0.25
LN_EPS = 1e-5

ROW_TILE = 512
GLA_CHUNK = 128
GLA_LEVELS = 7
MOE_BLOCK = 512
RANK_TILE = 384
VMEM_LIMIT = 56 * 1024 * 1024


def _params(n_axes, vmem=VMEM_LIMIT):
    return pltpu.CompilerParams(dimension_semantics=("arbitrary",) * n_axes, vmem_limit_bytes=vmem)


def _bdot(a, b):
    return jnp.dot(a.astype(BF16), b.astype(BF16), preferred_element_type=F32)


def _layer_norm(y, g, b):
    mu = jnp.mean(y, axis=-1, keepdims=True)
    yc = y - mu
    var = jnp.mean(yc * yc, axis=-1, keepdims=True)
    return yc * lax.rsqrt(var + LN_EPS) * g + b


def _full(shape):
    n = len(shape)
    return pl.BlockSpec(shape, lambda *_: (0,) * n)


def _proj_ab_kernel(x_ref, wm_ref, wa_ref, walpha_ref, balpha_ref,
                    qkl_ref, v_ref, sg_ref, u_ref, wm_bf, wa_bf):
    @pl.when(pl.program_id(0) == 0)
    def _():
        wm_bf[...] = wm_ref[...].astype(BF16)
        wa_bf[...] = wa_ref[...].astype(BF16)

    xb = x_ref[...].astype(BF16)
    qk = jnp.dot(xb, wm_bf[:, 0:2 * QK_A], preferred_element_type=F32)
    qkl_ref[:, 0:QK_A] = qk[:, 0:QK_A] * (GLA_DK ** -0.5)
    qkl_ref[:, QK_A:2 * QK_A] = qk[:, QK_A:2 * QK_A]
    a_lr = jnp.dot(xb, wa_bf[...], preferred_element_type=F32)
    z = _bdot(a_lr, walpha_ref[...]) + balpha_ref[...]
    log_sig = jnp.minimum(z, 0.0) - jnp.log1p(jnp.exp(-jnp.abs(z)))
    qkl_ref[:, 2 * QK_A:3 * QK_A] = log_sig / GLA_TAU
    c0 = 2 * QK_A
    v_ref[...] = jnp.dot(xb, wm_bf[:, c0:c0 + V_A], preferred_element_type=F32)
    g = jnp.dot(xb, wm_bf[:, c0 + V_A:c0 + 2 * V_A], preferred_element_type=F32)
    sg_ref[...] = g * jax.nn.sigmoid(g)
    u_ref[...] = jnp.dot(xb, wm_bf[:, c0 + 2 * V_A:c0 + 2 * V_A + POOL_CH], preferred_element_type=F32)


def _proj_ab(x, w_main, w_a, w_alpha, b_alpha, tm):
    m = x.shape[0]
    nmain = w_main.shape[1]
    row = lambda w: pl.BlockSpec((tm, w), lambda i: (i, 0))
    return pl.pallas_call(
        _proj_ab_kernel,
        grid=(m // tm,),
        in_specs=[row(D_MODEL), _full((D_MODEL, nmain)), _full((D_MODEL, GLA_LOWRANK)),
                  _full((GLA_LOWRANK, QK_A)), _full((1, QK_A))],
        out_specs=[row(3 * QK_A), row(V_A), row(V_A), row(POOL_CH)],
        out_shape=[jax.ShapeDtypeStruct((m, 3 * QK_A), F32), jax.ShapeDtypeStruct((m, V_A), F32),
                   jax.ShapeDtypeStruct((m, V_A), F32), jax.ShapeDtypeStruct((m, POOL_CH), F32)],
        scratch_shapes=[pltpu.VMEM((D_MODEL, nmain), BF16), pltpu.VMEM((D_MODEL, GLA_LOWRANK), BF16)],
        compiler_params=_params(1),
        name="proj_ab",
    )(x, w_main, w_a, w_alpha, b_alpha)


def _gla_tables():
    c = GLA_CHUNK
    i = np.arange(c)[:, None]
    s = np.arange(c)[None, :]
    mats = [(s <= i), (s > i)]
    for lev in range(GLA_LEVELS):
        p = GLA_LEVELS - 1 - lev
        half = 1 << p
        start = (i >> (p + 1)) << (p + 1)
        mid = start + half - 1
        upper = i >= start + half
        mats.append(np.where(upper, (s > mid) & (s <= i), (s > i) & (s <= mid)))
    seg = np.concatenate(mats, axis=0).astype(np.float32)
    j = np.arange(c)[None, :]
    x = i ^ j
    lvl = np.full((c, c), GLA_LEVELS + 1, np.int32)
    lvl[np.arange(c), np.arange(c)] = GLA_LEVELS
    for lev in range(GLA_LEVELS):
        p = GLA_LEVELS - 1 - lev
        lvl = np.where(((x >> p) == 1) & (((i >> p) & 1) == 1), lev, lvl)
    lvl4 = np.tile(lvl, (GLA_HEADS, 1)).astype(np.int32)
    lane_head = (np.arange(QK_A) // GLA_DK)[None, :]
    row_head = (np.arange(GLA_HEADS * c) // c)[:, None]
    hm4 = (lane_head == row_head).astype(np.float32)
    return seg, lvl4, hm4


def _gla_chunk(q, k, la, v, s_all, seg, lvl4, hm4):
    c = GLA_CHUNK
    e = jnp.dot(seg, la, precision=HI, preferred_element_type=F32)
    w = jnp.exp(e)
    w_b = w[0:c]
    w_k = w[c:2 * c]

    def stack_heads(t):
        return (jnp.concatenate([t] * GLA_HEADS, axis=0) * hm4).astype(BF16)

    o_inter = jnp.dot(stack_heads(q * w_b), s_all.astype(BF16), preferred_element_type=F32)
    a = jnp.zeros((GLA_HEADS * c, c), F32)
    for lev in range(GLA_LEVELS + 1):
        if lev < GLA_LEVELS:
            w_l = w[(2 + lev) * c:(3 + lev) * c]
            ql, kl = q * w_l, k * w_l
        else:
            ql, kl = q, k
        p_l = lax.dot_general(stack_heads(ql), kl.astype(BF16), (((1,), (1,)), ((), ())),
                              preferred_element_type=F32)
        a = jnp.where(lvl4 == lev, p_l, a)
    a = a.astype(BF16)
    outs = []
    for h in range(GLA_HEADS):
        v_h = v[:, h * GLA_DV:(h + 1) * GLA_DV].astype(BF16)
        o_h = o_inter[h * c:(h + 1) * c] + jnp.dot(a[h * c:(h + 1) * c], v_h, preferred_element_type=F32)
        outs.append(o_h)
    ks_t = (k * w_k).T.astype(BF16)
    kv = jnp.dot(ks_t, v.astype(BF16), preferred_element_type=F32)
    dec = jnp.exp(jnp.sum(la.T, axis=1, keepdims=True))
    new_rows = []
    for h in range(GLA_HEADS):
        rows = slice(h * GLA_DK, (h + 1) * GLA_DK)
        new_rows.append(dec[rows] * s_all[rows] + kv[rows, h * GLA_DV:(h + 1) * GLA_DV])
    return outs, jnp.concatenate(new_rows, axis=0)


def _gla_finish(o_h, norm_g, sg_h):
    o_h = o_h * lax.rsqrt(jnp.mean(o_h * o_h, axis=-1, keepdims=True) + LN_EPS) * norm_g
    return o_h * sg_h


def _gla_prompt_kernel(qkl_ref, v_ref, sg_ref, ng_ref, seg_ref, lvl_ref, hm_ref,
                       mix_ref, state_ref, s_scr):
    t = pl.program_id(1)

    @pl.when(t == 0)
    def _():
        s_scr[...] = jnp.zeros_like(s_scr)

    seg = seg_ref[...]
    lvl4 = lvl_ref[...]
    hm4 = hm_ref[...]
    norm_g = ng_ref[...]
    n_chunks = qkl_ref.shape[0] // GLA_CHUNK

    def body(ci, carry):
        r0 = pl.multiple_of(ci * GLA_CHUNK, GLA_CHUNK)
        rows = pl.ds(r0, GLA_CHUNK)
        q = qkl_ref[rows, 0:QK_A]
        k = qkl_ref[rows, QK_A:2 * QK_A]
        la = qkl_ref[rows, 2 * QK_A:3 * QK_A]
        v = v_ref[rows, :]
        outs, s_new = _gla_chunk(q, k, la, v, s_scr[...], seg, lvl4, hm4)
        s_scr[...] = s_new
        for h in range(GLA_HEADS):
            cols = slice(h * GLA_DV, (h + 1) * GLA_DV)
            mix_ref[rows, cols] = _gla_finish(outs[h], norm_g, sg_ref[rows, cols])
        return carry

    lax.fori_loop(0, n_chunks, body, 0)

    @pl.when(t == pl.num_programs(1) - 1)
    def _():
        state_ref[0] = s_scr[...]


def _gla_prompt(qkl, v, sg, norm_g, batch, seq, tm):
    m = qkl.shape[0]
    nt = seq // tm
    seg, lvl4, hm4 = _gla_tables()
    row = lambda w: pl.BlockSpec((tm, w), lambda b, t: (b * nt + t, 0))
    return pl.pallas_call(
        _gla_prompt_kernel,
        grid=(batch, nt),
        in_specs=[row(3 * QK_A), row(V_A), row(V_A), _full((1, GLA_DV)),
                  _full(seg.shape), _full(lvl4.shape), _full(hm4.shape)],
        out_specs=[pl.BlockSpec((tm, V_A), lambda b, t: (b * nt + t, 0)),
                   pl.BlockSpec((1, QK_A, GLA_DV), lambda b, t: (b, 0, 0))],
        out_shape=[jax.ShapeDtypeStruct((m, D_MODEL), F32),
                   jax.ShapeDtypeStruct((batch, QK_A, GLA_DV), F32)],
        scratch_shapes=[pltpu.VMEM((QK_A, GLA_DV), F32)],
        compiler_params=_params(2),
        name="gla_prompt",
    )(qkl, v, sg, norm_g, jnp.asarray(seg), jnp.asarray(lvl4), jnp.asarray(hm4))


GLA_SAMPLE_ROWS = 8


def _gla_sample_kernel(qkl_ref, v_ref, sg_ref, ng_ref, s0_ref, mix_ref, s1_ref):
    ones = jnp.ones((QK_A, GLA_DV), F32)
    r_i = lax.broadcasted_iota(jnp.int32, (QK_A, QK_A), 0)
    c_i = lax.broadcasted_iota(jnp.int32, (QK_A, QK_A), 1)
    eye = r_i == c_i
    norm_g = ng_ref[...]

    def col_bcast(row):
        diag = jnp.where(eye, jnp.broadcast_to(row, (QK_A, QK_A)), 0.0)
        return jnp.dot(diag, ones, precision=HI, preferred_element_type=F32)

    for r in range(GLA_SAMPLE_ROWS):
        q = col_bcast(qkl_ref[r:r + 1, 0:QK_A])
        k = col_bcast(qkl_ref[r:r + 1, QK_A:2 * QK_A])
        dec = jnp.exp(col_bcast(qkl_ref[r:r + 1, 2 * QK_A:3 * QK_A]))
        for h in range(GLA_HEADS):
            rows = slice(h * GLA_DK, (h + 1) * GLA_DK)
            cols = slice(h * GLA_DV, (h + 1) * GLA_DV)
            s_new = dec[rows] * s0_ref[r, rows, :] + k[rows] * v_ref[r:r + 1, cols]
            s1_ref[r, rows, :] = s_new
            o_h = jnp.sum(q[rows] * s_new, axis=0, keepdims=True)
            mix_ref[r:r + 1, cols] = _gla_finish(o_h, norm_g, sg_ref[r:r + 1, cols])


def _gla_sample(qkl, v, sg, norm_g, s0):
    m = qkl.shape[0]
    rb = GLA_SAMPLE_ROWS
    row = lambda w: pl.BlockSpec((rb, w), lambda i: (i, 0))
    st = pl.BlockSpec((rb, QK_A, GLA_DV), lambda i: (i, 0, 0))
    return pl.pallas_call(
        _gla_sample_kernel,
        grid=(m // rb,),
        in_specs=[row(3 * QK_A), row(V_A), row(V_A), _full((1, GLA_DV)), st],
        out_specs=[pl.BlockSpec((rb, V_A), lambda i: (i, 0)), st],
        out_shape=[jax.ShapeDtypeStruct((m, D_MODEL), F32),
                   jax.ShapeDtypeStruct((m, QK_A, GLA_DV), F32)],
        compiler_params=_params(1),
        name="gla_sample",
    )(qkl, v, sg, norm_g, s0)


def _pool_project(d_groups, wp_ref, scale_ref, mix_ref):
    for g in range(len(POOL_WINDOWS)):
        cols = slice(g * POOL_GROUP, (g + 1) * POOL_GROUP)
        mix_ref[:, cols] = _bdot(d_groups[g], wp_ref[g]) * scale_ref[:, cols]


def _pool_prompt_kernel(u_ref, prev_ref, wp_ref, scale_ref, mixin_ref, mix_ref):
    del mixin_ref
    t = pl.program_id(1)
    tm = u_ref.shape[0]
    hist = prev_ref.shape[0]
    prev = jnp.where(t > 0, prev_ref[...], 0.0)
    z = jnp.concatenate([prev, u_ref[...]], axis=0)
    pos = t * tm + lax.broadcasted_iota(jnp.int32, (tm, 1), 0)
    d_groups = []
    for g, w in enumerate(POOL_WINDOWS):
        cols = slice(g * POOL_GROUP, (g + 1) * POOL_GROUP)
        s = z[:, cols]
        shift = 1
        while shift < w:
            s = s + pltpu.roll(s, shift, 0)
            shift *= 2
        cnt = jnp.minimum(w, pos + 1).astype(F32)
        d_groups.append(s[hist:] / cnt - z[hist:, cols])
    _pool_project(d_groups, wp_ref, scale_ref, mix_ref)


def _pool_prompt(u, mix, w_pool, scale, batch, seq, tm):
    nt = seq // tm
    hist = 16
    assert hist > POOL_HIST and tm % hist == 0
    per = tm // hist
    return pl.pallas_call(
        _pool_prompt_kernel,
        grid=(batch, nt),
        in_specs=[pl.BlockSpec((tm, POOL_CH), lambda b, t: (b * nt + t, 0)),
                  pl.BlockSpec((hist, POOL_CH), lambda b, t: (jnp.maximum((b * nt + t) * per - 1, 0), 0)),
                  _full(w_pool.shape), _full((1, POOL_CH)),
                  pl.BlockSpec(memory_space=pl.ANY)],
        out_specs=pl.BlockSpec((tm, POOL_CH), lambda b, t: (b * nt + t, 1)),
        out_shape=jax.ShapeDtypeStruct(mix.shape, F32),
        input_output_aliases={4: 0},
        compiler_params=_params(2),
        name="pool_prompt",
    )(u, u, w_pool, scale, mix)


def _pool_sample_kernel(u_ref, st_ref, wp_ref, scale_ref, mixin_ref, mix_ref, st_out_ref):
    del mixin_ref
    u = u_ref[...]
    d_groups = []
    for g, w in enumerate(POOL_WINDOWS):
        cols = slice(g * POOL_GROUP, (g + 1) * POOL_GROUP)
        win = u[:, cols] + jnp.sum(st_ref[:, POOL_HIST - (w - 1):POOL_HIST, cols], axis=1)
        d_groups.append(win / float(w) - u[:, cols])
    _pool_project(d_groups, wp_ref, scale_ref, mix_ref)
    st_out_ref[:, 0:POOL_HIST - 1, :] = st_ref[:, 1:POOL_HIST, :]
    st_out_ref[:, POOL_HIST - 1:POOL_HIST, :] = u[:, None, :]


def _pool_sample(u, st, mix, w_pool, scale, pos0):
    m = u.shape[0]
    assert pos0 + 1 >= max(POOL_WINDOWS)
    return pl.pallas_call(
        _pool_sample_kernel,
        grid=(1,),
        in_specs=[_full((m, POOL_CH)), _full(st.shape), _full(w_pool.shape), _full((1, POOL_CH)),
                  pl.BlockSpec(memory_space=pl.ANY)],
        out_specs=[pl.BlockSpec((m, POOL_CH), lambda i: (0, 1)), _full(st.shape)],
        out_shape=[jax.ShapeDtypeStruct(mix.shape, F32), jax.ShapeDtypeStruct(st.shape, F32)],
        input_output_aliases={4: 0},
        compiler_params=_params(1),
        name="pool_sample",
    )(u, st, w_pool, scale, mix)


def _mix_kernel(x_ref, mix_ref, w_ref, lng_ref, lnb_ref, wr_ref, br_ref,
                x1_ref, idx_ref, gate_ref, w_bf):
    @pl.when(pl.program_id(0) == 0)
    def _():
        w_bf[...] = w_ref[...].astype(BF16)

    h = jnp.dot(mix_ref[...].astype(BF16), w_bf[...], preferred_element_type=F32)
    x1 = _layer_norm(DEEPNORM_ALPHA * x_ref[...] + h, lng_ref[...], lnb_ref[...])
    x1_ref[...] = x1
    logits = _bdot(x1, wr_ref[...]) + br_ref[...]
    tm = logits.shape[0]
    lane = lax.broadcasted_iota(jnp.int32, (tm, N_EXPERTS), 1)
    lane_k = lax.broadcasted_iota(jnp.int32, (tm, TOP_K), 1)
    idx_out = jnp.zeros((tm, TOP_K), jnp.int32)
    val_out = jnp.zeros((tm, TOP_K), F32)
    cur = logits
    for kk in range(TOP_K):
        mval = jnp.max(cur, axis=1, keepdims=True)
        midx = jnp.min(jnp.where(cur == mval, lane.astype(F32), float(N_EXPERTS)), axis=1,
                       keepdims=True).astype(jnp.int32)
        idx_out = jnp.where(lane_k == kk, midx, idx_out)
        val_out = jnp.where(lane_k == kk, mval, val_out)
        cur = jnp.where(lane == midx, -jnp.inf, cur)
    ex = jnp.exp(val_out - val_out[:, 0:1])
    idx_ref[...] = idx_out
    gate_ref[...] = ex / jnp.sum(ex, axis=1, keepdims=True)


def _mix(x, mix, w_out, ln_g, ln_b, w_router, b_router, tm):
    m = x.shape[0]
    row = lambda w: pl.BlockSpec((tm, w), lambda i: (i, 0))
    return pl.pallas_call(
        _mix_kernel,
        grid=(m // tm,),
        in_specs=[row(D_MODEL), row(D_MODEL), _full((D_MODEL, D_MODEL)), _full((1, D_MODEL)),
                  _full((1, D_MODEL)), _full((D_MODEL, N_EXPERTS)), _full((1, N_EXPERTS))],
        out_specs=[row(D_MODEL), row(TOP_K), row(TOP_K)],
        out_shape=[jax.ShapeDtypeStruct((m, D_MODEL), F32),
                   jax.ShapeDtypeStruct((m, TOP_K), jnp.int32), jax.ShapeDtypeStruct((m, TOP_K), F32)],
        scratch_shapes=[pltpu.VMEM((D_MODEL, D_MODEL), BF16)],
        compiler_params=_params(1),
        name="mix",
    )(x, mix, w_out, ln_g, ln_b, w_router, b_router)


def _rank_kernel(idx_ref, rank_ref, cnt_ref, carry):
    i = pl.program_id(0)

    @pl.when(i == 0)
    def _():
        carry[...] = jnp.zeros_like(carry)

    idx = idx_ref[...]
    tm = idx.shape[0]
    lane = lax.broadcasted_iota(jnp.int32, (tm, N_EXPERTS), 1)
    onehots = [(idx[:, kk:kk + 1] == lane) for kk in range(TOP_K)]
    member = sum(jnp.where(o, 1.0, 0.0) for o in onehots)
    r_i = lax.broadcasted_iota(jnp.int32, (tm, tm), 0)
    c_i = lax.broadcasted_iota(jnp.int32, (tm, tm), 1)
    strict_lower = jnp.where(c_i < r_i, 1.0, 0.0).astype(BF16)
    before = jnp.dot(strict_lower, member.astype(BF16), preferred_element_type=F32) + carry[...]
    lane_k = lax.broadcasted_iota(jnp.int32, (tm, TOP_K), 1)
    rank = jnp.zeros((tm, TOP_K), F32)
    for kk in range(TOP_K):
        r_k = jnp.sum(jnp.where(onehots[kk], before, 0.0), axis=1, keepdims=True)
        rank = jnp.where(lane_k == kk, r_k, rank)
    rank_ref[...] = rank.astype(jnp.int32)
    carry[...] = carry[...] + jnp.sum(member, axis=0, keepdims=True)
    cnt_ref[...] = carry[...].astype(jnp.int32)


def _rank(idx):
    n = idx.shape[0]
    tm = RANK_TILE
    return pl.pallas_call(
        _rank_kernel,
        grid=(n // tm,),
        in_specs=[pl.BlockSpec((tm, TOP_K), lambda i: (i, 0))],
        out_specs=[pl.BlockSpec((tm, TOP_K), lambda i: (i, 0)), _full((1, N_EXPERTS))],
        out_shape=[jax.ShapeDtypeStruct((n, TOP_K), jnp.int32),
                   jax.ShapeDtypeStruct((1, N_EXPERTS), jnp.int32)],
        scratch_shapes=[pltpu.VMEM((1, N_EXPERTS), F32)],
        compiler_params=_params(1),
        name="moe_rank",
    )(idx)


def _expert_kernel(be_ref, nu_ref, xs_ref, wgu_ref, bgu_ref, wd_ref, bd_ref, ys_ref, wgu_bf, wd_bf):
    blk = pl.program_id(0)
    prev = be_ref[jnp.maximum(blk - 1, 0)]
    fresh = jnp.logical_or(blk == 0, be_ref[blk] != prev)
    used = blk < nu_ref[0]

    @pl.when(jnp.logical_and(fresh, used))
    def _():
        wgu_bf[...] = wgu_ref[0, 0].astype(BF16)
        wd_bf[...] = wd_ref[0, 0].astype(BF16)

    @pl.when(used)
    def _():
        xb = xs_ref[...].astype(BF16)
        acc = jnp.zeros(ys_ref.shape, F32) + bd_ref[0, 0]
        half = D_EXPERT // 2
        for c in range(2):
            cg = slice(c * half, (c + 1) * half)
            cu = slice(D_EXPERT + c * half, D_EXPERT + (c + 1) * half)
            gate = jnp.dot(xb, wgu_bf[:, cg], preferred_element_type=F32) + bgu_ref[0, 0, :, cg]
            up = jnp.dot(xb, wgu_bf[:, cu], preferred_element_type=F32) + bgu_ref[0, 0, :, cu]
            gate = jnp.minimum(gate, SWIGLU_LIMIT)
            up = jnp.clip(up, -SWIGLU_LIMIT, SWIGLU_LIMIT)
            act = (up + 1.0) * gate * jax.nn.sigmoid(SWIGLU_ALPHA * gate)
            acc = acc + jnp.dot(act.astype(BF16), wd_bf[cg, :], preferred_element_type=F32)
        ys_ref[...] = acc

    @pl.when(jnp.logical_not(used))
    def _():
        ys_ref[...] = jnp.zeros_like(ys_ref)


def _experts(layer, block_e, n_used, xs, w_gu, b_gu, w_down, b_down):
    p = xs.shape[0]
    bm = MOE_BLOCK
    nb = p // bm
    depth = w_gu.shape[0]
    last = lambda blk, nu: jnp.minimum(blk, nu[0] - 1)
    grid_spec = pltpu.PrefetchScalarGridSpec(
        num_scalar_prefetch=2,
        grid=(nb,),
        in_specs=[pl.BlockSpec((bm, D_MODEL), lambda blk, be, nu: (last(blk, nu), 0)),
                  pl.BlockSpec((1, 1, D_MODEL, 2 * D_EXPERT), lambda blk, be, nu: (layer, be[blk], 0, 0)),
                  pl.BlockSpec((1, 1, 1, 2 * D_EXPERT), lambda blk, be, nu: (layer, be[blk], 0, 0)),
                  pl.BlockSpec((1, 1, D_EXPERT, D_MODEL), lambda blk, be, nu: (layer, be[blk], 0, 0)),
                  pl.BlockSpec((1, 1, 1, D_MODEL), lambda blk, be, nu: (layer, be[blk], 0, 0))],
        out_specs=pl.BlockSpec((bm, D_MODEL), lambda blk, be, nu: (blk, 0)),
        scratch_shapes=[pltpu.VMEM((D_MODEL, 2 * D_EXPERT), BF16), pltpu.VMEM((D_EXPERT, D_MODEL), BF16)],
    )
    return pl.pallas_call(
        _expert_kernel,
        grid_spec=grid_spec,
        out_shape=jax.ShapeDtypeStruct((p, D_MODEL), F32),
        compiler_params=_params(1),
        name="moe_experts",
    )(block_e, n_used, xs, w_gu, b_gu.reshape(depth, N_EXPERTS, 1, 2 * D_EXPERT), w_down,
      b_down.reshape(depth, N_EXPERTS, 1, D_MODEL))


def _finish_kernel(x1_ref, y0_ref, y1_ref, y2_ref, y3_ref, gate_ref, p_ref, lng_ref, lnb_ref,
                   wpg_ref, wpp_ref, out_ref, wpg_bf):
    @pl.when(pl.program_id(0) == 0)
    def _():
        wpg_bf[...] = wpg_ref[...].astype(BF16)

    gates = gate_ref[...]
    moe = y0_ref[...] * gates[:, 0:1]
    for kk, y_ref in enumerate((y1_ref, y2_ref, y3_ref), start=1):
        moe = moe + y_ref[...] * gates[:, kk:kk + 1]
    x2 = _layer_norm(DEEPNORM_ALPHA * x1_ref[...] + moe, lng_ref[...], lnb_ref[...])
    pg = jax.nn.sigmoid(jnp.dot(x2.astype(BF16), wpg_bf[...], preferred_element_type=F32))
    pp = _bdot(p_ref[...], wpp_ref[...])
    out_ref[...] = x2 + pg * pp


def _finish(x1, ysg, gates, p, ln_g, ln_b, w_pg, w_pp, tm):
    m = x1.shape[0]
    nt = m // tm
    row = lambda w: pl.BlockSpec((tm, w), lambda i: (i, 0))
    ple = p.shape[1]
    y_specs = [pl.BlockSpec((tm, D_MODEL), lambda i, kk=kk: (kk * nt + i, 0)) for kk in range(TOP_K)]
    return pl.pallas_call(
        _finish_kernel,
        grid=(nt,),
        in_specs=[row(D_MODEL)] + y_specs + [row(TOP_K), row(ple), _full((1, D_MODEL)), _full((1, D_MODEL)),
                                             _full((D_MODEL, D_MODEL)), _full((ple, D_MODEL))],
        out_specs=row(D_MODEL),
        out_shape=jax.ShapeDtypeStruct((m, D_MODEL), F32),
        scratch_shapes=[pltpu.VMEM((D_MODEL, D_MODEL), BF16)],
        compiler_params=_params(1),
        name="finish",
    )(x1, ysg, ysg, ysg, ysg, gates, p, ln_g, ln_b, w_pg, w_pp)


def _qkv_kernel(x_ref, w_ref, q_ref, k_ref, v_ref, w_bf):
    @pl.when(pl.program_id(0) == 0)
    def _():
        w_bf[...] = w_ref[...].astype(BF16)

    xb = x_ref[...].astype(BF16)
    q_ref[...] = jnp.dot(xb, w_bf[:, 0:D_MODEL], preferred_element_type=F32) * (MOBA_DH ** -0.5)
    k_ref[...] = jnp.dot(xb, w_bf[:, D_MODEL:2 * D_MODEL], preferred_element_type=F32)
    v_ref[...] = jnp.dot(xb, w_bf[:, 2 * D_MODEL:3 * D_MODEL], preferred_element_type=F32)


def _qkv(x, w_qkv, tm):
    m = x.shape[0]
    row = pl.BlockSpec((tm, D_MODEL), lambda i: (i, 0))
    shp = jax.ShapeDtypeStruct((m, D_MODEL), F32)
    return pl.pallas_call(
        _qkv_kernel,
        grid=(m // tm,),
        in_specs=[row, _full((D_MODEL, 3 * D_MODEL))],
        out_specs=[row, row, row],
        out_shape=[shp, shp, shp],
        scratch_shapes=[pltpu.VMEM((D_MODEL, 3 * D_MODEL), BF16)],
        compiler_params=_params(1),
        name="qkv",
    )(x, w_qkv)


def _topk_mask(gate, n_valid, axis):
    nb = gate.shape[axis]
    pos_i = lax.broadcasted_iota(jnp.int32, gate.shape, axis)
    valid = pos_i < n_valid
    pos = pos_i.astype(F32)
    cur = jnp.where(valid, gate, -jnp.inf)
    sel = jnp.zeros(gate.shape, F32)
    for _ in range(min(MOBA_TOPK, nb)):
        best = jnp.max(cur, axis=axis, keepdims=True)
        first = jnp.min(jnp.where(cur == best, pos, float(nb)), axis=axis, keepdims=True)
        pick = jnp.logical_and(pos == first, valid)
        sel = jnp.where(pick, 1.0, sel)
        cur = jnp.where(pick, -jnp.inf, cur)
    return sel


MOBA_AUG_MASK0 = 8
MOBA_NEG = -30720.0


def _split3(x):
    hi = x.astype(BF16).astype(F32)
    mid = (x - hi).astype(BF16).astype(F32)
    lo = (x - hi - mid).astype(BF16).astype(F32)
    return hi, mid, lo


def _moba_prompt_kernel(q_ref, k_ref, v_ref, slope_ref, o_ref, qt_scr, vt_scr, vtp_scr, ka_scr, km_scr, s_scr):
    hp = pl.program_id(1)
    qi = pl.program_id(2)
    nb = qt_scr.shape[0]
    blk = MOBA_BLOCK
    dh = MOBA_DH
    pair = 2 * dh

    @pl.when(qi == 0)
    def _():
        lane = lax.broadcasted_iota(jnp.int32, (blk, pair), 1)
        key_off = lax.broadcasted_iota(jnp.int32, (blk, pair), 0).astype(F32)
        for n in range(nb):
            rows = slice(n * blk, (n + 1) * blk)
            qt_scr[n] = q_ref[rows, :].T.astype(BF16)
            v_t = v_ref[rows, :].T.astype(BF16)
            vt_scr[n] = v_t
            vtp_scr[n // 2, :, (n % 2) * blk:(n % 2 + 1) * blk] = v_t
            kblk = k_ref[rows, :]
            km_scr[n:n + 1, :] = jnp.mean(kblk, axis=0, keepdims=True)
            for hh in range(2):
                a = lane - (1 - hh) * dh
                aug = jnp.where(a < 3, key_off,
                                jnp.where(a < 6, float(n * blk),
                                          jnp.where(a == MOBA_AUG_MASK0 + n, 1.0, 0.0)))
                in_head = jnp.logical_and(lane >= hh * dh, lane < (hh + 1) * dh)
                ka_scr[hh, rows, :] = jnp.where(in_head, kblk, aug).astype(BF16)

    key_i = lax.broadcasted_iota(jnp.int32, (blk, blk), 0)
    qry_i = lax.broadcasted_iota(jnp.int32, (blk, blk), 1)
    causal = key_i <= qry_i
    own = pl.ds(pl.multiple_of(qi * blk, blk), blk)
    q_t = qt_scr[qi].astype(F32)
    v_t_d = vt_scr[qi]
    lane_k = lax.broadcasted_iota(jnp.int32, (nb, pair), 1)
    r8 = lax.broadcasted_iota(jnp.int32, (8, blk), 0)
    pad_rows = jnp.zeros((dh - MOBA_AUG_MASK0 - nb, blk), F32)

    q_aug, init = [], []
    for hh in range(2):
        hrows = slice(hh * dh, (hh + 1) * dh)
        s_hi, s_mid, s_lo = _split3(slope_ref[pl.ds(2 * hp + hh, 1), :])
        is_hi = jnp.logical_or(r8 == 0, r8 == 3)
        is_mid = jnp.logical_or(r8 == 1, r8 == 4)
        slope_rows = jnp.where(is_hi, s_hi, jnp.where(is_mid, s_mid, s_lo))
        slope_rows = jnp.where(r8 < 6, slope_rows, 0.0)

        def with_aug(aug_rows, hh=hh, hrows=hrows):
            parts = [q_t[hrows], aug_rows] if hh == 0 else [aug_rows, q_t[hrows]]
            return jnp.concatenate(parts, axis=0).astype(BF16)

        q_diag = with_aug(jnp.concatenate([slope_rows, jnp.zeros((dh - 8, blk), F32)], axis=0))
        in_head = jnp.logical_and(lane_k >= hh * dh, lane_k < (hh + 1) * dh)
        km_h = jnp.where(in_head, km_scr[...], 0.0).astype(BF16)
        gate_t = jnp.dot(km_h, q_diag, preferred_element_type=F32)
        sel = _topk_mask(gate_t, qi, 0)
        mask_rows = jnp.where(sel > 0.0, 0.0, MOBA_NEG)
        q_aug.append(with_aug(jnp.concatenate([slope_rows, mask_rows, pad_rows], axis=0)))

        s = jnp.dot(ka_scr[hh, own, :], q_diag, preferred_element_type=F32)
        s = jnp.where(causal, s, -jnp.inf)
        m0 = jnp.max(s, axis=0, keepdims=True)
        p = jnp.exp(s - m0)
        l0 = jnp.sum(p, axis=0, keepdims=True)
        acc0 = jnp.dot(v_t_d[hrows, :], p.astype(BF16), preferred_element_type=F32)
        init += [m0, l0, acc0]

    def scores(pair_idx, slot):
        keys = pl.ds(pl.multiple_of(pair_idx * 2 * blk, 2 * blk), 2 * blk)
        col_max = []
        for hh in range(2):
            s = jnp.dot(ka_scr[hh, keys, :], q_aug[hh], preferred_element_type=F32)
            s_scr[slot, hh] = s
            col_max.append(jnp.max(s, axis=0, keepdims=True))
        return col_max

    n_pairs = (qi + 1) // 2

    def body(jj, carry):
        slot = jj % 2
        v_pair = vtp_scr[jj]
        out = []
        for hh in range(2):
            m, l, acc, mx = carry[4 * hh:4 * hh + 4]
            m_new = jnp.maximum(m, mx)
            alpha = jnp.exp(m - m_new)
            p = jnp.exp(s_scr[slot, hh] - m_new)
            l = alpha * l + jnp.sum(p, axis=0, keepdims=True)
            acc = alpha * acc + jnp.dot(v_pair[hh * dh:(hh + 1) * dh, :], p.astype(BF16),
                                        preferred_element_type=F32)
            out.append([m_new, l, acc])
        nxt = scores(jnp.minimum(jj + 1, nb // 2 - 1), 1 - slot)
        return tuple(out[0] + [nxt[0]] + out[1] + [nxt[1]])

    mx0 = scores(0, 0)
    res = lax.fori_loop(0, n_pairs, body, tuple(init[0:3] + [mx0[0]] + init[3:6] + [mx0[1]]))
    o_ref[...] = jnp.concatenate([res[2] / res[1], res[6] / res[5]], axis=0).T


def _moba_prompt(q, k, v, slopes, batch, seq):
    m = q.shape[0]
    nb = seq // MOBA_BLOCK
    assert nb % 2 == 0 and MOBA_AUG_MASK0 + nb <= MOBA_DH
    pair = 2 * MOBA_DH
    n_pairs = MOBA_HEADS // 2
    slope_rows = jnp.broadcast_to(slopes[:, None], (MOBA_HEADS, MOBA_BLOCK))
    seq_spec = pl.BlockSpec((seq, pair), lambda b, hp, qi: (b, hp))
    return pl.pallas_call(
        _moba_prompt_kernel,
        grid=(batch, n_pairs, nb),
        in_specs=[seq_spec, seq_spec, seq_spec, _full((MOBA_HEADS, MOBA_BLOCK))],
        out_specs=pl.BlockSpec((MOBA_BLOCK, pair), lambda b, hp, qi: (b * nb + qi, hp)),
        out_shape=jax.ShapeDtypeStruct((m, D_MODEL), F32),
        scratch_shapes=[pltpu.VMEM((nb, pair, MOBA_BLOCK), BF16), pltpu.VMEM((nb, pair, MOBA_BLOCK), BF16),
                        pltpu.VMEM((nb // 2, pair, 2 * MOBA_BLOCK), BF16),
                        pltpu.VMEM((2, seq, pair), BF16), pltpu.VMEM((nb, pair), F32),
                        pltpu.VMEM((2, 2, 2 * MOBA_BLOCK, MOBA_BLOCK), F32)],
        compiler_params=_params(3),
        name="moba_prompt",
    )(q, k, v, slope_rows)


def _moba_sample_kernel(n_pages, pos0, pt_ref, q_ref, kn_ref, vn_ref, slope_ref, *refs):
    del pt_ref
    k_pages = refs[:n_pages]
    v_pages = refs[n_pages:2 * n_pages]
    o_ref = refs[2 * n_pages]
    per_blk = MOBA_BLOCK // PAGE_SIZE
    n_blk = n_pages // per_blk
    h_i = lax.broadcasted_iota(jnp.int32, (MOBA_HEADS, D_MODEL), 0)
    d_i = lax.broadcasted_iota(jnp.int32, (MOBA_HEADS, D_MODEL), 1)
    own_head = d_i // MOBA_DH == h_i
    q_bd = jnp.where(own_head, q_ref[0], 0.0)
    q_bd16 = q_bd.astype(BF16)
    slope = slope_ref[...]

    lane_n = lax.broadcasted_iota(jnp.int32, (D_MODEL, n_blk), 1)
    means = jnp.zeros((D_MODEL, n_blk), F32)
    for n in range(n_blk):
        tot = sum(jnp.sum(k_pages[n * per_blk + i][...], axis=1, keepdims=True) for i in range(per_blk))
        means = jnp.where(lane_n == n, tot / float(MOBA_BLOCK), means)
    gate = jnp.dot(q_bd16, means.astype(BF16), preferred_element_type=F32)
    sel = _topk_mask(gate, n_blk, 1)

    lane = lax.broadcasted_iota(jnp.int32, (1, PAGE_SIZE), 1)
    scores = []
    for pg in range(n_pages):
        s = jnp.dot(q_bd16, k_pages[pg][...].astype(BF16), preferred_element_type=F32)
        dist = (pos0 - pg * PAGE_SIZE - lane).astype(F32)
        s = s - slope * dist
        n = pg // per_blk
        scores.append(jnp.where(sel[:, n:n + 1] > 0.0, s, -jnp.inf))
    k_new = kn_ref[0].astype(BF16).astype(F32)
    s_new = jnp.sum(q_bd16.astype(F32) * k_new, axis=1, keepdims=True)
    m = s_new
    for s in scores:
        m = jnp.maximum(m, jnp.max(s, axis=1, keepdims=True))
    p_new = jnp.exp(s_new - m)
    l = p_new
    acc = p_new.astype(BF16).astype(F32) * vn_ref[0].astype(BF16).astype(F32)
    for pg in range(n_pages):
        p = jnp.exp(scores[pg] - m)
        l = l + jnp.sum(p, axis=1, keepdims=True)
        acc = acc + lax.dot_general(p.astype(BF16), v_pages[pg][...].astype(BF16),
                                    (((1,), (1,)), ((), ())), preferred_element_type=F32)
    o_ref[0] = jnp.sum(jnp.where(own_head, acc / l, 0.0), axis=0, keepdims=True)


def _moba_sample(q, k_new, v_new, cache_k, cache_v, page_table, slopes, pos0):
    m = q.shape[0]
    n_pages = page_table.shape[1]
    assert pos0 == n_pages * PAGE_SIZE and pos0 % MOBA_BLOCK == 0
    n_phys = cache_k.shape[0]
    ck = jnp.transpose(cache_k, (0, 2, 3, 1)).reshape(n_phys, D_MODEL, PAGE_SIZE)
    cv = jnp.transpose(cache_v, (0, 2, 3, 1)).reshape(n_phys, D_MODEL, PAGE_SIZE)
    vec = pl.BlockSpec((1, 1, D_MODEL), lambda b, pt: (b, 0, 0))
    page_specs = [pl.BlockSpec((None, D_MODEL, PAGE_SIZE), lambda b, pt, pg=pg: (pt[b, pg], 0, 0))
                  for pg in range(n_pages)]
    grid_spec = pltpu.PrefetchScalarGridSpec(
        num_scalar_prefetch=1,
        grid=(m,),
        in_specs=[vec, vec, vec, pl.BlockSpec((MOBA_HEADS, 1), lambda b, pt: (0, 0))] + page_specs + page_specs,
        out_specs=vec,
    )
    out = pl.pallas_call(
        functools.partial(_moba_sample_kernel, n_pages, pos0),
        grid_spec=grid_spec,
        out_shape=jax.ShapeDtypeStruct((m, 1, D_MODEL), F32),
        compiler_params=_params(1),
        name="moba_sample",
    )(page_table, q.reshape(m, 1, D_MODEL), k_new.reshape(m, 1, D_MODEL), v_new.reshape(m, 1, D_MODEL),
      slopes.reshape(MOBA_HEADS, 1), *([ck] * n_pages), *([cv] * n_pages))
    return out.reshape(m, D_MODEL)


def _layer_tail(layer, xp, xs, mix_p, mix_s, p_p, p_s, w_out, ln_g, ln_b, w_router, b_router,
                w_gu, b_gu, w_down, b_down, w_pg, w_pp):
    n_p, n_s = xp.shape[0], xs.shape[0]
    lg0, lb0 = ln_g[0:1], ln_b[0:1]
    lg1, lb1 = ln_g[1:2], ln_b[1:2]
    br = b_router.reshape(1, N_EXPERTS)
    x1p, idxp, gatep = _mix(xp, mix_p, w_out, lg0, lb0, w_router, br, ROW_TILE)
    x1s, idxs, gates = _mix(xs, mix_s, w_out, lg0, lb0, w_router, br, n_s)

    idx = jnp.concatenate([idxp, idxs], axis=0)
    rank, counts = _rank(idx)
    counts = counts[0]
    bm = MOE_BLOCK
    n_tok = n_p + n_s
    n_blocks = (n_tok * TOP_K + N_EXPERTS * (bm - 1)) // bm
    padded = (counts + bm - 1) // bm * bm
    pend = jnp.cumsum(padded)
    pstart = pend - padded
    dest = pstart[idx] + rank
    n_used = (pend[-1] // bm).astype(jnp.int32)
    blk_ids = jnp.minimum(jnp.arange(n_blocks, dtype=jnp.int32), n_used - 1)
    block_e = jnp.minimum(jnp.searchsorted(pend, blk_ids * bm, side='right'), N_EXPERTS - 1).astype(jnp.int32)
    tok = jnp.broadcast_to(jnp.arange(n_tok, dtype=jnp.int32)[:, None], (n_tok, TOP_K))
    row_tok = jnp.zeros((n_blocks * bm,), jnp.int32).at[dest.reshape(-1)].set(tok.reshape(-1))
    x1 = jnp.concatenate([x1p, x1s], axis=0)
    gather = lambda src, rows: src.at[rows].get(mode="promise_in_bounds")
    xs_rows = gather(x1, row_tok)
    ys = _experts(layer, block_e, n_used.reshape(1), xs_rows, w_gu, b_gu, w_down, b_down)
    ysg_p = gather(ys, dest[:n_p].T.reshape(-1))
    ysg_s = gather(ys, dest[n_p:].T.reshape(-1))
    yp = _finish(x1p, ysg_p, gatep, p_p, lg1, lb1, w_pg, w_pp, ROW_TILE)
    ys_out = _finish(x1s, ysg_s, gates, p_s, lg1, lb1, w_pg, w_pp, n_s)
    return yp, ys_out


def kernel(x_prompt, x_sample, state_gla, state_pool, cache_k, cache_v, page_table, p_prompt, p_sample, w_in_ab, gla_w_alpha, gla_b_alpha, gla_norm_g, pool_w, pool_scale, w_out_ab, w_qkv_c, w_out_c, ln_g, ln_b, moe_w_router, moe_b_router, moe_w_gu, moe_b_gu, moe_w_down, moe_b_down, ple_w_gate, ple_w_proj):
    batch, seq, _ = x_prompt.shape
    n_s = x_sample.shape[0]
    n_p = batch * seq
    pos0 = page_table.shape[1] * PAGE_SIZE
    xp = x_prompt.reshape(n_p, D_MODEL)
    xs = x_sample.reshape(n_s, D_MODEL)

    def tail(i, xp, xs, mix_p, mix_s, w_out):
        return _layer_tail(i, xp, xs, mix_p, mix_s, p_prompt[i].reshape(n_p, -1), p_sample[i].reshape(n_s, -1),
                           w_out, ln_g[i], ln_b[i], moe_w_router[i], moe_b_router[i], moe_w_gu,
                           moe_b_gu, moe_w_down, moe_b_down, ple_w_gate[i], ple_w_proj[i])

    w_in = w_in_ab[0]
    c_a = 2 * QK_A + 2 * V_A
    w_main = jnp.concatenate([w_in[:, :c_a], w_in[:, c_a + GLA_LOWRANK:]], axis=1)
    w_a = w_in[:, c_a:c_a + GLA_LOWRANK]
    b_alpha = gla_b_alpha[0].reshape(1, QK_A)
    norm_g = gla_norm_g[0].reshape(1, GLA_DV)
    scale = pool_scale[0].reshape(1, POOL_CH)
    qkl_p, v_p, sg_p, u_p = _proj_ab(xp, w_main, w_a, gla_w_alpha[0], b_alpha, ROW_TILE)
    qkl_s, v_s, sg_s, u_s = _proj_ab(xs, w_main, w_a, gla_w_alpha[0], b_alpha, n_s)
    mix_p, gla_p = _gla_prompt(qkl_p, v_p, sg_p, norm_g, batch, seq, ROW_TILE)
    mix_p = _pool_prompt(u_p, mix_p, pool_w[0], scale, batch, seq, ROW_TILE)
    mix_s, gla_s = _gla_sample(qkl_s, v_s, sg_s, norm_g, state_gla[0].reshape(n_s, QK_A, GLA_DV))
    mix_s, pool_s = _pool_sample(u_s, state_pool[0], mix_s, pool_w[0], scale, pos0)
    pool_p = u_p.reshape(batch, seq, POOL_CH)[:, seq - POOL_HIST:]
    xp, xs = tail(0, xp, xs, mix_p, mix_s, w_out_ab[0])

    slopes = jnp.exp2(-8.0 * jnp.arange(1, MOBA_HEADS + 1, dtype=F32) / MOBA_HEADS)
    q_p, k_p, v_p2 = _qkv(xp, w_qkv_c[0], ROW_TILE)
    q_s, k_s, v_s2 = _qkv(xs, w_qkv_c[0], n_s)
    o_p = _moba_prompt(q_p, k_p, v_p2, slopes, batch, seq)
    o_s = _moba_sample(q_s, k_s, v_s2, cache_k[0], cache_v[0], page_table, slopes, pos0)
    xp, xs = tail(1, xp, xs, o_p, o_s, w_out_c[0])

    hd = (MOBA_HEADS, MOBA_DH)
    return (xp.reshape(batch, seq, D_MODEL), xs.reshape(n_s, 1, D_MODEL),
            gla_p.reshape(1, batch, GLA_HEADS, GLA_DK, GLA_DV), gla_s.reshape(1, n_s, GLA_HEADS, GLA_DK, GLA_DV),
            pool_p[None], pool_s[None],
            k_p.reshape(1, batch, seq, *hd), v_p2.reshape(1, batch, seq, *hd),
            k_s.reshape(1, n_s, 1, *hd), v_s2.reshape(1, n_s, 1, *hd))
```

```python
import functools
import math

import numpy as np
import jax
import jax.numpy as jnp
from jax import lax
from jax.experimental import pallas as pl
from jax.experimental.pallas import tpu as pltpu

F32 = jnp.float32
BF16 = jnp.bfloat16
HI = lax.Precision.HIGHEST

D_MODEL = 1024
GLA_HEADS = 4
GLA_DK = 64
GLA_DV = 128
GLA_LOWRANK = 16
GLA_TAU = 16.0
QK_A = GLA_HEADS * GLA_DK
V_A = GLA_HEADS * GLA_DV
POOL_WINDOWS = (2, 4, 8, 16)
POOL_CH = 512
POOL_GROUP = 128
POOL_HIST = 15
MOBA_HEADS = 16
MOBA_DH = 64
MOBA_BLOCK = 256
MOBA_TOPK = 3
PAGE_SIZE = 128
N_EXPERTS = 32
TOP_K = 4
D_EXPERT = 1024
SWIGLU_LIMIT = 7.0
SWIGLU_ALPHA = 1.702
DEPTH = 2
DEEPNORM_ALPHA = (2 * DEPTH) ** 0.25
LN_EPS = 1e-5

ROW_TILE = 512
GLA_CHUNK = 128
GLA_LEVELS = 7
MOE_BLOCK = 512
RANK_TILE = 384
VMEM_LIMIT = 56 * 1024 * 1024


def _params(n_axes, vmem=VMEM_LIMIT):
    return pltpu.CompilerParams(dimension_semantics=("arbitrary",) * n_axes, vmem_limit_bytes=vmem)


def _bdot(a, b):
    return jnp.dot(a.astype(BF16), b.astype(BF16), preferred_element_type=F32)


def _layer_norm(y, g, b):
    mu = jnp.mean(y, axis=-1, keepdims=True)
    yc = y - mu
    var = jnp.mean(yc * yc, axis=-1, keepdims=True)
    return yc * lax.rsqrt(var + LN_EPS) * g + b


def _full(shape):
    n = len(shape)
    return pl.BlockSpec(shape, lambda *_: (0,) * n)


def _proj_ab_kernel(x_ref, wm_ref, wa_ref, walpha_ref, balpha_ref,
                    qkl_ref, v_ref, sg_ref, u_ref, wm_bf, wa_bf):
    @pl.when(pl.program_id(0) == 0)
    def _():
        wm_bf[...] = wm_ref[...].astype(BF16)
        wa_bf[...] = wa_ref[...].astype(BF16)

    xb = x_ref[...].astype(BF16)
    qk = jnp.dot(xb, wm_bf[:, 0:2 * QK_A], preferred_element_type=F32)
    qkl_ref[:, 0:QK_A] = qk[:, 0:QK_A] * (GLA_DK ** -0.5)
    qkl_ref[:, QK_A:2 * QK_A] = qk[:, QK_A:2 * QK_A]
    a_lr = jnp.dot(xb, wa_bf[...], preferred_element_type=F32)
    z = _bdot(a_lr, walpha_ref[...]) + balpha_ref[...]
    log_sig = jnp.minimum(z, 0.0) - jnp.log1p(jnp.exp(-jnp.abs(z)))
    qkl_ref[:, 2 * QK_A:3 * QK_A] = log_sig / GLA_TAU
    c0 = 2 * QK_A
    v_ref[...] = jnp.dot(xb, wm_bf[:, c0:c0 + V_A], preferred_element_type=F32)
    g = jnp.dot(xb, wm_bf[:, c0 + V_A:c0 + 2 * V_A], preferred_element_type=F32)
    sg_ref[...] = g * jax.nn.sigmoid(g)
    u_ref[...] = jnp.dot(xb, wm_bf[:, c0 + 2 * V_A:c0 + 2 * V_A + POOL_CH], preferred_element_type=F32)


def _proj_ab(x, w_main, w_a, w_alpha, b_alpha, tm):
    m = x.shape[0]
    nmain = w_main.shape[1]
    row = lambda w: pl.BlockSpec((tm, w), lambda i: (i, 0))
    return pl.pallas_call(
        _proj_ab_kernel,
        grid=(m // tm,),
        in_specs=[row(D_MODEL), _full((D_MODEL, nmain)), _full((D_MODEL, GLA_LOWRANK)),
                  _full((GLA_LOWRANK, QK_A)), _full((1, QK_A))],
        out_specs=[row(3 * QK_A), row(V_A), row(V_A), row(POOL_CH)],
        out_shape=[jax.ShapeDtypeStruct((m, 3 * QK_A), F32), jax.ShapeDtypeStruct((m, V_A), F32),
                   jax.ShapeDtypeStruct((m, V_A), F32), jax.ShapeDtypeStruct((m, POOL_CH), F32)],
        scratch_shapes=[pltpu.VMEM((D_MODEL, nmain), BF16), pltpu.VMEM((D_MODEL, GLA_LOWRANK), BF16)],
        compiler_params=_params(1),
        name="proj_ab",
    )(x, w_main, w_a, w_alpha, b_alpha)


def _gla_tables():
    c = GLA_CHUNK
    i = np.arange(c)[:, None]
    s = np.arange(c)[None, :]
    mats = [(s <= i), (s > i)]
    for lev in range(GLA_LEVELS):
        p = GLA_LEVELS - 1 - lev
        half = 1 << p
        start = (i >> (p + 1)) << (p + 1)
        mid = start + half - 1
        upper = i >= start + half
        mats.append(np.where(upper, (s > mid) & (s <= i), (s > i) & (s <= mid)))
    seg = np.concatenate(mats, axis=0).astype(np.float32)
    j = np.arange(c)[None, :]
    x = i ^ j
    lvl = np.full((c, c), GLA_LEVELS + 1, np.int32)
    lvl[np.arange(c), np.arange(c)] = GLA_LEVELS
    for lev in range(GLA_LEVELS):
        p = GLA_LEVELS - 1 - lev
        lvl = np.where(((x >> p) == 1) & (((i >> p) & 1) == 1), lev, lvl)
    lvl4 = np.tile(lvl, (GLA_HEADS, 1)).astype(np.int32)
    lane_head = (np.arange(QK_A) // GLA_DK)[None, :]
    row_head = (np.arange(GLA_HEADS * c) // c)[:, None]
    hm4 = (lane_head == row_head).astype(np.float32)
    return seg, lvl4, hm4


def _gla_chunk(q, k, la, v, s_all, seg, lvl4, hm4):
    c = GLA_CHUNK
    e = sum(jnp.dot(seg, part.astype(BF16), preferred_element_type=F32) for part in _split3(la))
    w = jnp.exp(e)
    w_b = w[0:c]
    w_k = w[c:2 * c]

    def stack_heads(t):
        return (jnp.concatenate([t] * GLA_HEADS, axis=0) * hm4).astype(BF16)

    o_inter = jnp.dot(stack_heads(q * w_b), s_all.astype(BF16), preferred_element_type=F32)
    a = jnp.zeros((GLA_HEADS * c, c), F32)
    for lev in range(GLA_LEVELS + 1):
        if lev < GLA_LEVELS:
            w_l = w[(2 + lev) * c:(3 + lev) * c]
            ql, kl = q * w_l, k * w_l
        else:
            ql, kl = q, k
        p_l = lax.dot_general(stack_heads(ql), kl.astype(BF16), (((1,), (1,)), ((), ())),
                              preferred_element_type=F32)
        a = jnp.where(lvl4 == lev, p_l, a)
    a = a.astype(BF16)
    outs = []
    for h in range(GLA_HEADS):
        v_h = v[:, h * GLA_DV:(h + 1) * GLA_DV].astype(BF16)
        o_h = o_inter[h * c:(h + 1) * c] + jnp.dot(a[h * c:(h + 1) * c], v_h, preferred_element_type=F32)
        outs.append(o_h)
    ks_t = (k * w_k).T.astype(BF16)
    kv = jnp.dot(ks_t, v.astype(BF16), preferred_element_type=F32)
    dec = jnp.exp(jnp.sum(la.T, axis=1, keepdims=True))
    new_rows = []
    for h in range(GLA_HEADS):
        rows = slice(h * GLA_DK, (h + 1) * GLA_DK)
        new_rows.append(dec[rows] * s_all[rows] + kv[rows, h * GLA_DV:(h + 1) * GLA_DV])
    return outs, jnp.concatenate(new_rows, axis=0)


def _gla_finish(o_h, norm_g, sg_h):
    o_h = o_h * lax.rsqrt(jnp.mean(o_h * o_h, axis=-1, keepdims=True) + LN_EPS) * norm_g
    return o_h * sg_h


def _gla_prompt_kernel(qkl_ref, v_ref, sg_ref, ng_ref, seg_ref, lvl_ref, hm_ref,
                       mix_ref, state_ref, s_scr):
    t = pl.program_id(1)

    @pl.when(t == 0)
    def _():
        s_scr[...] = jnp.zeros_like(s_scr)

    seg = seg_ref[...]
    lvl4 = lvl_ref[...]
    hm4 = hm_ref[...]
    norm_g = ng_ref[...]
    n_chunks = qkl_ref.shape[0] // GLA_CHUNK

    def body(ci, carry):
        r0 = pl.multiple_of(ci * GLA_CHUNK, GLA_CHUNK)
        rows = pl.ds(r0, GLA_CHUNK)
        q = qkl_ref[rows, 0:QK_A]
        k = qkl_ref[rows, QK_A:2 * QK_A]
        la = qkl_ref[rows, 2 * QK_A:3 * QK_A]
        v = v_ref[rows, :]
        outs, s_new = _gla_chunk(q, k, la, v, s_scr[...], seg, lvl4, hm4)
        s_scr[...] = s_new
        for h in range(GLA_HEADS):
            cols = slice(h * GLA_DV, (h + 1) * GLA_DV)
            mix_ref[rows, cols] = _gla_finish(outs[h], norm_g, sg_ref[rows, cols])
        return carry

    lax.fori_loop(0, n_chunks, body, 0)

    @pl.when(t == pl.num_programs(1) - 1)
    def _():
        state_ref[0] = s_scr[...]


def _gla_prompt(qkl, v, sg, norm_g, batch, seq, tm):
    m = qkl.shape[0]
    nt = seq // tm
    seg, lvl4, hm4 = _gla_tables()
    row = lambda w: pl.BlockSpec((tm, w), lambda b, t: (b * nt + t, 0))
    return pl.pallas_call(
        _gla_prompt_kernel,
        grid=(batch, nt),
        in_specs=[row(3 * QK_A), row(V_A), row(V_A), _full((1, GLA_DV)),
                  _full(seg.shape), _full(lvl4.shape), _full(hm4.shape)],
        out_specs=[pl.BlockSpec((tm, V_A), lambda b, t: (b * nt + t, 0)),
                   pl.BlockSpec((1, QK_A, GLA_DV), lambda b, t: (b, 0, 0))],
        out_shape=[jax.ShapeDtypeStruct((m, D_MODEL), F32),
                   jax.ShapeDtypeStruct((batch, QK_A, GLA_DV), F32)],
        scratch_shapes=[pltpu.VMEM((QK_A, GLA_DV), F32)],
        compiler_params=_params(2),
        name="gla_prompt",
    )(qkl, v, sg, norm_g, jnp.asarray(seg, BF16), jnp.asarray(lvl4), jnp.asarray(hm4))


GLA_SAMPLE_ROWS = 8


def _gla_sample_kernel(qkl_ref, v_ref, sg_ref, ng_ref, s0_ref, mix_ref, s1_ref):
    ones = jnp.ones((QK_A, GLA_DV), F32)
    r_i = lax.broadcasted_iota(jnp.int32, (QK_A, QK_A), 0)
    c_i = lax.broadcasted_iota(jnp.int32, (QK_A, QK_A), 1)
    eye = r_i == c_i
    norm_g = ng_ref[...]

    def col_bcast(row):
        diag = jnp.where(eye, jnp.broadcast_to(row, (QK_A, QK_A)), 0.0)
        return jnp.dot(diag, ones, precision=HI, preferred_element_type=F32)

    for r in range(GLA_SAMPLE_ROWS):
        q = col_bcast(qkl_ref[r:r + 1, 0:QK_A])
        k = col_bcast(qkl_ref[r:r + 1, QK_A:2 * QK_A])
        dec = jnp.exp(col_bcast(qkl_ref[r:r + 1, 2 * QK_A:3 * QK_A]))
        for h in range(GLA_HEADS):
            rows = slice(h * GLA_DK, (h + 1) * GLA_DK)
            cols = slice(h * GLA_DV, (h + 1) * GLA_DV)
            s_new = dec[rows] * s0_ref[r, rows, :] + k[rows] * v_ref[r:r + 1, cols]
            s1_ref[r, rows, :] = s_new
            o_h = jnp.sum(q[rows] * s_new, axis=0, keepdims=True)
            mix_ref[r:r + 1, cols] = _gla_finish(o_h, norm_g, sg_ref[r:r + 1, cols])


def _gla_sample(qkl, v, sg, norm_g, s0):
    m = qkl.shape[0]
    rb = GLA_SAMPLE_ROWS
    row = lambda w: pl.BlockSpec((rb, w), lambda i: (i, 0))
    st = pl.BlockSpec((rb, QK_A, GLA_DV), lambda i: (i, 0, 0))
    return pl.pallas_call(
        _gla_sample_kernel,
        grid=(m // rb,),
        in_specs=[row(3 * QK_A), row(V_A), row(V_A), _full((1, GLA_DV)), st],
        out_specs=[pl.BlockSpec((rb, V_A), lambda i: (i, 0)), st],
        out_shape=[jax.ShapeDtypeStruct((m, D_MODEL), F32),
                   jax.ShapeDtypeStruct((m, QK_A, GLA_DV), F32)],
        compiler_params=_params(1),
        name="gla_sample",
    )(qkl, v, sg, norm_g, s0)


def _pool_project(d_groups, wp_ref, scale_ref, mix_ref):
    for g in range(len(POOL_WINDOWS)):
        cols = slice(g * POOL_GROUP, (g + 1) * POOL_GROUP)
        mix_ref[:, cols] = _bdot(d_groups[g], wp_ref[g]) * scale_ref[:, cols]


def _pool_prompt_kernel(u_ref, prev_ref, wp_ref, scale_ref, mixin_ref, mix_ref):
    del mixin_ref
    t = pl.program_id(1)
    tm = u_ref.shape[0]
    hist = prev_ref.shape[0]
    prev = jnp.where(t > 0, prev_ref[...], 0.0)
    z = jnp.concatenate([prev, u_ref[...]], axis=0)
    pos = t * tm + lax.broadcasted_iota(jnp.int32, (tm, 1), 0)
    d_groups = []
    for g, w in enumerate(POOL_WINDOWS):
        cols = slice(g * POOL_GROUP, (g + 1) * POOL_GROUP)
        s = z[:, cols]
        shift = 1
        while shift < w:
            s = s + pltpu.roll(s, shift, 0)
            shift *= 2
        cnt = jnp.minimum(w, pos + 1).astype(F32)
        d_groups.append(s[hist:] / cnt - z[hist:, cols])
    _pool_project(d_groups, wp_ref, scale_ref, mix_ref)


def _pool_prompt(u, mix, w_pool, scale, batch, seq, tm):
    nt = seq // tm
    hist = 16
    assert hist > POOL_HIST and tm % hist == 0
    per = tm // hist
    return pl.pallas_call(
        _pool_prompt_kernel,
        grid=(batch, nt),
        in_specs=[pl.BlockSpec((tm, POOL_CH), lambda b, t: (b * nt + t, 0)),
                  pl.BlockSpec((hist, POOL_CH), lambda b, t: (jnp.maximum((b * nt + t) * per - 1, 0), 0)),
                  _full(w_pool.shape), _full((1, POOL_CH)),
                  pl.BlockSpec(memory_space=pl.ANY)],
        out_specs=pl.BlockSpec((tm, POOL_CH), lambda b, t: (b * nt + t, 1)),
        out_shape=jax.ShapeDtypeStruct(mix.shape, F32),
        input_output_aliases={4: 0},
        compiler_params=_params(2),
        name="pool_prompt",
    )(u, u, w_pool, scale, mix)


def _pool_sample_kernel(u_ref, st_ref, wp_ref, scale_ref, mixin_ref, mix_ref, st_out_ref):
    del mixin_ref
    u = u_ref[...]
    d_groups = []
    for g, w in enumerate(POOL_WINDOWS):
        cols = slice(g * POOL_GROUP, (g + 1) * POOL_GROUP)
        win = u[:, cols] + jnp.sum(st_ref[:, POOL_HIST - (w - 1):POOL_HIST, cols], axis=1)
        d_groups.append(win / float(w) - u[:, cols])
    _pool_project(d_groups, wp_ref, scale_ref, mix_ref)
    st_out_ref[:, 0:POOL_HIST - 1, :] = st_ref[:, 1:POOL_HIST, :]
    st_out_ref[:, POOL_HIST - 1:POOL_HIST, :] = u[:, None, :]


def _pool_sample(u, st, mix, w_pool, scale, pos0):
    m = u.shape[0]
    assert pos0 + 1 >= max(POOL_WINDOWS)
    return pl.pallas_call(
        _pool_sample_kernel,
        grid=(1,),
        in_specs=[_full((m, POOL_CH)), _full(st.shape), _full(w_pool.shape), _full((1, POOL_CH)),
                  pl.BlockSpec(memory_space=pl.ANY)],
        out_specs=[pl.BlockSpec((m, POOL_CH), lambda i: (0, 1)), _full(st.shape)],
        out_shape=[jax.ShapeDtypeStruct(mix.shape, F32), jax.ShapeDtypeStruct(st.shape, F32)],
        input_output_aliases={4: 0},
        compiler_params=_params(1),
        name="pool_sample",
    )(u, st, w_pool, scale, mix)


def _mix_kernel(x_ref, mix_ref, w_ref, lng_ref, lnb_ref, wr_ref, br_ref,
                x1_ref, idx_ref, gate_ref, w_bf):
    @pl.when(pl.program_id(0) == 0)
    def _():
        w_bf[...] = w_ref[...].astype(BF16)

    h = jnp.dot(mix_ref[...].astype(BF16), w_bf[...], preferred_element_type=F32)
    x1 = _layer_norm(DEEPNORM_ALPHA * x_ref[...] + h, lng_ref[...], lnb_ref[...])
    x1_ref[...] = x1
    logits = _bdot(x1, wr_ref[...]) + br_ref[...]
    tm = logits.shape[0]
    lane = lax.broadcasted_iota(jnp.int32, (tm, N_EXPERTS), 1)
    lane_k = lax.broadcasted_iota(jnp.int32, (tm, TOP_K), 1)
    idx_out = jnp.zeros((tm, TOP_K), jnp.int32)
    val_out = jnp.zeros((tm, TOP_K), F32)
    cur = logits
    for kk in range(TOP_K):
        mval = jnp.max(cur, axis=1, keepdims=True)
        midx = jnp.min(jnp.where(cur == mval, lane.astype(F32), float(N_EXPERTS)), axis=1,
                       keepdims=True).astype(jnp.int32)
        idx_out = jnp.where(lane_k == kk, midx, idx_out)
        val_out = jnp.where(lane_k == kk, mval, val_out)
        cur = jnp.where(lane == midx, -jnp.inf, cur)
    ex = jnp.exp(val_out - val_out[:, 0:1])
    idx_ref[...] = idx_out
    gate_ref[...] = ex / jnp.sum(ex, axis=1, keepdims=True)


def _mix(x, mix, w_out, ln_g, ln_b, w_router, b_router, tm):
    m = x.shape[0]
    row = lambda w: pl.BlockSpec((tm, w), lambda i: (i, 0))
    return pl.pallas_call(
        _mix_kernel,
        grid=(m // tm,),
        in_specs=[row(D_MODEL), row(D_MODEL), _full((D_MODEL, D_MODEL)), _full((1, D_MODEL)),
                  _full((1, D_MODEL)), _full((D_MODEL, N_EXPERTS)), _full((1, N_EXPERTS))],
        out_specs=[row(D_MODEL), row(TOP_K), row(TOP_K)],
        out_shape=[jax.ShapeDtypeStruct((m, D_MODEL), F32),
                   jax.ShapeDtypeStruct((m, TOP_K), jnp.int32), jax.ShapeDtypeStruct((m, TOP_K), F32)],
        scratch_shapes=[pltpu.VMEM((D_MODEL, D_MODEL), BF16)],
        compiler_params=_params(1),
        name="mix",
    )(x, mix, w_out, ln_g, ln_b, w_router, b_router)


def _rank_kernel(idx_ref, rank_ref, cnt_ref, carry):
    i = pl.program_id(0)

    @pl.when(i == 0)
    def _():
        carry[...] = jnp.zeros_like(carry)

    idx = idx_ref[...]
    tm = idx.shape[0]
    lane = lax.broadcasted_iota(jnp.int32, (tm, N_EXPERTS), 1)
    onehots = [(idx[:, kk:kk + 1] == lane) for kk in range(TOP_K)]
    member = sum(jnp.where(o, 1.0, 0.0) for o in onehots)
    r_i = lax.broadcasted_iota(jnp.int32, (tm, tm), 0)
    c_i = lax.broadcasted_iota(jnp.int32, (tm, tm), 1)
    strict_lower = jnp.where(c_i < r_i, 1.0, 0.0).astype(BF16)
    before = jnp.dot(strict_lower, member.astype(BF16), preferred_element_type=F32) + carry[...]
    lane_k = lax.broadcasted_iota(jnp.int32, (tm, TOP_K), 1)
    rank = jnp.zeros((tm, TOP_K), F32)
    for kk in range(TOP_K):
        r_k = jnp.sum(jnp.where(onehots[kk], before, 0.0), axis=1, keepdims=True)
        rank = jnp.where(lane_k == kk, r_k, rank)
    rank_ref[...] = rank.astype(jnp.int32)
    carry[...] = carry[...] + jnp.sum(member, axis=0, keepdims=True)
    cnt_ref[...] = carry[...].astype(jnp.int32)


def _rank(idx):
    n = idx.shape[0]
    tm = RANK_TILE
    return pl.pallas_call(
        _rank_kernel,
        grid=(n // tm,),
        in_specs=[pl.BlockSpec((tm, TOP_K), lambda i: (i, 0))],
        out_specs=[pl.BlockSpec((tm, TOP_K), lambda i: (i, 0)), _full((1, N_EXPERTS))],
        out_shape=[jax.ShapeDtypeStruct((n, TOP_K), jnp.int32),
                   jax.ShapeDtypeStruct((1, N_EXPERTS), jnp.int32)],
        scratch_shapes=[pltpu.VMEM((1, N_EXPERTS), F32)],
        compiler_params=_params(1),
        name="moe_rank",
    )(idx)


def _expert_kernel(be_ref, nu_ref, xs_ref, wgu_ref, bgu_ref, wd_ref, bd_ref, ys_ref, wgu_bf, wd_bf):
    blk = pl.program_id(0)
    prev = be_ref[jnp.maximum(blk - 1, 0)]
    fresh = jnp.logical_or(blk == 0, be_ref[blk] != prev)
    used = blk < nu_ref[0]

    @pl.when(jnp.logical_and(fresh, used))
    def _():
        wgu_bf[...] = wgu_ref[0, 0].astype(BF16)
        wd_bf[...] = wd_ref[0, 0].astype(BF16)

    @pl.when(used)
    def _():
        xb = xs_ref[...].astype(BF16)
        acc = jnp.zeros(ys_ref.shape, F32) + bd_ref[0, 0]
        half = D_EXPERT // 2
        for c in range(2):
            cg = slice(c * half, (c + 1) * half)
            cu = slice(D_EXPERT + c * half, D_EXPERT + (c + 1) * half)
            gate = jnp.dot(xb, wgu_bf[:, cg], preferred_element_type=F32) + bgu_ref[0, 0, :, cg]
            up = jnp.dot(xb, wgu_bf[:, cu], preferred_element_type=F32) + bgu_ref[0, 0, :, cu]
            gate = jnp.minimum(gate, SWIGLU_LIMIT)
            up = jnp.clip(up, -SWIGLU_LIMIT, SWIGLU_LIMIT)
            act = (up + 1.0) * gate * jax.nn.sigmoid(SWIGLU_ALPHA * gate)
            acc = acc + jnp.dot(act.astype(BF16), wd_bf[cg, :], preferred_element_type=F32)
        ys_ref[...] = acc

    @pl.when(jnp.logical_not(used))
    def _():
        ys_ref[...] = jnp.zeros_like(ys_ref)


def _experts(layer, block_e, n_used, xs, w_gu, b_gu, w_down, b_down):
    p = xs.shape[0]
    bm = MOE_BLOCK
    nb = p // bm
    depth = w_gu.shape[0]
    last = lambda blk, nu: jnp.minimum(blk, nu[0] - 1)
    grid_spec = pltpu.PrefetchScalarGridSpec(
        num_scalar_prefetch=2,
        grid=(nb,),
        in_specs=[pl.BlockSpec((bm, D_MODEL), lambda blk, be, nu: (last(blk, nu), 0)),
                  pl.BlockSpec((1, 1, D_MODEL, 2 * D_EXPERT), lambda blk, be, nu: (layer, be[blk], 0, 0)),
                  pl.BlockSpec((1, 1, 1, 2 * D_EXPERT), lambda blk, be, nu: (layer, be[blk], 0, 0)),
                  pl.BlockSpec((1, 1, D_EXPERT, D_MODEL), lambda blk, be, nu: (layer, be[blk], 0, 0)),
                  pl.BlockSpec((1, 1, 1, D_MODEL), lambda blk, be, nu: (layer, be[blk], 0, 0))],
        out_specs=pl.BlockSpec((bm, D_MODEL), lambda blk, be, nu: (blk, 0)),
        scratch_shapes=[pltpu.VMEM((D_MODEL, 2 * D_EXPERT), BF16), pltpu.VMEM((D_EXPERT, D_MODEL), BF16)],
    )
    return pl.pallas_call(
        _expert_kernel,
        grid_spec=grid_spec,
        out_shape=jax.ShapeDtypeStruct((p, D_MODEL), F32),
        compiler_params=_params(1),
        name="moe_experts",
    )(block_e, n_used, xs, w_gu, b_gu.reshape(depth, N_EXPERTS, 1, 2 * D_EXPERT), w_down,
      b_down.reshape(depth, N_EXPERTS, 1, D_MODEL))


def _finish_kernel(x1_ref, y0_ref, y1_ref, y2_ref, y3_ref, gate_ref, p_ref, lng_ref, lnb_ref,
                   wpg_ref, wpp_ref, out_ref, wpg_bf):
    @pl.when(pl.program_id(0) == 0)
    def _():
        wpg_bf[...] = wpg_ref[...].astype(BF16)

    gates = gate_ref[...]
    moe = y0_ref[...] * gates[:, 0:1]
    for kk, y_ref in enumerate((y1_ref, y2_ref, y3_ref), start=1):
        moe = moe + y_ref[...] * gates[:, kk:kk + 1]
    x2 = _layer_norm(DEEPNORM_ALPHA * x1_ref[...] + moe, lng_ref[...], lnb_ref[...])
    pg = jax.nn.sigmoid(jnp.dot(x2.astype(BF16), wpg_bf[...], preferred_element_type=F32))
    pp = _bdot(p_ref[...], wpp_ref[...])
    out_ref[...] = x2 + pg * pp


def _finish(x1, ysg, gates, p, ln_g, ln_b, w_pg, w_pp, tm):
    m = x1.shape[0]
    nt = m // tm
    row = lambda w: pl.BlockSpec((tm, w), lambda i: (i, 0))
    ple = p.shape[1]
    y_specs = [pl.BlockSpec((tm, D_MODEL), lambda i, kk=kk: (kk * nt + i, 0)) for kk in range(TOP_K)]
    return pl.pallas_call(
        _finish_kernel,
        grid=(nt,),
        in_specs=[row(D_MODEL)] + y_specs + [row(TOP_K), row(ple), _full((1, D_MODEL)), _full((1, D_MODEL)),
                                             _full((D_MODEL, D_MODEL)), _full((ple, D_MODEL))],
        out_specs=row(D_MODEL),
        out_shape=jax.ShapeDtypeStruct((m, D_MODEL), F32),
        scratch_shapes=[pltpu.VMEM((D_MODEL, D_MODEL), BF16)],
        compiler_params=_params(1),
        name="finish",
    )(x1, ysg, ysg, ysg, ysg, gates, p, ln_g, ln_b, w_pg, w_pp)


def _qkv_kernel(x_ref, w_ref, q_ref, k_ref, v_ref, w_bf):
    @pl.when(pl.program_id(0) == 0)
    def _():
        w_bf[...] = w_ref[...].astype(BF16)

    xb = x_ref[...].astype(BF16)
    q_ref[...] = jnp.dot(xb, w_bf[:, 0:D_MODEL], preferred_element_type=F32) * (MOBA_DH ** -0.5)
    k_ref[...] = jnp.dot(xb, w_bf[:, D_MODEL:2 * D_MODEL], preferred_element_type=F32)
    v_ref[...] = jnp.dot(xb, w_bf[:, 2 * D_MODEL:3 * D_MODEL], preferred_element_type=F32)


def _qkv(x, w_qkv, tm):
    m = x.shape[0]
    row = pl.BlockSpec((tm, D_MODEL), lambda i: (i, 0))
    shp = jax.ShapeDtypeStruct((m, D_MODEL), F32)
    return pl.pallas_call(
        _qkv_kernel,
        grid=(m // tm,),
        in_specs=[row, _full((D_MODEL, 3 * D_MODEL))],
        out_specs=[row, row, row],
        out_shape=[shp, shp, shp],
        scratch_shapes=[pltpu.VMEM((D_MODEL, 3 * D_MODEL), BF16)],
        compiler_params=_params(1),
        name="qkv",
    )(x, w_qkv)


def _topk_mask(gate, n_valid, axis):
    nb = gate.shape[axis]
    pos_i = lax.broadcasted_iota(jnp.int32, gate.shape, axis)
    valid = pos_i < n_valid
    pos = pos_i.astype(F32)
    cur = jnp.where(valid, gate, -jnp.inf)
    sel = jnp.zeros(gate.shape, F32)
    for _ in range(min(MOBA_TOPK, nb)):
        best = jnp.max(cur, axis=axis, keepdims=True)
        first = jnp.min(jnp.where(cur == best, pos, float(nb)), axis=axis, keepdims=True)
        pick = jnp.logical_and(pos == first, valid)
        sel = jnp.where(pick, 1.0, sel)
        cur = jnp.where(pick, -jnp.inf, cur)
    return sel


MOBA_AUG_MASK0 = 8
MOBA_NEG = -30720.0


def _split3(x):
    hi = x.astype(BF16).astype(F32)
    mid = (x - hi).astype(BF16).astype(F32)
    lo = (x - hi - mid).astype(BF16).astype(F32)
    return hi, mid, lo


def _moba_prompt_kernel(q_ref, k_ref, v_ref, slope_ref, o_ref, qt_scr, vt_scr, vtp_scr, ka_scr, km_scr, s_scr):
    hp = pl.program_id(1)
    qi = pl.program_id(2)
    nb = qt_scr.shape[0]
    blk = MOBA_BLOCK
    dh = MOBA_DH
    pair = 2 * dh

    @pl.when(qi == 0)
    def _():
        lane = lax.broadcasted_iota(jnp.int32, (blk, pair), 1)
        key_off = lax.broadcasted_iota(jnp.int32, (blk, pair), 0).astype(F32)
        for n in range(nb):
            rows = slice(n * blk, (n + 1) * blk)
            qt_scr[n] = q_ref[rows, :].T.astype(BF16)
            v_t = v_ref[rows, :].T.astype(BF16)
            vt_scr[n] = v_t
            vtp_scr[n // 2, :, (n % 2) * blk:(n % 2 + 1) * blk] = v_t
            kblk = k_ref[rows, :]
            km_scr[n:n + 1, :] = jnp.mean(kblk, axis=0, keepdims=True)
            for hh in range(2):
                a = lane - (1 - hh) * dh
                aug = jnp.where(a < 3, key_off,
                                jnp.where(a < 6, float(n * blk),
                                          jnp.where(a == MOBA_AUG_MASK0 + n, 1.0, 0.0)))
                in_head = jnp.logical_and(lane >= hh * dh, lane < (hh + 1) * dh)
                ka_scr[hh, rows, :] = jnp.where(in_head, kblk, aug).astype(BF16)

    key_i = lax.broadcasted_iota(jnp.int32, (blk, blk), 0)
    qry_i = lax.broadcasted_iota(jnp.int32, (blk, blk), 1)
    causal = key_i <= qry_i
    own = pl.ds(pl.multiple_of(qi * blk, blk), blk)
    q_t = qt_scr[qi].astype(F32)
    v_t_d = vt_scr[qi]
    lane_k = lax.broadcasted_iota(jnp.int32, (nb, pair), 1)
    r8 = lax.broadcasted_iota(jnp.int32, (8, blk), 0)
    is_hi = jnp.logical_or(r8 == 0, r8 == 3)
    is_mid = jnp.logical_or(r8 == 1, r8 == 4)
    pad_rows = jnp.zeros((dh - MOBA_AUG_MASK0 - nb, blk), F32)

    q_aug, init = [], []
    for hh in range(2):
        hrows = slice(hh * dh, (hh + 1) * dh)
        s_hi, s_mid, s_lo = _split3(slope_ref[pl.ds(2 * hp + hh, 1), :])
        slope_rows = jnp.where(is_hi, s_hi, jnp.where(is_mid, s_mid, s_lo))
        slope_rows = jnp.where(r8 < 6, slope_rows, 0.0)

        def with_aug(aug_rows, hh=hh, hrows=hrows):
            parts = [q_t[hrows], aug_rows] if hh == 0 else [aug_rows, q_t[hrows]]
            return jnp.concatenate(parts, axis=0).astype(BF16)

        q_diag = with_aug(jnp.concatenate([slope_rows, jnp.zeros((dh - 8, blk), F32)], axis=0))
        in_head = jnp.logical_and(lane_k >= hh * dh, lane_k < (hh + 1) * dh)
        km_h = jnp.where(in_head, km_scr[...], 0.0).astype(BF16)
        gate_t = jnp.dot(km_h, q_diag, preferred_element_type=F32)
        sel = _topk_mask(gate_t, qi, 0)
        mask_rows = jnp.where(sel > 0.0, 0.0, MOBA_NEG)
        q_aug.append(with_aug(jnp.concatenate([slope_rows, mask_rows, pad_rows], axis=0)))

        s = jnp.dot(ka_scr[hh, own, :], q_diag, preferred_element_type=F32)
        s = jnp.where(causal, s, -jnp.inf)
        m0 = jnp.max(s, axis=0, keepdims=True)
        p = jnp.exp(s - m0)
        l0 = jnp.sum(p, axis=0, keepdims=True)
        acc0 = jnp.dot(v_t_d[hrows, :], p.astype(BF16), preferred_element_type=F32)
        init += [m0, l0, acc0]

    def scores(pair_idx, slot):
        keys = pl.ds(pl.multiple_of(pair_idx * 2 * blk, 2 * blk), 2 * blk)
        col_max = []
        for hh in range(2):
            s = jnp.dot(ka_scr[hh, keys, :], q_aug[hh], preferred_element_type=F32)
            s_scr[slot, hh] = s
            col_max.append(jnp.max(s, axis=0, keepdims=True))
        return col_max

    n_pairs = (qi + 1) // 2

    def body(jj, carry):
        slot = jj % 2
        v_pair = vtp_scr[jj]
        out = []
        for hh in range(2):
            m, l, acc, mx = carry[4 * hh:4 * hh + 4]
            m_new = jnp.maximum(m, mx)
            alpha = jnp.exp(m - m_new)
            p = jnp.exp(s_scr[slot, hh] - m_new)
            l = alpha * l + jnp.sum(p, axis=0, keepdims=True)
            acc = alpha * acc + jnp.dot(v_pair[hh * dh:(hh + 1) * dh, :], p.astype(BF16),
                                        preferred_element_type=F32)
            out.append([m_new, l, acc])
        nxt = scores(jnp.minimum(jj + 1, nb // 2 - 1), 1 - slot)
        return tuple(out[0] + [nxt[0]] + out[1] + [nxt[1]])

    mx0 = scores(0, 0)
    res = lax.fori_loop(0, n_pairs, body, tuple(init[0:3] + [mx0[0]] + init[3:6] + [mx0[1]]))
    o_ref[...] = jnp.concatenate([res[2] / res[1], res[6] / res[5]], axis=0).T


def _moba_prompt(q, k, v, slopes, batch, seq):
    m = q.shape[0]
    nb = seq // MOBA_BLOCK
    assert nb % 2 == 0 and MOBA_AUG_MASK0 + nb <= MOBA_DH
    pair = 2 * MOBA_DH
    n_pairs = MOBA_HEADS // 2
    slope_rows = jnp.broadcast_to(slopes[:, None], (MOBA_HEADS, MOBA_BLOCK))
    seq_spec = pl.BlockSpec((seq, pair), lambda b, hp, qi: (b, hp))
    return pl.pallas_call(
        _moba_prompt_kernel,
        grid=(batch, n_pairs, nb),
        in_specs=[seq_spec, seq_spec, seq_spec, _full((MOBA_HEADS, MOBA_BLOCK))],
        out_specs=pl.BlockSpec((MOBA_BLOCK, pair), lambda b, hp, qi: (b * nb + qi, hp)),
        out_shape=jax.ShapeDtypeStruct((m, D_MODEL), F32),
        scratch_shapes=[pltpu.VMEM((nb, pair, MOBA_BLOCK), BF16), pltpu.VMEM((nb, pair, MOBA_BLOCK), BF16),
                        pltpu.VMEM((nb // 2, pair, 2 * MOBA_BLOCK), BF16),
                        pltpu.VMEM((2, seq, pair), BF16), pltpu.VMEM((nb, pair), F32),
                        pltpu.VMEM((2, 2, 2 * MOBA_BLOCK, MOBA_BLOCK), F32)],
        compiler_params=_params(3),
        name="moba_prompt",
    )(q, k, v, slope_rows)


def _moba_sample_kernel(n_pages, pos0, pt_ref, q_ref, kn_ref, vn_ref, slope_ref, *refs):
    del pt_ref
    k_pages = refs[:n_pages]
    v_pages = refs[n_pages:2 * n_pages]
    o_ref = refs[2 * n_pages]
    per_blk = MOBA_BLOCK // PAGE_SIZE
    n_blk = n_pages // per_blk
    h_i = lax.broadcasted_iota(jnp.int32, (MOBA_HEADS, D_MODEL), 0)
    d_i = lax.broadcasted_iota(jnp.int32, (MOBA_HEADS, D_MODEL), 1)
    own_head = d_i // MOBA_DH == h_i
    q_bd = jnp.where(own_head, q_ref[0], 0.0)
    q_bd16 = q_bd.astype(BF16)
    slope = slope_ref[...]

    lane_n = lax.broadcasted_iota(jnp.int32, (D_MODEL, n_blk), 1)
    means = jnp.zeros((D_MODEL, n_blk), F32)
    for n in range(n_blk):
        tot = jnp.sum(sum(k_pages[n * per_blk + i][...] for i in range(per_blk)), axis=1, keepdims=True)
        means = jnp.where(lane_n == n, tot / float(MOBA_BLOCK), means)
    gate = jnp.dot(q_bd16, means.astype(BF16), preferred_element_type=F32)
    sel = _topk_mask(gate, n_blk, 1)

    lane = lax.broadcasted_iota(jnp.int32, (1, PAGE_SIZE), 1)
    scores = []
    for pg in range(n_pages):
        s = jnp.dot(q_bd16, k_pages[pg][...].astype(BF16), preferred_element_type=F32)
        dist = (pos0 - pg * PAGE_SIZE - lane).astype(F32)
        s = s - slope * dist
        n = pg // per_blk
        scores.append(jnp.where(sel[:, n:n + 1] > 0.0, s, -jnp.inf))
    k_new = kn_ref[0].astype(BF16).astype(F32)
    s_new = jnp.sum(q_bd16.astype(F32) * k_new, axis=1, keepdims=True)
    m = s_new
    for s in scores:
        m = jnp.maximum(m, jnp.max(s, axis=1, keepdims=True))
    p_new = jnp.exp(s_new - m)
    l = p_new
    d_e = lax.broadcasted_iota(jnp.int32, (D_MODEL, MOBA_HEADS), 0)
    h_e = lax.broadcasted_iota(jnp.int32, (D_MODEL, MOBA_HEADS), 1)
    expand = jnp.where(d_e // MOBA_DH == h_e, 1.0, 0.0).astype(BF16)
    acc = jnp.zeros((D_MODEL, PAGE_SIZE), F32)
    for pg in range(n_pages):
        p = jnp.exp(scores[pg] - m)
        l = l + jnp.sum(p, axis=1, keepdims=True)
        acc = acc + jnp.dot(expand, p.astype(BF16), preferred_element_type=F32) * v_pages[pg][...]
    o_past = jnp.sum(acc.T, axis=0, keepdims=True)
    row_of = lambda col: jnp.sum(jnp.where(own_head, col, 0.0), axis=0, keepdims=True)
    o_new = row_of(p_new.astype(BF16).astype(F32)) * vn_ref[0]
    o_ref[0] = (o_past + o_new) / row_of(l)


def _moba_sample(q, k_new, v_new, cache_k, cache_v, page_table, slopes, pos0):
    m = q.shape[0]
    n_pages = page_table.shape[1]
    assert pos0 == n_pages * PAGE_SIZE and pos0 % MOBA_BLOCK == 0
    n_phys = cache_k.shape[0]
    ck = jnp.transpose(cache_k, (0, 2, 3, 1)).reshape(n_phys, D_MODEL, PAGE_SIZE)
    cv = jnp.transpose(cache_v, (0, 2, 3, 1)).reshape(n_phys, D_MODEL, PAGE_SIZE)
    vec = pl.BlockSpec((1, 1, D_MODEL), lambda b, pt: (b, 0, 0))
    page_specs = [pl.BlockSpec((None, D_MODEL, PAGE_SIZE), lambda b, pt, pg=pg: (pt[b, pg], 0, 0))
                  for pg in range(n_pages)]
    grid_spec = pltpu.PrefetchScalarGridSpec(
        num_scalar_prefetch=1,
        grid=(m,),
        in_specs=[vec, vec, vec, pl.BlockSpec((MOBA_HEADS, 1), lambda b, pt: (0, 0))] + page_specs + page_specs,
        out_specs=vec,
    )
    out = pl.pallas_call(
        functools.partial(_moba_sample_kernel, n_pages, pos0),
        grid_spec=grid_spec,
        out_shape=jax.ShapeDtypeStruct((m, 1, D_MODEL), F32),
        compiler_params=_params(1),
        name="moba_sample",
    )(page_table, q.reshape(m, 1, D_MODEL), k_new.reshape(m, 1, D_MODEL), v_new.reshape(m, 1, D_MODEL),
      slopes.reshape(MOBA_HEADS, 1), *([ck] * n_pages), *([cv] * n_pages))
    return out.reshape(m, D_MODEL)


def _layer_tail(layer, xp, xs, mix_p, mix_s, p_p, p_s, w_out, ln_g, ln_b, w_router, b_router,
                w_gu, b_gu, w_down, b_down, w_pg, w_pp):
    n_p, n_s = xp.shape[0], xs.shape[0]
    lg0, lb0 = ln_g[0:1], ln_b[0:1]
    lg1, lb1 = ln_g[1:2], ln_b[1:2]
    br = b_router.reshape(1, N_EXPERTS)
    x1p, idxp, gatep = _mix(xp, mix_p, w_out, lg0, lb0, w_router, br, ROW_TILE)
    x1s, idxs, gates = _mix(xs, mix_s, w_out, lg0, lb0, w_router, br, n_s)

    idx = jnp.concatenate([idxp, idxs], axis=0)
    rank, counts = _rank(idx)
    counts = counts[0]
    bm = MOE_BLOCK
    n_tok = n_p + n_s
    n_blocks = (n_tok * TOP_K + N_EXPERTS * (bm - 1)) // bm
    padded = (counts + bm - 1) // bm * bm
    pend = jnp.cumsum(padded)
    pstart = pend - padded
    dest = pstart[idx] + rank
    n_used = (pend[-1] // bm).astype(jnp.int32)
    blk_ids = jnp.minimum(jnp.arange(n_blocks, dtype=jnp.int32), n_used - 1)
    block_e = jnp.sum((pend[None, :] <= (blk_ids * bm)[:, None]).astype(jnp.int32), axis=1)
    block_e = jnp.minimum(block_e, N_EXPERTS - 1)
    tok = jnp.broadcast_to(jnp.arange(n_tok, dtype=jnp.int32)[:, None], (n_tok, TOP_K))
    row_tok = jnp.zeros((n_blocks * bm,), jnp.int32).at[dest.reshape(-1)].set(
        tok.reshape(-1), unique_indices=True, mode="promise_in_bounds")
    x1 = jnp.concatenate([x1p, x1s], axis=0)
    gather = lambda src, rows: src.at[rows].get(mode="promise_in_bounds")
    xs_rows = gather(x1, row_tok)
    ys = _experts(layer, block_e, n_used.reshape(1), xs_rows, w_gu, b_gu, w_down, b_down)
    ysg_p = gather(ys, dest[:n_p].T.reshape(-1))
    ysg_s = gather(ys, dest[n_p:].T.reshape(-1))
    yp = _finish(x1p, ysg_p, gatep, p_p, lg1, lb1, w_pg, w_pp, ROW_TILE)
    ys_out = _finish(x1s, ysg_s, gates, p_s, lg1, lb1, w_pg, w_pp, n_s)
    return yp, ys_out


def kernel(x_prompt, x_sample, state_gla, state_pool, cache_k, cache_v, page_table, p_prompt, p_sample, w_in_ab, gla_w_alpha, gla_b_alpha, gla_norm_g, pool_w, pool_scale, w_out_ab, w_qkv_c, w_out_c, ln_g, ln_b, moe_w_router, moe_b_router, moe_w_gu, moe_b_gu, moe_w_down, moe_b_down, ple_w_gate, ple_w_proj):
    batch, seq, _ = x_prompt.shape
    n_s = x_sample.shape[0]
    n_p = batch * seq
    pos0 = page_table.shape[1] * PAGE_SIZE
    xp = x_prompt.reshape(n_p, D_MODEL)
    xs = x_sample.reshape(n_s, D_MODEL)

    def tail(i, xp, xs, mix_p, mix_s, w_out):
        return _layer_tail(i, xp, xs, mix_p, mix_s, p_prompt[i].reshape(n_p, -1), p_sample[i].reshape(n_s, -1),
                           w_out, ln_g[i], ln_b[i], moe_w_router[i], moe_b_router[i], moe_w_gu,
                           moe_b_gu, moe_w_down, moe_b_down, ple_w_gate[i], ple_w_proj[i])

    w_in = w_in_ab[0]
    c_a = 2 * QK_A + 2 * V_A
    w_main = jnp.concatenate([w_in[:, :c_a], w_in[:, c_a + GLA_LOWRANK:]], axis=1)
    w_a = w_in[:, c_a:c_a + GLA_LOWRANK]
    b_alpha = gla_b_alpha[0].reshape(1, QK_A)
    norm_g = gla_norm_g[0].reshape(1, GLA_DV)
    scale = pool_scale[0].reshape(1, POOL_CH)
    qkl_p, v_p, sg_p, u_p = _proj_ab(xp, w_main, w_a, gla_w_alpha[0], b_alpha, ROW_TILE)
    qkl_s, v_s, sg_s, u_s = _proj_ab(xs, w_main, w_a, gla_w_alpha[0], b_alpha, n_s)
    mix_p, gla_p = _gla_prompt(qkl_p, v_p, sg_p, norm_g, batch, seq, ROW_TILE)
    mix_p = _pool_prompt(u_p, mix_p, pool_w[0], scale, batch, seq, ROW_TILE)
    mix_s, gla_s = _gla_sample(qkl_s, v_s, sg_s, norm_g, state_gla[0].reshape(n_s, QK_A, GLA_DV))
    mix_s, pool_s = _pool_sample(u_s, state_pool[0], mix_s, pool_w[0], scale, pos0)
    pool_p = u_p.reshape(batch, seq, POOL_CH)[:, seq - POOL_HIST:]
    xp, xs = tail(0, xp, xs, mix_p, mix_s, w_out_ab[0])

    slopes = jnp.exp2(-8.0 * jnp.arange(1, MOBA_HEADS + 1, dtype=F32) / MOBA_HEADS)
    q_p, k_p, v_p2 = _qkv(xp, w_qkv_c[0], ROW_TILE)
    q_s, k_s, v_s2 = _qkv(xs, w_qkv_c[0], n_s)
    o_p = _moba_prompt(q_p, k_p, v_p2, slopes, batch, seq)
    o_s = _moba_sample(q_s, k_s, v_s2, cache_k[0], cache_v[0], page_table, slopes, pos0)
    xp, xs = tail(1, xp, xs, o_p, o_s, w_out_c[0])

    hd = (MOBA_HEADS, MOBA_DH)
    return (xp.reshape(batch, seq, D_MODEL), xs.reshape(n_s, 1, D_MODEL),
            gla_p.reshape(1, batch, GLA_HEADS, GLA_DK, GLA_DV), gla_s.reshape(1, n_s, GLA_HEADS, GLA_DK, GLA_DV),
            pool_p[None], pool_s[None],
            k_p.reshape(1, batch, seq, *hd), v_p2.reshape(1, batch, seq, *hd),
            k_s.reshape(1, n_s, 1, *hd), v_s2.reshape(1, n_s, 1, *hd))
```

```python
import functools
import math

import numpy as np
import jax
import jax.numpy as jnp
from jax import lax
from jax.experimental import pallas as pl
from jax.experimental.pallas import tpu as pltpu

F32 = jnp.float32
BF16 = jnp.bfloat16
HI = lax.Precision.HIGHEST

D_MODEL = 1024
GLA_HEADS = 4
GLA_DK = 64
GLA_DV = 128
GLA_LOWRANK = 16
GLA_TAU = 16.0
QK_A = GLA_HEADS * GLA_DK
V_A = GLA_HEADS * GLA_DV
POOL_WINDOWS = (2, 4, 8, 16)
POOL_CH = 512
POOL_GROUP = 128
POOL_HIST = 15
MOBA_HEADS = 16
MOBA_DH = 64
MOBA_BLOCK = 256
MOBA_TOPK = 3
PAGE_SIZE = 128
N_EXPERTS = 32
TOP_K = 4
D_EXPERT = 1024
SWIGLU_LIMIT = 7.0
SWIGLU_ALPHA = 1.702
DEPTH = 2
DEEPNORM_ALPHA = (2 * DEPTH) ** 0.25
LN_EPS = 1e-5

ROW_TILE = 512
GLA_CHUNK = 128
GLA_LEVELS = 7
MOE_BLOCK = 512
RANK_TILE = 384
VMEM_LIMIT = 56 * 1024 * 1024


def _params(n_axes, vmem=VMEM_LIMIT):
    return pltpu.CompilerParams(dimension_semantics=("arbitrary",) * n_axes, vmem_limit_bytes=vmem)


def _bdot(a, b):
    return jnp.dot(a.astype(BF16), b.astype(BF16), preferred_element_type=F32)


def _layer_norm(y, g, b):
    mu = jnp.mean(y, axis=-1, keepdims=True)
    yc = y - mu
    var = jnp.mean(yc * yc, axis=-1, keepdims=True)
    return yc * lax.rsqrt(var + LN_EPS) * g + b


def _full(shape):
    n = len(shape)
    return pl.BlockSpec(shape, lambda *_: (0,) * n)


def _proj_ab_kernel(x_ref, wm_ref, wa_ref, walpha_ref, balpha_ref,
                    qkl_ref, v_ref, sg_ref, u_ref, wm_bf, wa_bf):
    @pl.when(pl.program_id(0) == 0)
    def _():
        wm_bf[...] = wm_ref[...].astype(BF16)
        wa_bf[...] = wa_ref[...].astype(BF16)

    xb = x_ref[...].astype(BF16)
    qk = jnp.dot(xb, wm_bf[:, 0:2 * QK_A], preferred_element_type=F32)
    qkl_ref[:, 0:QK_A] = qk[:, 0:QK_A] * (GLA_DK ** -0.5)
    qkl_ref[:, QK_A:2 * QK_A] = qk[:, QK_A:2 * QK_A]
    a_lr = jnp.dot(xb, wa_bf[...], preferred_element_type=F32)
    z = _bdot(a_lr, walpha_ref[...]) + balpha_ref[...]
    log_sig = jnp.minimum(z, 0.0) - jnp.log1p(jnp.exp(-jnp.abs(z)))
    qkl_ref[:, 2 * QK_A:3 * QK_A] = log_sig / GLA_TAU
    c0 = 2 * QK_A
    v_ref[...] = jnp.dot(xb, wm_bf[:, c0:c0 + V_A], preferred_element_type=F32)
    g = jnp.dot(xb, wm_bf[:, c0 + V_A:c0 + 2 * V_A], preferred_element_type=F32)
    sg_ref[...] = g * jax.nn.sigmoid(g)
    u_ref[...] = jnp.dot(xb, wm_bf[:, c0 + 2 * V_A:c0 + 2 * V_A + POOL_CH], preferred_element_type=F32)


def _proj_ab(x, w_main, w_a, w_alpha, b_alpha, tm):
    m = x.shape[0]
    nmain = w_main.shape[1]
    row = lambda w: pl.BlockSpec((tm, w), lambda i: (i, 0))
    return pl.pallas_call(
        _proj_ab_kernel,
        grid=(m // tm,),
        in_specs=[row(D_MODEL), _full((D_MODEL, nmain)), _full((D_MODEL, GLA_LOWRANK)),
                  _full((GLA_LOWRANK, QK_A)), _full((1, QK_A))],
        out_specs=[row(3 * QK_A), row(V_A), row(V_A), row(POOL_CH)],
        out_shape=[jax.ShapeDtypeStruct((m, 3 * QK_A), F32), jax.ShapeDtypeStruct((m, V_A), F32),
                   jax.ShapeDtypeStruct((m, V_A), F32), jax.ShapeDtypeStruct((m, POOL_CH), F32)],
        scratch_shapes=[pltpu.VMEM((D_MODEL, nmain), BF16), pltpu.VMEM((D_MODEL, GLA_LOWRANK), BF16)],
        compiler_params=_params(1),
        name="proj_ab",
    )(x, w_main, w_a, w_alpha, b_alpha)


def _gla_tables():
    c = GLA_CHUNK
    i = np.arange(c)[:, None]
    s = np.arange(c)[None, :]
    mats = [(s <= i), (s > i)]
    for lev in range(GLA_LEVELS):
        p = GLA_LEVELS - 1 - lev
        half = 1 << p
        start = (i >> (p + 1)) << (p + 1)
        mid = start + half - 1
        upper = i >= start + half
        mats.append(np.where(upper, (s > mid) & (s <= i), (s > i) & (s <= mid)))
    seg = np.concatenate(mats, axis=0).astype(np.float32)
    j = np.arange(c)[None, :]
    x = i ^ j
    lvl = np.full((c, c), GLA_LEVELS + 1, np.int32)
    lvl[np.arange(c), np.arange(c)] = GLA_LEVELS
    for lev in range(GLA_LEVELS):
        p = GLA_LEVELS - 1 - lev
        lvl = np.where(((x >> p) == 1) & (((i >> p) & 1) == 1), lev, lvl)
    lvl4 = np.tile(lvl, (GLA_HEADS, 1)).astype(np.int32)
    lane_head = (np.arange(QK_A) // GLA_DK)[None, :]
    row_head = (np.arange(GLA_HEADS * c) // c)[:, None]
    hm4 = (lane_head == row_head).astype(np.float32)
    return seg, lvl4, hm4


def _gla_chunk(q, k, la, v, s_all, seg, lvl4, hm4):
    c = GLA_CHUNK
    e = sum(jnp.dot(seg, part.astype(BF16), preferred_element_type=F32) for part in _split3(la))
    w = jnp.exp(e)
    w_b = w[0:c]
    w_k = w[c:2 * c]

    def stack_heads(t):
        return (jnp.concatenate([t] * GLA_HEADS, axis=0) * hm4).astype(BF16)

    o_inter = jnp.dot(stack_heads(q * w_b), s_all.astype(BF16), preferred_element_type=F32)
    a = jnp.zeros((GLA_HEADS * c, c), F32)
    for lev in range(GLA_LEVELS + 1):
        if lev < GLA_LEVELS:
            w_l = w[(2 + lev) * c:(3 + lev) * c]
            ql, kl = q * w_l, k * w_l
        else:
            ql, kl = q, k
        p_l = lax.dot_general(stack_heads(ql), kl.astype(BF16), (((1,), (1,)), ((), ())),
                              preferred_element_type=F32)
        a = jnp.where(lvl4 == lev, p_l, a)
    a = a.astype(BF16)
    outs = []
    for h in range(GLA_HEADS):
        v_h = v[:, h * GLA_DV:(h + 1) * GLA_DV].astype(BF16)
        o_h = o_inter[h * c:(h + 1) * c] + jnp.dot(a[h * c:(h + 1) * c], v_h, preferred_element_type=F32)
        outs.append(o_h)
    ks_t = (k * w_k).T.astype(BF16)
    kv = jnp.dot(ks_t, v.astype(BF16), preferred_element_type=F32)
    dec = jnp.exp(jnp.sum(la.T, axis=1, keepdims=True))
    new_rows = []
    for h in range(GLA_HEADS):
        rows = slice(h * GLA_DK, (h + 1) * GLA_DK)
        new_rows.append(dec[rows] * s_all[rows] + kv[rows, h * GLA_DV:(h + 1) * GLA_DV])
    return outs, jnp.concatenate(new_rows, axis=0)


def _gla_finish(o_h, norm_g, sg_h):
    o_h = o_h * lax.rsqrt(jnp.mean(o_h * o_h, axis=-1, keepdims=True) + LN_EPS) * norm_g
    return o_h * sg_h


def _gla_prompt_kernel(qkl_ref, v_ref, sg_ref, ng_ref, seg_ref, lvl_ref, hm_ref,
                       mix_ref, state_ref, s_scr):
    t = pl.program_id(1)

    @pl.when(t == 0)
    def _():
        s_scr[...] = jnp.zeros_like(s_scr)

    seg = seg_ref[...]
    lvl4 = lvl_ref[...]
    hm4 = hm_ref[...]
    norm_g = ng_ref[...]
    n_chunks = qkl_ref.shape[0] // GLA_CHUNK

    def body(ci, carry):
        r0 = pl.multiple_of(ci * GLA_CHUNK, GLA_CHUNK)
        rows = pl.ds(r0, GLA_CHUNK)
        q = qkl_ref[rows, 0:QK_A]
        k = qkl_ref[rows, QK_A:2 * QK_A]
        la = qkl_ref[rows, 2 * QK_A:3 * QK_A]
        v = v_ref[rows, :]
        outs, s_new = _gla_chunk(q, k, la, v, s_scr[...], seg, lvl4, hm4)
        s_scr[...] = s_new
        for h in range(GLA_HEADS):
            cols = slice(h * GLA_DV, (h + 1) * GLA_DV)
            mix_ref[rows, cols] = _gla_finish(outs[h], norm_g, sg_ref[rows, cols])
        return carry

    lax.fori_loop(0, n_chunks, body, 0)

    @pl.when(t == pl.num_programs(1) - 1)
    def _():
        state_ref[0] = s_scr[...]


def _gla_prompt(qkl, v, sg, norm_g, batch, seq, tm):
    m = qkl.shape[0]
    nt = seq // tm
    seg, lvl4, hm4 = _gla_tables()
    row = lambda w: pl.BlockSpec((tm, w), lambda b, t: (b * nt + t, 0))
    return pl.pallas_call(
        _gla_prompt_kernel,
        grid=(batch, nt),
        in_specs=[row(3 * QK_A), row(V_A), row(V_A), _full((1, GLA_DV)),
                  _full(seg.shape), _full(lvl4.shape), _full(hm4.shape)],
        out_specs=[pl.BlockSpec((tm, V_A), lambda b, t: (b * nt + t, 0)),
                   pl.BlockSpec((1, QK_A, GLA_DV), lambda b, t: (b, 0, 0))],
        out_shape=[jax.ShapeDtypeStruct((m, D_MODEL), F32),
                   jax.ShapeDtypeStruct((batch, QK_A, GLA_DV), F32)],
        scratch_shapes=[pltpu.VMEM((QK_A, GLA_DV), F32)],
        compiler_params=_params(2),
        name="gla_prompt",
    )(qkl, v, sg, norm_g, jnp.asarray(seg, BF16), jnp.asarray(lvl4), jnp.asarray(hm4))


GLA_SAMPLE_ROWS = 8


def _gla_sample_kernel(qkl_ref, v_ref, sg_ref, ng_ref, s0_ref, mix_ref, s1_ref):
    ones = jnp.ones((QK_A, GLA_DV), F32)
    r_i = lax.broadcasted_iota(jnp.int32, (QK_A, QK_A), 0)
    c_i = lax.broadcasted_iota(jnp.int32, (QK_A, QK_A), 1)
    eye = r_i == c_i
    norm_g = ng_ref[...]

    def col_bcast(row):
        diag = jnp.where(eye, jnp.broadcast_to(row, (QK_A, QK_A)), 0.0)
        return jnp.dot(diag, ones, precision=HI, preferred_element_type=F32)

    for r in range(GLA_SAMPLE_ROWS):
        q = col_bcast(qkl_ref[r:r + 1, 0:QK_A])
        k = col_bcast(qkl_ref[r:r + 1, QK_A:2 * QK_A])
        dec = jnp.exp(col_bcast(qkl_ref[r:r + 1, 2 * QK_A:3 * QK_A]))
        for h in range(GLA_HEADS):
            rows = slice(h * GLA_DK, (h + 1) * GLA_DK)
            cols = slice(h * GLA_DV, (h + 1) * GLA_DV)
            s_new = dec[rows] * s0_ref[r, rows, :] + k[rows] * v_ref[r:r + 1, cols]
            s1_ref[r, rows, :] = s_new
            o_h = jnp.sum(q[rows] * s_new, axis=0, keepdims=True)
            mix_ref[r:r + 1, cols] = _gla_finish(o_h, norm_g, sg_ref[r:r + 1, cols])


def _gla_sample(qkl, v, sg, norm_g, s0):
    m = qkl.shape[0]
    rb = GLA_SAMPLE_ROWS
    row = lambda w: pl.BlockSpec((rb, w), lambda i: (i, 0))
    st = pl.BlockSpec((rb, QK_A, GLA_DV), lambda i: (i, 0, 0))
    return pl.pallas_call(
        _gla_sample_kernel,
        grid=(m // rb,),
        in_specs=[row(3 * QK_A), row(V_A), row(V_A), _full((1, GLA_DV)), st],
        out_specs=[pl.BlockSpec((rb, V_A), lambda i: (i, 0)), st],
        out_shape=[jax.ShapeDtypeStruct((m, D_MODEL), F32),
                   jax.ShapeDtypeStruct((m, QK_A, GLA_DV), F32)],
        compiler_params=_params(1),
        name="gla_sample",
    )(qkl, v, sg, norm_g, s0)


def _pool_project(d_groups, wp_ref, scale_ref, mix_ref):
    for g in range(len(POOL_WINDOWS)):
        cols = slice(g * POOL_GROUP, (g + 1) * POOL_GROUP)
        mix_ref[:, cols] = _bdot(d_groups[g], wp_ref[g]) * scale_ref[:, cols]


def _pool_prompt_kernel(u_ref, prev_ref, wp_ref, scale_ref, mixin_ref, mix_ref):
    del mixin_ref
    t = pl.program_id(1)
    tm = u_ref.shape[0]
    hist = prev_ref.shape[0]
    prev = jnp.where(t > 0, prev_ref[...], 0.0)
    z = jnp.concatenate([prev, u_ref[...]], axis=0)
    pos = t * tm + lax.broadcasted_iota(jnp.int32, (tm, 1), 0)
    d_groups = []
    for g, w in enumerate(POOL_WINDOWS):
        cols = slice(g * POOL_GROUP, (g + 1) * POOL_GROUP)
        s = z[:, cols]
        shift = 1
        while shift < w:
            s = s + pltpu.roll(s, shift, 0)
            shift *= 2
        cnt = jnp.minimum(w, pos + 1).astype(F32)
        d_groups.append(s[hist:] / cnt - z[hist:, cols])
    _pool_project(d_groups, wp_ref, scale_ref, mix_ref)


def _pool_prompt(u, mix, w_pool, scale, batch, seq, tm):
    nt = seq // tm
    hist = 16
    assert hist > POOL_HIST and tm % hist == 0
    per = tm // hist
    return pl.pallas_call(
        _pool_prompt_kernel,
        grid=(batch, nt),
        in_specs=[pl.BlockSpec((tm, POOL_CH), lambda b, t: (b * nt + t, 0)),
                  pl.BlockSpec((hist, POOL_CH), lambda b, t: (jnp.maximum((b * nt + t) * per - 1, 0), 0)),
                  _full(w_pool.shape), _full((1, POOL_CH)),
                  pl.BlockSpec(memory_space=pl.ANY)],
        out_specs=pl.BlockSpec((tm, POOL_CH), lambda b, t: (b * nt + t, 1)),
        out_shape=jax.ShapeDtypeStruct(mix.shape, F32),
        input_output_aliases={4: 0},
        compiler_params=_params(2),
        name="pool_prompt",
    )(u, u, w_pool, scale, mix)


def _pool_sample_kernel(u_ref, st_ref, wp_ref, scale_ref, mixin_ref, mix_ref, st_out_ref):
    del mixin_ref
    u = u_ref[...]
    d_groups = []
    for g, w in enumerate(POOL_WINDOWS):
        cols = slice(g * POOL_GROUP, (g + 1) * POOL_GROUP)
        win = u[:, cols] + jnp.sum(st_ref[:, POOL_HIST - (w - 1):POOL_HIST, cols], axis=1)
        d_groups.append(win / float(w) - u[:, cols])
    _pool_project(d_groups, wp_ref, scale_ref, mix_ref)
    st_out_ref[:, 0:POOL_HIST - 1, :] = st_ref[:, 1:POOL_HIST, :]
    st_out_ref[:, POOL_HIST - 1:POOL_HIST, :] = u[:, None, :]


def _pool_sample(u, st, mix, w_pool, scale, pos0):
    m = u.shape[0]
    assert pos0 + 1 >= max(POOL_WINDOWS)
    return pl.pallas_call(
        _pool_sample_kernel,
        grid=(1,),
        in_specs=[_full((m, POOL_CH)), _full(st.shape), _full(w_pool.shape), _full((1, POOL_CH)),
                  pl.BlockSpec(memory_space=pl.ANY)],
        out_specs=[pl.BlockSpec((m, POOL_CH), lambda i: (0, 1)), _full(st.shape)],
        out_shape=[jax.ShapeDtypeStruct(mix.shape, F32), jax.ShapeDtypeStruct(st.shape, F32)],
        input_output_aliases={4: 0},
        compiler_params=_params(1),
        name="pool_sample",
    )(u, st, w_pool, scale, mix)


def _mix_kernel(x_ref, mix_ref, w_ref, lng_ref, lnb_ref, wr_ref, br_ref,
                x1_ref, idx_ref, gate_ref, w_bf):
    @pl.when(pl.program_id(0) == 0)
    def _():
        w_bf[...] = w_ref[...].astype(BF16)

    h = jnp.dot(mix_ref[...].astype(BF16), w_bf[...], preferred_element_type=F32)
    x1 = _layer_norm(DEEPNORM_ALPHA * x_ref[...] + h, lng_ref[...], lnb_ref[...])
    x1_ref[...] = x1
    logits = _bdot(x1, wr_ref[...]) + br_ref[...]
    tm = logits.shape[0]
    lane = lax.broadcasted_iota(jnp.int32, (tm, N_EXPERTS), 1)
    lane_k = lax.broadcasted_iota(jnp.int32, (tm, TOP_K), 1)
    idx_out = jnp.zeros((tm, TOP_K), jnp.int32)
    val_out = jnp.zeros((tm, TOP_K), F32)
    cur = logits
    for kk in range(TOP_K):
        mval = jnp.max(cur, axis=1, keepdims=True)
        midx = jnp.min(jnp.where(cur == mval, lane.astype(F32), float(N_EXPERTS)), axis=1,
                       keepdims=True).astype(jnp.int32)
        idx_out = jnp.where(lane_k == kk, midx, idx_out)
        val_out = jnp.where(lane_k == kk, mval, val_out)
        cur = jnp.where(lane == midx, -jnp.inf, cur)
    ex = jnp.exp(val_out - val_out[:, 0:1])
    idx_ref[...] = idx_out
    gate_ref[...] = ex / jnp.sum(ex, axis=1, keepdims=True)


def _mix(x, mix, w_out, ln_g, ln_b, w_router, b_router, tm):
    m = x.shape[0]
    row = lambda w: pl.BlockSpec((tm, w), lambda i: (i, 0))
    return pl.pallas_call(
        _mix_kernel,
        grid=(m // tm,),
        in_specs=[row(D_MODEL), row(D_MODEL), _full((D_MODEL, D_MODEL)), _full((1, D_MODEL)),
                  _full((1, D_MODEL)), _full((D_MODEL, N_EXPERTS)), _full((1, N_EXPERTS))],
        out_specs=[row(D_MODEL), row(TOP_K), row(TOP_K)],
        out_shape=[jax.ShapeDtypeStruct((m, D_MODEL), F32),
                   jax.ShapeDtypeStruct((m, TOP_K), jnp.int32), jax.ShapeDtypeStruct((m, TOP_K), F32)],
        scratch_shapes=[pltpu.VMEM((D_MODEL, D_MODEL), BF16)],
        compiler_params=_params(1),
        name="mix",
    )(x, mix, w_out, ln_g, ln_b, w_router, b_router)


def _rank_kernel(idx_ref, rank_ref, cnt_ref, carry):
    i = pl.program_id(0)

    @pl.when(i == 0)
    def _():
        carry[...] = jnp.zeros_like(carry)

    idx = idx_ref[...]
    tm = idx.shape[0]
    lane = lax.broadcasted_iota(jnp.int32, (tm, N_EXPERTS), 1)
    onehots = [(idx[:, kk:kk + 1] == lane) for kk in range(TOP_K)]
    member = sum(jnp.where(o, 1.0, 0.0) for o in onehots)
    r_i = lax.broadcasted_iota(jnp.int32, (tm, tm), 0)
    c_i = lax.broadcasted_iota(jnp.int32, (tm, tm), 1)
    strict_lower = jnp.where(c_i < r_i, 1.0, 0.0).astype(BF16)
    before = jnp.dot(strict_lower, member.astype(BF16), preferred_element_type=F32) + carry[...]
    lane_k = lax.broadcasted_iota(jnp.int32, (tm, TOP_K), 1)
    rank = jnp.zeros((tm, TOP_K), F32)
    for kk in range(TOP_K):
        r_k = jnp.sum(jnp.where(onehots[kk], before, 0.0), axis=1, keepdims=True)
        rank = jnp.where(lane_k == kk, r_k, rank)
    rank_ref[...] = rank.astype(jnp.int32)
    carry[...] = carry[...] + jnp.sum(member, axis=0, keepdims=True)
    cnt_ref[...] = carry[...].astype(jnp.int32)


def _rank(idx):
    n = idx.shape[0]
    tm = RANK_TILE
    return pl.pallas_call(
        _rank_kernel,
        grid=(n // tm,),
        in_specs=[pl.BlockSpec((tm, TOP_K), lambda i: (i, 0))],
        out_specs=[pl.BlockSpec((tm, TOP_K), lambda i: (i, 0)), _full((1, N_EXPERTS))],
        out_shape=[jax.ShapeDtypeStruct((n, TOP_K), jnp.int32),
                   jax.ShapeDtypeStruct((1, N_EXPERTS), jnp.int32)],
        scratch_shapes=[pltpu.VMEM((1, N_EXPERTS), F32)],
        compiler_params=_params(1),
        name="moe_rank",
    )(idx)


def _expert_kernel(be_ref, nu_ref, xs_ref, wgu_ref, bgu_ref, wd_ref, bd_ref, ys_ref, wgu_bf, wd_bf):
    blk = pl.program_id(0)
    prev = be_ref[jnp.maximum(blk - 1, 0)]
    fresh = jnp.logical_or(blk == 0, be_ref[blk] != prev)
    used = blk < nu_ref[0]

    @pl.when(jnp.logical_and(fresh, used))
    def _():
        wgu_bf[...] = wgu_ref[0, 0].astype(BF16)
        wd_bf[...] = wd_ref[0, 0].astype(BF16)

    @pl.when(used)
    def _():
        xb = xs_ref[...].astype(BF16)
        acc = jnp.zeros(ys_ref.shape, F32) + bd_ref[0, 0]
        half = D_EXPERT // 2
        for c in range(2):
            cg = slice(c * half, (c + 1) * half)
            cu = slice(D_EXPERT + c * half, D_EXPERT + (c + 1) * half)
            gate = jnp.dot(xb, wgu_bf[:, cg], preferred_element_type=F32) + bgu_ref[0, 0, :, cg]
            up = jnp.dot(xb, wgu_bf[:, cu], preferred_element_type=F32) + bgu_ref[0, 0, :, cu]
            gate = jnp.minimum(gate, SWIGLU_LIMIT)
            up = jnp.clip(up, -SWIGLU_LIMIT, SWIGLU_LIMIT)
            act = (up + 1.0) * gate * jax.nn.sigmoid(SWIGLU_ALPHA * gate)
            acc = acc + jnp.dot(act.astype(BF16), wd_bf[cg, :], preferred_element_type=F32)
        ys_ref[...] = acc

    @pl.when(jnp.logical_not(used))
    def _():
        ys_ref[...] = jnp.zeros_like(ys_ref)


def _experts(layer, block_e, n_used, xs, w_gu, b_gu, w_down, b_down):
    p = xs.shape[0]
    bm = MOE_BLOCK
    nb = p // bm
    depth = w_gu.shape[0]
    last = lambda blk, nu: jnp.minimum(blk, nu[0] - 1)
    grid_spec = pltpu.PrefetchScalarGridSpec(
        num_scalar_prefetch=2,
        grid=(nb,),
        in_specs=[pl.BlockSpec((bm, D_MODEL), lambda blk, be, nu: (last(blk, nu), 0)),
                  pl.BlockSpec((1, 1, D_MODEL, 2 * D_EXPERT), lambda blk, be, nu: (layer, be[blk], 0, 0)),
                  pl.BlockSpec((1, 1, 1, 2 * D_EXPERT), lambda blk, be, nu: (layer, be[blk], 0, 0)),
                  pl.BlockSpec((1, 1, D_EXPERT, D_MODEL), lambda blk, be, nu: (layer, be[blk], 0, 0)),
                  pl.BlockSpec((1, 1, 1, D_MODEL), lambda blk, be, nu: (layer, be[blk], 0, 0))],
        out_specs=pl.BlockSpec((bm, D_MODEL), lambda blk, be, nu: (blk, 0)),
        scratch_shapes=[pltpu.VMEM((D_MODEL, 2 * D_EXPERT), BF16), pltpu.VMEM((D_EXPERT, D_MODEL), BF16)],
    )
    return pl.pallas_call(
        _expert_kernel,
        grid_spec=grid_spec,
        out_shape=jax.ShapeDtypeStruct((p, D_MODEL), F32),
        compiler_params=_params(1),
        name="moe_experts",
    )(block_e, n_used, xs, w_gu, b_gu.reshape(depth, N_EXPERTS, 1, 2 * D_EXPERT), w_down,
      b_down.reshape(depth, N_EXPERTS, 1, D_MODEL))


def _finish_kernel(x1_ref, y0_ref, y1_ref, y2_ref, y3_ref, gate_ref, p_ref, lng_ref, lnb_ref,
                   wpg_ref, wpp_ref, out_ref, wpg_bf):
    @pl.when(pl.program_id(0) == 0)
    def _():
        wpg_bf[...] = wpg_ref[...].astype(BF16)

    gates = gate_ref[...]
    moe = y0_ref[...] * gates[:, 0:1]
    for kk, y_ref in enumerate((y1_ref, y2_ref, y3_ref), start=1):
        moe = moe + y_ref[...] * gates[:, kk:kk + 1]
    x2 = _layer_norm(DEEPNORM_ALPHA * x1_ref[...] + moe, lng_ref[...], lnb_ref[...])
    pg = jax.nn.sigmoid(jnp.dot(x2.astype(BF16), wpg_bf[...], preferred_element_type=F32))
    pp = _bdot(p_ref[...], wpp_ref[...])
    out_ref[...] = x2 + pg * pp


def _finish(x1, ysg, gates, p, ln_g, ln_b, w_pg, w_pp, tm):
    m = x1.shape[0]
    nt = m // tm
    row = lambda w: pl.BlockSpec((tm, w), lambda i: (i, 0))
    ple = p.shape[1]
    y_specs = [pl.BlockSpec((tm, D_MODEL), lambda i, kk=kk: (kk * nt + i, 0)) for kk in range(TOP_K)]
    return pl.pallas_call(
        _finish_kernel,
        grid=(nt,),
        in_specs=[row(D_MODEL)] + y_specs + [row(TOP_K), row(ple), _full((1, D_MODEL)), _full((1, D_MODEL)),
                                             _full((D_MODEL, D_MODEL)), _full((ple, D_MODEL))],
        out_specs=row(D_MODEL),
        out_shape=jax.ShapeDtypeStruct((m, D_MODEL), F32),
        scratch_shapes=[pltpu.VMEM((D_MODEL, D_MODEL), BF16)],
        compiler_params=_params(1),
        name="finish",
    )(x1, ysg, ysg, ysg, ysg, gates, p, ln_g, ln_b, w_pg, w_pp)


def _qkv_kernel(x_ref, w_ref, q_ref, k_ref, v_ref, w_bf):
    @pl.when(pl.program_id(0) == 0)
    def _():
        w_bf[...] = w_ref[...].astype(BF16)

    xb = x_ref[...].astype(BF16)
    q_ref[...] = jnp.dot(xb, w_bf[:, 0:D_MODEL], preferred_element_type=F32) * (MOBA_DH ** -0.5)
    k_ref[...] = jnp.dot(xb, w_bf[:, D_MODEL:2 * D_MODEL], preferred_element_type=F32)
    v_ref[...] = jnp.dot(xb, w_bf[:, 2 * D_MODEL:3 * D_MODEL], preferred_element_type=F32)


def _qkv(x, w_qkv, tm):
    m = x.shape[0]
    row = pl.BlockSpec((tm, D_MODEL), lambda i: (i, 0))
    shp = jax.ShapeDtypeStruct((m, D_MODEL), F32)
    return pl.pallas_call(
        _qkv_kernel,
        grid=(m // tm,),
        in_specs=[row, _full((D_MODEL, 3 * D_MODEL))],
        out_specs=[row, row, row],
        out_shape=[shp, shp, shp],
        scratch_shapes=[pltpu.VMEM((D_MODEL, 3 * D_MODEL), BF16)],
        compiler_params=_params(1),
        name="qkv",
    )(x, w_qkv)


def _topk_mask(gate, n_valid, axis):
    nb = gate.shape[axis]
    pos_i = lax.broadcasted_iota(jnp.int32, gate.shape, axis)
    valid = pos_i < n_valid
    pos = pos_i.astype(F32)
    cur = jnp.where(valid, gate, -jnp.inf)
    sel = jnp.zeros(gate.shape, F32)
    for _ in range(min(MOBA_TOPK, nb)):
        best = jnp.max(cur, axis=axis, keepdims=True)
        first = jnp.min(jnp.where(cur == best, pos, float(nb)), axis=axis, keepdims=True)
        pick = jnp.logical_and(pos == first, valid)
        sel = jnp.where(pick, 1.0, sel)
        cur = jnp.where(pick, -jnp.inf, cur)
    return sel


MOBA_AUG_MASK0 = 8
MOBA_NEG = -30720.0


def _split3(x):
    hi = x.astype(BF16).astype(F32)
    mid = (x - hi).astype(BF16).astype(F32)
    lo = (x - hi - mid).astype(BF16).astype(F32)
    return hi, mid, lo


def _moba_prompt_kernel(q_ref, k_ref, v_ref, slope_ref, o_ref, qt_scr, vtp_scr, ka_scr, km_scr, s_scr):
    hp = pl.program_id(1)
    qt = pl.program_id(2)
    nb = km_scr.shape[0]
    blk = MOBA_BLOCK
    qw = 2 * blk
    dh = MOBA_DH
    pair = 2 * dh

    @pl.when(qt == 0)
    def _():
        lane = lax.broadcasted_iota(jnp.int32, (blk, pair), 1)
        key_off = lax.broadcasted_iota(jnp.int32, (blk, pair), 0).astype(F32)
        for n in range(nb):
            rows = slice(n * blk, (n + 1) * blk)
            half = slice((n % 2) * blk, (n % 2 + 1) * blk)
            qt_scr[n // 2, :, half] = q_ref[rows, :].T.astype(BF16)
            vtp_scr[n // 2, :, half] = v_ref[rows, :].T.astype(BF16)
            kblk = k_ref[rows, :]
            km_scr[n:n + 1, :] = jnp.mean(kblk, axis=0, keepdims=True)
            for hh in range(2):
                a = lane - (1 - hh) * dh
                aug = jnp.where(a < 3, key_off,
                                jnp.where(a < 6, float(n * blk),
                                          jnp.where(a == MOBA_AUG_MASK0 + n, 1.0, 0.0)))
                in_head = jnp.logical_and(lane >= hh * dh, lane < (hh + 1) * dh)
                ka_scr[hh, rows, :] = jnp.where(in_head, kblk, aug).astype(BF16)

    key_i = lax.broadcasted_iota(jnp.int32, (qw, qw), 0)
    qry_i = lax.broadcasted_iota(jnp.int32, (qw, qw), 1)
    same_block = (key_i >= blk) == (qry_i >= blk)
    causal = jnp.logical_and(same_block, key_i <= qry_i)
    first_key_second_qry = jnp.logical_and(key_i < blk, qry_i >= blk)
    col = lax.broadcasted_iota(jnp.int32, (1, qw), 1)
    cur_blk = 2 * qt + jnp.where(col >= blk, 1, 0)
    blk_row = lax.broadcasted_iota(jnp.int32, (nb, qw), 0)

    own = pl.ds(pl.multiple_of(qt * qw, qw), qw)
    q_t = qt_scr[qt].astype(F32)
    v_t_d = vtp_scr[qt]
    lane_k = lax.broadcasted_iota(jnp.int32, (nb, pair), 1)
    r8 = lax.broadcasted_iota(jnp.int32, (8, qw), 0)
    is_hi = jnp.logical_or(r8 == 0, r8 == 3)
    is_mid = jnp.logical_or(r8 == 1, r8 == 4)
    pad_rows = jnp.zeros((dh - MOBA_AUG_MASK0 - nb, qw), F32)

    q_aug, init = [], []
    for hh in range(2):
        hrows = slice(hh * dh, (hh + 1) * dh)
        s_hi, s_mid, s_lo = _split3(slope_ref[pl.ds(2 * hp + hh, 1), :])
        slope_rows = jnp.where(is_hi, s_hi, jnp.where(is_mid, s_mid, s_lo))
        slope_rows = jnp.where(r8 < 6, slope_rows, 0.0)

        def with_aug(aug_rows, hh=hh, hrows=hrows):
            parts = [q_t[hrows], aug_rows] if hh == 0 else [aug_rows, q_t[hrows]]
            return jnp.concatenate(parts, axis=0).astype(BF16)

        q_diag = with_aug(jnp.concatenate([slope_rows, jnp.zeros((dh - 8, qw), F32)], axis=0))
        in_head = jnp.logical_and(lane_k >= hh * dh, lane_k < (hh + 1) * dh)
        km_h = jnp.where(in_head, km_scr[...], 0.0).astype(BF16)
        gate_t = jnp.dot(km_h, q_diag, preferred_element_type=F32)
        sel = _topk_mask(gate_t, cur_blk, 0)
        mask_rows = jnp.where(sel > 0.0, 0.0, MOBA_NEG)
        q_aug.append(with_aug(jnp.concatenate([slope_rows, mask_rows, pad_rows], axis=0)))

        first_sel = jnp.sum(jnp.where(blk_row == 2 * qt, sel, 0.0), axis=0, keepdims=True)
        allowed = jnp.logical_or(causal, jnp.logical_and(first_key_second_qry, first_sel > 0.0))
        s = jnp.dot(ka_scr[hh, own, :], q_diag, preferred_element_type=F32)
        s = jnp.where(allowed, s, -jnp.inf)
        m0 = jnp.max(s, axis=0, keepdims=True)
        p = jnp.exp(s - m0)
        l0 = jnp.sum(p, axis=0, keepdims=True)
        acc0 = jnp.dot(v_t_d[hrows, :], p.astype(BF16), preferred_element_type=F32)
        init += [m0, l0, acc0]

    def scores(pair_idx, slot):
        keys = pl.ds(pl.multiple_of(pair_idx * qw, qw), qw)
        col_max = []
        for hh in range(2):
            s = jnp.dot(ka_scr[hh, keys, :], q_aug[hh], preferred_element_type=F32)
            s_scr[slot, hh] = s
            col_max.append(jnp.max(s, axis=0, keepdims=True))
        return col_max

    n_pairs = qt

    def body(jj, carry):
        slot = jj % 2
        v_pair = vtp_scr[jj]
        out = []
        for hh in range(2):
            m, l, acc, mx = carry[4 * hh:4 * hh + 4]
            m_new = jnp.maximum(m, mx)
            alpha = jnp.exp(m - m_new)
            p = jnp.exp(s_scr[slot, hh] - m_new)
            l = alpha * l + jnp.sum(p, axis=0, keepdims=True)
            acc = alpha * acc + jnp.dot(v_pair[hh * dh:(hh + 1) * dh, :], p.astype(BF16),
                                        preferred_element_type=F32)
            out.append([m_new, l, acc])
        nxt = scores(jnp.minimum(jj + 1, nb // 2 - 1), 1 - slot)
        return tuple(out[0] + [nxt[0]] + out[1] + [nxt[1]])

    mx0 = scores(0, 0)
    res = lax.fori_loop(0, n_pairs, body, tuple(init[0:3] + [mx0[0]] + init[3:6] + [mx0[1]]))
    o_ref[...] = jnp.concatenate([res[2] / res[1], res[6] / res[5]], axis=0).T


def _moba_prompt(q, k, v, slopes, batch, seq):
    m = q.shape[0]
    nb = seq // MOBA_BLOCK
    assert nb % 2 == 0 and MOBA_AUG_MASK0 + nb <= MOBA_DH
    pair = 2 * MOBA_DH
    n_pairs = MOBA_HEADS // 2
    qw = 2 * MOBA_BLOCK
    nt = nb // 2
    slope_rows = jnp.broadcast_to(slopes[:, None], (MOBA_HEADS, qw))
    seq_spec = pl.BlockSpec((seq, pair), lambda b, hp, qt: (b, hp))
    return pl.pallas_call(
        _moba_prompt_kernel,
        grid=(batch, n_pairs, nt),
        in_specs=[seq_spec, seq_spec, seq_spec, _full((MOBA_HEADS, qw))],
        out_specs=pl.BlockSpec((qw, pair), lambda b, hp, qt: (b * nt + qt, hp)),
        out_shape=jax.ShapeDtypeStruct((m, D_MODEL), F32),
        scratch_shapes=[pltpu.VMEM((nt, pair, qw), BF16), pltpu.VMEM((nt, pair, qw), BF16),
                        pltpu.VMEM((2, seq, pair), BF16), pltpu.VMEM((nb, pair), F32),
                        pltpu.VMEM((2, 2, qw, qw), F32)],
        compiler_params=_params(3),
        name="moba_prompt",
    )(q, k, v, slope_rows)


def _moba_sample_kernel(n_pages, pos0, pt_ref, q_ref, kn_ref, vn_ref, slope_ref, *refs):
    del pt_ref
    k_pages = refs[:n_pages]
    v_pages = refs[n_pages:2 * n_pages]
    o_ref = refs[2 * n_pages]
    per_blk = MOBA_BLOCK // PAGE_SIZE
    n_blk = n_pages // per_blk
    h_i = lax.broadcasted_iota(jnp.int32, (MOBA_HEADS, D_MODEL), 0)
    d_i = lax.broadcasted_iota(jnp.int32, (MOBA_HEADS, D_MODEL), 1)
    own_head = d_i // MOBA_DH == h_i
    q_bd = jnp.where(own_head, q_ref[0], 0.0)
    q_bd16 = q_bd.astype(BF16)
    slope = slope_ref[...]

    lane_n = lax.broadcasted_iota(jnp.int32, (D_MODEL, n_blk), 1)
    means = jnp.zeros((D_MODEL, n_blk), F32)
    for n in range(n_blk):
        tot = jnp.sum(sum(k_pages[n * per_blk + i][...] for i in range(per_blk)), axis=1, keepdims=True)
        means = jnp.where(lane_n == n, tot / float(MOBA_BLOCK), means)
    gate = jnp.dot(q_bd16, means.astype(BF16), preferred_element_type=F32)
    sel = _topk_mask(gate, n_blk, 1)

    lane = lax.broadcasted_iota(jnp.int32, (1, PAGE_SIZE), 1)
    scores = []
    for pg in range(n_pages):
        s = jnp.dot(q_bd16, k_pages[pg][...].astype(BF16), preferred_element_type=F32)
        dist = (pos0 - pg * PAGE_SIZE - lane).astype(F32)
        s = s - slope * dist
        n = pg // per_blk
        scores.append(jnp.where(sel[:, n:n + 1] > 0.0, s, -jnp.inf))
    k_new = kn_ref[0].astype(BF16).astype(F32)
    s_new = jnp.sum(q_bd16.astype(F32) * k_new, axis=1, keepdims=True)
    m = s_new
    for s in scores:
        m = jnp.maximum(m, jnp.max(s, axis=1, keepdims=True))
    p_new = jnp.exp(s_new - m)
    l = p_new
    d_e = lax.broadcasted_iota(jnp.int32, (D_MODEL, MOBA_HEADS), 0)
    h_e = lax.broadcasted_iota(jnp.int32, (D_MODEL, MOBA_HEADS), 1)
    expand = jnp.where(d_e // MOBA_DH == h_e, 1.0, 0.0).astype(BF16)
    acc = jnp.zeros((D_MODEL, PAGE_SIZE), F32)
    for pg in range(n_pages):
        p = jnp.exp(scores[pg] - m)
        l = l + jnp.sum(p, axis=1, keepdims=True)
        acc = acc + jnp.dot(expand, p.astype(BF16), preferred_element_type=F32) * v_pages[pg][...]
    o_past = jnp.sum(acc.T, axis=0, keepdims=True)
    row_of = lambda col: jnp.sum(jnp.where(own_head, col, 0.0), axis=0, keepdims=True)
    o_new = row_of(p_new.astype(BF16).astype(F32)) * vn_ref[0]
    o_ref[0] = (o_past + o_new) / row_of(l)


def _moba_sample(q, k_new, v_new, cache_k, cache_v, page_table, slopes, pos0):
    m = q.shape[0]
    n_pages = page_table.shape[1]
    assert pos0 == n_pages * PAGE_SIZE and pos0 % MOBA_BLOCK == 0
    n_phys = cache_k.shape[0]
    ck = jnp.transpose(cache_k, (0, 2, 3, 1)).reshape(n_phys, D_MODEL, PAGE_SIZE)
    cv = jnp.transpose(cache_v, (0, 2, 3, 1)).reshape(n_phys, D_MODEL, PAGE_SIZE)
    vec = pl.BlockSpec((1, 1, D_MODEL), lambda b, pt: (b, 0, 0))
    page_specs = [pl.BlockSpec((None, D_MODEL, PAGE_SIZE), lambda b, pt, pg=pg: (pt[b, pg], 0, 0))
                  for pg in range(n_pages)]
    grid_spec = pltpu.PrefetchScalarGridSpec(
        num_scalar_prefetch=1,
        grid=(m,),
        in_specs=[vec, vec, vec, pl.BlockSpec((MOBA_HEADS, 1), lambda b, pt: (0, 0))] + page_specs + page_specs,
        out_specs=vec,
    )
    out = pl.pallas_call(
        functools.partial(_moba_sample_kernel, n_pages, pos0),
        grid_spec=grid_spec,
        out_shape=jax.ShapeDtypeStruct((m, 1, D_MODEL), F32),
        compiler_params=_params(1),
        name="moba_sample",
    )(page_table, q.reshape(m, 1, D_MODEL), k_new.reshape(m, 1, D_MODEL), v_new.reshape(m, 1, D_MODEL),
      slopes.reshape(MOBA_HEADS, 1), *([ck] * n_pages), *([cv] * n_pages))
    return out.reshape(m, D_MODEL)


def _layer_tail(layer, xp, xs, mix_p, mix_s, p_p, p_s, w_out, ln_g, ln_b, w_router, b_router,
                w_gu, b_gu, w_down, b_down, w_pg, w_pp):
    n_p, n_s = xp.shape[0], xs.shape[0]
    lg0, lb0 = ln_g[0:1], ln_b[0:1]
    lg1, lb1 = ln_g[1:2], ln_b[1:2]
    br = b_router.reshape(1, N_EXPERTS)
    x1p, idxp, gatep = _mix(xp, mix_p, w_out, lg0, lb0, w_router, br, ROW_TILE)
    x1s, idxs, gates = _mix(xs, mix_s, w_out, lg0, lb0, w_router, br, n_s)

    idx = jnp.concatenate([idxp, idxs], axis=0)
    rank, counts = _rank(idx)
    counts = counts[0]
    bm = MOE_BLOCK
    n_tok = n_p + n_s
    n_blocks = (n_tok * TOP_K + N_EXPERTS * (bm - 1)) // bm
    padded = (counts + bm - 1) // bm * bm
    pend = jnp.cumsum(padded)
    pstart = pend - padded
    dest = pstart[idx] + rank
    n_used = (pend[-1] // bm).astype(jnp.int32)
    blk_ids = jnp.minimum(jnp.arange(n_blocks, dtype=jnp.int32), n_used - 1)
    block_e = jnp.sum((pend[None, :] <= (blk_ids * bm)[:, None]).astype(jnp.int32), axis=1)
    block_e = jnp.minimum(block_e, N_EXPERTS - 1)
    order = jnp.argsort(idx.reshape(-1), stable=True).astype(jnp.int32)
    start = jnp.cumsum(counts) - counts
    r_in_group = (blk_ids * bm - pstart[block_e])[:, None] + jnp.arange(bm, dtype=jnp.int32)[None, :]
    live = r_in_group < counts[block_e][:, None]
    src = jnp.where(live, start[block_e][:, None] + r_in_group, 0)
    row_tok = jnp.where(live, order.at[src].get(mode="promise_in_bounds") // TOP_K, 0).reshape(-1)
    x1 = jnp.concatenate([x1p, x1s], axis=0)
    gather = lambda src, rows: src.at[rows].get(mode="promise_in_bounds")
    xs_rows = gather(x1, row_tok)
    ys = _experts(layer, block_e, n_used.reshape(1), xs_rows, w_gu, b_gu, w_down, b_down)
    ysg_p = gather(ys, dest[:n_p].T.reshape(-1))
    ysg_s = gather(ys, dest[n_p:].T.reshape(-1))
    yp = _finish(x1p, ysg_p, gatep, p_p, lg1, lb1, w_pg, w_pp, ROW_TILE)
    ys_out = _finish(x1s, ysg_s, gates, p_s, lg1, lb1, w_pg, w_pp, n_s)
    return yp, ys_out


def kernel(x_prompt, x_sample, state_gla, state_pool, cache_k, cache_v, page_table, p_prompt, p_sample, w_in_ab, gla_w_alpha, gla_b_alpha, gla_norm_g, pool_w, pool_scale, w_out_ab, w_qkv_c, w_out_c, ln_g, ln_b, moe_w_router, moe_b_router, moe_w_gu, moe_b_gu, moe_w_down, moe_b_down, ple_w_gate, ple_w_proj):
    batch, seq, _ = x_prompt.shape
    n_s = x_sample.shape[0]
    n_p = batch * seq
    pos0 = page_table.shape[1] * PAGE_SIZE
    xp = x_prompt.reshape(n_p, D_MODEL)
    xs = x_sample.reshape(n_s, D_MODEL)

    def tail(i, xp, xs, mix_p, mix_s, w_out):
        return _layer_tail(i, xp, xs, mix_p, mix_s, p_prompt[i].reshape(n_p, -1), p_sample[i].reshape(n_s, -1),
                           w_out, ln_g[i], ln_b[i], moe_w_router[i], moe_b_router[i], moe_w_gu,
                           moe_b_gu, moe_w_down, moe_b_down, ple_w_gate[i], ple_w_proj[i])

    w_in = w_in_ab[0]
    c_a = 2 * QK_A + 2 * V_A
    w_main = jnp.concatenate([w_in[:, :c_a], w_in[:, c_a + GLA_LOWRANK:]], axis=1)
    w_a = w_in[:, c_a:c_a + GLA_LOWRANK]
    b_alpha = gla_b_alpha[0].reshape(1, QK_A)
    norm_g = gla_norm_g[0].reshape(1, GLA_DV)
    scale = pool_scale[0].reshape(1, POOL_CH)
    qkl_p, v_p, sg_p, u_p = _proj_ab(xp, w_main, w_a, gla_w_alpha[0], b_alpha, ROW_TILE)
    qkl_s, v_s, sg_s, u_s = _proj_ab(xs, w_main, w_a, gla_w_alpha[0], b_alpha, n_s)
    mix_p, gla_p = _gla_prompt(qkl_p, v_p, sg_p, norm_g, batch, seq, ROW_TILE)
    mix_p = _pool_prompt(u_p, mix_p, pool_w[0], scale, batch, seq, ROW_TILE)
    mix_s, gla_s = _gla_sample(qkl_s, v_s, sg_s, norm_g, state_gla[0].reshape(n_s, QK_A, GLA_DV))
    mix_s, pool_s = _pool_sample(u_s, state_pool[0], mix_s, pool_w[0], scale, pos0)
    pool_p = u_p.reshape(batch, seq, POOL_CH)[:, seq - POOL_HIST:]
    xp, xs = tail(0, xp, xs, mix_p, mix_s, w_out_ab[0])

    slopes = jnp.exp2(-8.0 * jnp.arange(1, MOBA_HEADS + 1, dtype=F32) / MOBA_HEADS)
    q_p, k_p, v_p2 = _qkv(xp, w_qkv_c[0], ROW_TILE)
    q_s, k_s, v_s2 = _qkv(xs, w_qkv_c[0], n_s)
    o_p = _moba_prompt(q_p, k_p, v_p2, slopes, batch, seq)
    o_s = _moba_sample(q_s, k_s, v_s2, cache_k[0], cache_v[0], page_table, slopes, pos0)
    xp, xs = tail(1, xp, xs, o_p, o_s, w_out_c[0])

    hd = (MOBA_HEADS, MOBA_DH)
    return (xp.reshape(batch, seq, D_MODEL), xs.reshape(n_s, 1, D_MODEL),
            gla_p.reshape(1, batch, GLA_HEADS, GLA_DK, GLA_DV), gla_s.reshape(1, n_s, GLA_HEADS, GLA_DK, GLA_DV),
            pool_p[None], pool_s[None],
            k_p.reshape(1, batch, seq, *hd), v_p2.reshape(1, batch, seq, *hd),
            k_s.reshape(1, n_s, 1, *hd), v_s2.reshape(1, n_s, 1, *hd))
```

```python
import functools
import math

import numpy as np
import jax
import jax.numpy as jnp
from jax import lax
from jax.experimental import pallas as pl
from jax.experimental.pallas import tpu as pltpu

F32 = jnp.float32
BF16 = jnp.bfloat16
HI = lax.Precision.HIGHEST

D_MODEL = 1024
GLA_HEADS = 4
GLA_DK = 64
GLA_DV = 128
GLA_LOWRANK = 16
GLA_TAU = 16.0
QK_A = GLA_HEADS * GLA_DK
V_A = GLA_HEADS * GLA_DV
POOL_WINDOWS = (2, 4, 8, 16)
POOL_CH = 512
POOL_GROUP = 128
POOL_HIST = 15
MOBA_HEADS = 16
MOBA_DH = 64
MOBA_BLOCK = 256
MOBA_TOPK = 3
PAGE_SIZE = 128
N_EXPERTS = 32
TOP_K = 4
D_EXPERT = 1024
SWIGLU_LIMIT = 7.0
SWIGLU_ALPHA = 1.702
DEPTH = 2
DEEPNORM_ALPHA = (2 * DEPTH) ** 0.25
LN_EPS = 1e-5

ROW_TILE = 512
SUB_TILE = 256
GLA_CHUNK = 128
GLA_LEVELS = 7
MOE_BLOCK = 512
RANK_TILE = 384
VMEM_LIMIT = 56 * 1024 * 1024


def _params(n_axes, vmem=VMEM_LIMIT):
    return pltpu.CompilerParams(dimension_semantics=("arbitrary",) * n_axes, vmem_limit_bytes=vmem)


def _bdot(a, b):
    return jnp.dot(a.astype(BF16), b.astype(BF16), preferred_element_type=F32)


def _layer_norm(y, g, b):
    mu = jnp.mean(y, axis=-1, keepdims=True)
    yc = y - mu
    var = jnp.mean(yc * yc, axis=-1, keepdims=True)
    return yc * lax.rsqrt(var + LN_EPS) * g + b


def _full(shape):
    n = len(shape)
    return pl.BlockSpec(shape, lambda *_: (0,) * n)


def _proj_ab_kernel(x_ref, wm_ref, wa_ref, walpha_ref, balpha_ref,
                    qkl_ref, v_ref, sg_ref, u_ref, wm_bf, wa_bf):
    @pl.when(pl.program_id(0) == 0)
    def _():
        wm_bf[...] = wm_ref[...].astype(BF16)
        wa_bf[...] = wa_ref[...].astype(BF16)

    xb = x_ref[...].astype(BF16)
    qk = jnp.dot(xb, wm_bf[:, 0:2 * QK_A], preferred_element_type=F32)
    qkl_ref[:, 0:QK_A] = qk[:, 0:QK_A] * (GLA_DK ** -0.5)
    qkl_ref[:, QK_A:2 * QK_A] = qk[:, QK_A:2 * QK_A]
    a_lr = jnp.dot(xb, wa_bf[...], preferred_element_type=F32)
    z = _bdot(a_lr, walpha_ref[...]) + balpha_ref[...]
    log_sig = jnp.minimum(z, 0.0) - jnp.log1p(jnp.exp(-jnp.abs(z)))
    qkl_ref[:, 2 * QK_A:3 * QK_A] = log_sig / GLA_TAU
    c0 = 2 * QK_A
    v_ref[...] = jnp.dot(xb, wm_bf[:, c0:c0 + V_A], preferred_element_type=F32)
    g = jnp.dot(xb, wm_bf[:, c0 + V_A:c0 + 2 * V_A], preferred_element_type=F32)
    sg_ref[...] = g * jax.nn.sigmoid(g)
    u_ref[...] = jnp.dot(xb, wm_bf[:, c0 + 2 * V_A:c0 + 2 * V_A + POOL_CH], preferred_element_type=F32)


def _proj_ab(x, w_main, w_a, w_alpha, b_alpha, tm):
    m = x.shape[0]
    nmain = w_main.shape[1]
    row = lambda w: pl.BlockSpec((tm, w), lambda i: (i, 0))
    return pl.pallas_call(
        _proj_ab_kernel,
        grid=(m // tm,),
        in_specs=[row(D_MODEL), _full((D_MODEL, nmain)), _full((D_MODEL, GLA_LOWRANK)),
                  _full((GLA_LOWRANK, QK_A)), _full((1, QK_A))],
        out_specs=[row(3 * QK_A), row(V_A), row(V_A), row(POOL_CH)],
        out_shape=[jax.ShapeDtypeStruct((m, 3 * QK_A), F32), jax.ShapeDtypeStruct((m, V_A), F32),
                   jax.ShapeDtypeStruct((m, V_A), F32), jax.ShapeDtypeStruct((m, POOL_CH), F32)],
        scratch_shapes=[pltpu.VMEM((D_MODEL, nmain), BF16), pltpu.VMEM((D_MODEL, GLA_LOWRANK), BF16)],
        compiler_params=_params(1),
        name="proj_ab",
    )(x, w_main, w_a, w_alpha, b_alpha)


def _gla_tables():
    c = GLA_CHUNK
    i = np.arange(c)[:, None]
    s = np.arange(c)[None, :]
    mats = [(s <= i), (s > i)]
    for lev in range(GLA_LEVELS):
        p = GLA_LEVELS - 1 - lev
        half = 1 << p
        start = (i >> (p + 1)) << (p + 1)
        mid = start + half - 1
        upper = i >= start + half
        mats.append(np.where(upper, (s > mid) & (s <= i), (s > i) & (s <= mid)))
    seg = np.concatenate(mats, axis=0).astype(np.float32)
    j = np.arange(c)[None, :]
    x = i ^ j
    lvl = np.full((c, c), GLA_LEVELS + 1, np.int32)
    lvl[np.arange(c), np.arange(c)] = GLA_LEVELS
    for lev in range(GLA_LEVELS):
        p = GLA_LEVELS - 1 - lev
        lvl = np.where(((x >> p) == 1) & (((i >> p) & 1) == 1), lev, lvl)
    lvl4 = np.tile(lvl, (GLA_HEADS, 1)).astype(np.int32)
    lane_head = (np.arange(QK_A) // GLA_DK)[None, :]
    row_head = (np.arange(GLA_HEADS * c) // c)[:, None]
    hm4 = (lane_head == row_head).astype(np.float32)
    return seg, lvl4, hm4


def _gla_chunk(q, k, la, v, s_all, seg, lvl4, hm4):
    c = GLA_CHUNK
    e = sum(jnp.dot(seg, part.astype(BF16), preferred_element_type=F32) for part in _split3(la))
    w = jnp.exp(e)
    w_b = w[0:c]
    w_k = w[c:2 * c]

    def stack_heads(t):
        return (jnp.concatenate([t] * GLA_HEADS, axis=0) * hm4).astype(BF16)

    o_inter = jnp.dot(stack_heads(q * w_b), s_all.astype(BF16), preferred_element_type=F32)
    a = jnp.zeros((GLA_HEADS * c, c), F32)
    for lev in range(GLA_LEVELS + 1):
        if lev < GLA_LEVELS:
            w_l = w[(2 + lev) * c:(3 + lev) * c]
            ql, kl = q * w_l, k * w_l
        else:
            ql, kl = q, k
        p_l = lax.dot_general(stack_heads(ql), kl.astype(BF16), (((1,), (1,)), ((), ())),
                              preferred_element_type=F32)
        a = jnp.where(lvl4 == lev, p_l, a)
    a = a.astype(BF16)
    outs = []
    for h in range(GLA_HEADS):
        v_h = v[:, h * GLA_DV:(h + 1) * GLA_DV].astype(BF16)
        o_h = o_inter[h * c:(h + 1) * c] + jnp.dot(a[h * c:(h + 1) * c], v_h, preferred_element_type=F32)
        outs.append(o_h)
    ks_t = (k * w_k).T.astype(BF16)
    kv = jnp.dot(ks_t, v.astype(BF16), preferred_element_type=F32)
    dec = jnp.exp(jnp.sum(la.T, axis=1, keepdims=True))
    new_rows = []
    for h in range(GLA_HEADS):
        rows = slice(h * GLA_DK, (h + 1) * GLA_DK)
        new_rows.append(dec[rows] * s_all[rows] + kv[rows, h * GLA_DV:(h + 1) * GLA_DV])
    return outs, jnp.concatenate(new_rows, axis=0)


def _gla_finish(o_h, norm_g, sg_h):
    o_h = o_h * lax.rsqrt(jnp.mean(o_h * o_h, axis=-1, keepdims=True) + LN_EPS) * norm_g
    return o_h * sg_h


def _gla_prompt_kernel(qkl_ref, v_ref, sg_ref, ng_ref, seg_ref, lvl_ref, hm_ref,
                       mix_ref, state_ref, s_scr):
    t = pl.program_id(1)

    @pl.when(t == 0)
    def _():
        s_scr[...] = jnp.zeros_like(s_scr)

    seg = seg_ref[...]
    lvl4 = lvl_ref[...]
    hm4 = hm_ref[...]
    norm_g = ng_ref[...]
    n_chunks = qkl_ref.shape[0] // GLA_CHUNK

    s_all = s_scr[...]
    for ci in range(n_chunks):
        rows = pl.ds(ci * GLA_CHUNK, GLA_CHUNK)
        q = qkl_ref[rows, 0:QK_A]
        k = qkl_ref[rows, QK_A:2 * QK_A]
        la = qkl_ref[rows, 2 * QK_A:3 * QK_A]
        v = v_ref[rows, :]
        outs, s_all = _gla_chunk(q, k, la, v, s_all, seg, lvl4, hm4)
        if ci == n_chunks - 1:
            s_scr[...] = s_all
        for h in range(GLA_HEADS):
            cols = slice(h * GLA_DV, (h + 1) * GLA_DV)
            mix_ref[rows, cols] = _gla_finish(outs[h], norm_g, sg_ref[rows, cols])

    @pl.when(t == pl.num_programs(1) - 1)
    def _():
        state_ref[0] = s_scr[...]


def _gla_prompt(qkl, v, sg, norm_g, batch, seq, tm):
    m = qkl.shape[0]
    nt = seq // tm
    seg, lvl4, hm4 = _gla_tables()
    row = lambda w: pl.BlockSpec((tm, w), lambda b, t: (b * nt + t, 0))
    return pl.pallas_call(
        _gla_prompt_kernel,
        grid=(batch, nt),
        in_specs=[row(3 * QK_A), row(V_A), row(V_A), _full((1, GLA_DV)),
                  _full(seg.shape), _full(lvl4.shape), _full(hm4.shape)],
        out_specs=[pl.BlockSpec((tm, V_A), lambda b, t: (b * nt + t, 0)),
                   pl.BlockSpec((1, QK_A, GLA_DV), lambda b, t: (b, 0, 0))],
        out_shape=[jax.ShapeDtypeStruct((m, D_MODEL), F32),
                   jax.ShapeDtypeStruct((batch, QK_A, GLA_DV), F32)],
        scratch_shapes=[pltpu.VMEM((QK_A, GLA_DV), F32)],
        compiler_params=_params(2),
        name="gla_prompt",
    )(qkl, v, sg, norm_g, jnp.asarray(seg, BF16), jnp.asarray(lvl4), jnp.asarray(hm4))


GLA_SAMPLE_ROWS = 8


def _gla_sample_kernel(qkl_ref, v_ref, sg_ref, ng_ref, s0_ref, mix_ref, s1_ref):
    ones = jnp.ones((QK_A, GLA_DV), F32)
    r_i = lax.broadcasted_iota(jnp.int32, (QK_A, QK_A), 0)
    c_i = lax.broadcasted_iota(jnp.int32, (QK_A, QK_A), 1)
    eye = r_i == c_i
    norm_g = ng_ref[...]

    def col_bcast(row):
        diag = jnp.where(eye, jnp.broadcast_to(row, (QK_A, QK_A)), 0.0)
        return jnp.dot(diag, ones, precision=HI, preferred_element_type=F32)

    for r in range(GLA_SAMPLE_ROWS):
        q = col_bcast(qkl_ref[r:r + 1, 0:QK_A])
        k = col_bcast(qkl_ref[r:r + 1, QK_A:2 * QK_A])
        dec = jnp.exp(col_bcast(qkl_ref[r:r + 1, 2 * QK_A:3 * QK_A]))
        for h in range(GLA_HEADS):
            rows = slice(h * GLA_DK, (h + 1) * GLA_DK)
            cols = slice(h * GLA_DV, (h + 1) * GLA_DV)
            s_new = dec[rows] * s0_ref[r, rows, :] + k[rows] * v_ref[r:r + 1, cols]
            s1_ref[r, rows, :] = s_new
            o_h = jnp.sum(q[rows] * s_new, axis=0, keepdims=True)
            mix_ref[r:r + 1, cols] = _gla_finish(o_h, norm_g, sg_ref[r:r + 1, cols])


def _gla_sample(qkl, v, sg, norm_g, s0):
    m = qkl.shape[0]
    rb = GLA_SAMPLE_ROWS
    row = lambda w: pl.BlockSpec((rb, w), lambda i: (i, 0))
    st = pl.BlockSpec((rb, QK_A, GLA_DV), lambda i: (i, 0, 0))
    return pl.pallas_call(
        _gla_sample_kernel,
        grid=(m // rb,),
        in_specs=[row(3 * QK_A), row(V_A), row(V_A), _full((1, GLA_DV)), st],
        out_specs=[pl.BlockSpec((rb, V_A), lambda i: (i, 0)), st],
        out_shape=[jax.ShapeDtypeStruct((m, D_MODEL), F32),
                   jax.ShapeDtypeStruct((m, QK_A, GLA_DV), F32)],
        compiler_params=_params(1),
        name="gla_sample",
    )(qkl, v, sg, norm_g, s0)


def _pool_project(d_groups, wp_ref, scale_ref, mix_ref):
    for g in range(len(POOL_WINDOWS)):
        cols = slice(g * POOL_GROUP, (g + 1) * POOL_GROUP)
        mix_ref[:, cols] = _bdot(d_groups[g], wp_ref[g]) * scale_ref[:, cols]


def _pool_prompt_kernel(u_ref, prev_ref, wp_ref, scale_ref, mixin_ref, mix_ref):
    del mixin_ref
    t = pl.program_id(1)
    tm = u_ref.shape[0]
    hist = prev_ref.shape[0]
    prev = jnp.where(t > 0, prev_ref[...], 0.0)
    z = jnp.concatenate([prev, u_ref[...]], axis=0)
    pos = t * tm + lax.broadcasted_iota(jnp.int32, (tm, 1), 0)
    d_groups = []
    for g, w in enumerate(POOL_WINDOWS):
        cols = slice(g * POOL_GROUP, (g + 1) * POOL_GROUP)
        s = z[:, cols]
        shift = 1
        while shift < w:
            s = s + pltpu.roll(s, shift, 0)
            shift *= 2
        cnt = jnp.minimum(w, pos + 1).astype(F32)
        d_groups.append(s[hist:] / cnt - z[hist:, cols])
    _pool_project(d_groups, wp_ref, scale_ref, mix_ref)


def _pool_prompt(u, mix, w_pool, scale, batch, seq, tm):
    nt = seq // tm
    hist = 16
    assert hist > POOL_HIST and tm % hist == 0
    per = tm // hist
    return pl.pallas_call(
        _pool_prompt_kernel,
        grid=(batch, nt),
        in_specs=[pl.BlockSpec((tm, POOL_CH), lambda b, t: (b * nt + t, 0)),
                  pl.BlockSpec((hist, POOL_CH), lambda b, t: (jnp.maximum((b * nt + t) * per - 1, 0), 0)),
                  _full(w_pool.shape), _full((1, POOL_CH)),
                  pl.BlockSpec(memory_space=pl.ANY)],
        out_specs=pl.BlockSpec((tm, POOL_CH), lambda b, t: (b * nt + t, 1)),
        out_shape=jax.ShapeDtypeStruct(mix.shape, F32),
        input_output_aliases={4: 0},
        compiler_params=_params(2),
        name="pool_prompt",
    )(u, u, w_pool, scale, mix)


def _pool_sample_kernel(u_ref, st_ref, wp_ref, scale_ref, mixin_ref, mix_ref, st_out_ref):
    del mixin_ref
    u = u_ref[...]
    d_groups = []
    for g, w in enumerate(POOL_WINDOWS):
        cols = slice(g * POOL_GROUP, (g + 1) * POOL_GROUP)
        win = u[:, cols] + jnp.sum(st_ref[:, POOL_HIST - (w - 1):POOL_HIST, cols], axis=1)
        d_groups.append(win / float(w) - u[:, cols])
    _pool_project(d_groups, wp_ref, scale_ref, mix_ref)
    st_out_ref[:, 0:POOL_HIST - 1, :] = st_ref[:, 1:POOL_HIST, :]
    st_out_ref[:, POOL_HIST - 1:POOL_HIST, :] = u[:, None, :]


def _pool_sample(u, st, mix, w_pool, scale, pos0):
    m = u.shape[0]
    assert pos0 + 1 >= max(POOL_WINDOWS)
    return pl.pallas_call(
        _pool_sample_kernel,
        grid=(1,),
        in_specs=[_full((m, POOL_CH)), _full(st.shape), _full(w_pool.shape), _full((1, POOL_CH)),
                  pl.BlockSpec(memory_space=pl.ANY)],
        out_specs=[pl.BlockSpec((m, POOL_CH), lambda i: (0, 1)), _full(st.shape)],
        out_shape=[jax.ShapeDtypeStruct(mix.shape, F32), jax.ShapeDtypeStruct(st.shape, F32)],
        input_output_aliases={4: 0},
        compiler_params=_params(1),
        name="pool_sample",
    )(u, st, w_pool, scale, mix)


def _sub_tiles(tm):
    ts = SUB_TILE if tm % SUB_TILE == 0 else tm
    return [slice(s, s + ts) for s in range(0, tm, ts)]


def _mix_kernel(x_ref, mix_ref, w_ref, lng_ref, lnb_ref, wr_ref, br_ref, *rest):
    x1_ref, idx_ref, gate_ref, w_bf = rest[-4:]

    @pl.when(pl.program_id(0) == 0)
    def _():
        w_bf[...] = w_ref[...].astype(BF16)

    for rows in _sub_tiles(x_ref.shape[0]):
        h = jnp.dot(mix_ref[rows, :].astype(BF16), w_bf[...], preferred_element_type=F32)
        x1 = _layer_norm(DEEPNORM_ALPHA * x_ref[rows, :] + h, lng_ref[...], lnb_ref[...])
        x1_ref[rows, :] = x1
        logits = _bdot(x1, wr_ref[...]) + br_ref[...]
        ts = logits.shape[0]
        lane = lax.broadcasted_iota(jnp.int32, (ts, N_EXPERTS), 1)
        lane_k = lax.broadcasted_iota(jnp.int32, (ts, TOP_K), 1)
        idx_out = jnp.zeros((ts, TOP_K), jnp.int32)
        val_out = jnp.zeros((ts, TOP_K), F32)
        cur = logits
        for kk in range(TOP_K):
            mval = jnp.max(cur, axis=1, keepdims=True)
            midx = jnp.min(jnp.where(cur == mval, lane.astype(F32), float(N_EXPERTS)), axis=1,
                           keepdims=True).astype(jnp.int32)
            idx_out = jnp.where(lane_k == kk, midx, idx_out)
            val_out = jnp.where(lane_k == kk, mval, val_out)
            cur = jnp.where(lane == midx, -jnp.inf, cur)
        ex = jnp.exp(val_out - val_out[:, 0:1])
        idx_ref[rows, :] = idx_out
        gate_ref[rows, :] = ex / jnp.sum(ex, axis=1, keepdims=True)


def _mix(x, mix, w_out, ln_g, ln_b, w_router, b_router, tm, x1_buf=None, n_total=None):
    m = x.shape[0]
    n_total = n_total if x1_buf is None else x1_buf.shape[0]
    off_blocks = 0 if x1_buf is None else (n_total - m) // tm
    row = lambda w: pl.BlockSpec((tm, w), lambda i: (i, 0))
    in_specs = [row(D_MODEL), row(D_MODEL), _full((D_MODEL, D_MODEL)), _full((1, D_MODEL)),
                _full((1, D_MODEL)), _full((D_MODEL, N_EXPERTS)), _full((1, N_EXPERTS))]
    args = [x, mix, w_out, ln_g, ln_b, w_router, b_router]
    aliases = {}
    if x1_buf is not None:
        assert (n_total - m) % tm == 0
        in_specs.append(pl.BlockSpec(memory_space=pl.ANY))
        args.append(x1_buf)
        aliases = {len(args) - 1: 0}
    return pl.pallas_call(
        _mix_kernel,
        grid=(m // tm,),
        in_specs=in_specs,
        out_specs=[pl.BlockSpec((tm, D_MODEL), lambda i: (i + off_blocks, 0)), row(TOP_K), row(TOP_K)],
        out_shape=[jax.ShapeDtypeStruct((n_total, D_MODEL), F32),
                   jax.ShapeDtypeStruct((m, TOP_K), jnp.int32), jax.ShapeDtypeStruct((m, TOP_K), F32)],
        scratch_shapes=[pltpu.VMEM((D_MODEL, D_MODEL), BF16)],
        input_output_aliases=aliases,
        compiler_params=_params(1),
        name="mix",
    )(*args)


def _rank_kernel(idx_ref, rank_ref, cnt_ref, carry):
    i = pl.program_id(0)

    @pl.when(i == 0)
    def _():
        carry[...] = jnp.zeros_like(carry)

    idx = idx_ref[...]
    tm = idx.shape[0]
    lane = lax.broadcasted_iota(jnp.int32, (tm, N_EXPERTS), 1)
    onehots = [(idx[:, kk:kk + 1] == lane) for kk in range(TOP_K)]
    member = sum(jnp.where(o, 1.0, 0.0) for o in onehots)
    r_i = lax.broadcasted_iota(jnp.int32, (tm, tm), 0)
    c_i = lax.broadcasted_iota(jnp.int32, (tm, tm), 1)
    strict_lower = jnp.where(c_i < r_i, 1.0, 0.0).astype(BF16)
    before = jnp.dot(strict_lower, member.astype(BF16), preferred_element_type=F32) + carry[...]
    lane_k = lax.broadcasted_iota(jnp.int32, (tm, TOP_K), 1)
    rank = jnp.zeros((tm, TOP_K), F32)
    for kk in range(TOP_K):
        r_k = jnp.sum(jnp.where(onehots[kk], before, 0.0), axis=1, keepdims=True)
        rank = jnp.where(lane_k == kk, r_k, rank)
    rank_ref[...] = rank.astype(jnp.int32)
    carry[...] = carry[...] + jnp.sum(member, axis=0, keepdims=True)
    cnt_ref[...] = carry[...].astype(jnp.int32)


def _rank(idx):
    n = idx.shape[0]
    tm = RANK_TILE
    return pl.pallas_call(
        _rank_kernel,
        grid=(n // tm,),
        in_specs=[pl.BlockSpec((tm, TOP_K), lambda i: (i, 0))],
        out_specs=[pl.BlockSpec((tm, TOP_K), lambda i: (i, 0)), _full((1, N_EXPERTS))],
        out_shape=[jax.ShapeDtypeStruct((n, TOP_K), jnp.int32),
                   jax.ShapeDtypeStruct((1, N_EXPERTS), jnp.int32)],
        scratch_shapes=[pltpu.VMEM((1, N_EXPERTS), F32)],
        compiler_params=_params(1),
        name="moe_rank",
    )(idx)


def _expert_kernel(be_ref, nu_ref, xs_ref, wgu_ref, bgu_ref, wd_ref, bd_ref, ys_ref, wgu_bf, wd_bf):
    blk = pl.program_id(0)
    prev = be_ref[jnp.maximum(blk - 1, 0)]
    fresh = jnp.logical_or(blk == 0, be_ref[blk] != prev)
    used = blk < nu_ref[0]

    @pl.when(jnp.logical_and(fresh, used))
    def _():
        wgu_bf[...] = wgu_ref[0, 0].astype(BF16)
        wd_bf[...] = wd_ref[0, 0].astype(BF16)

    @pl.when(used)
    def _():
        xb = xs_ref[...].astype(BF16)
        acc = jnp.zeros(ys_ref.shape, F32) + bd_ref[0, 0]
        half = D_EXPERT // 2
        for c in range(2):
            cg = slice(c * half, (c + 1) * half)
            cu = slice(D_EXPERT + c * half, D_EXPERT + (c + 1) * half)
            gate = jnp.dot(xb, wgu_bf[:, cg], preferred_element_type=F32) + bgu_ref[0, 0, :, cg]
            up = jnp.dot(xb, wgu_bf[:, cu], preferred_element_type=F32) + bgu_ref[0, 0, :, cu]
            gate = jnp.minimum(gate, SWIGLU_LIMIT)
            up = jnp.clip(up, -SWIGLU_LIMIT, SWIGLU_LIMIT)
            act = (up + 1.0) * gate * jax.nn.sigmoid(SWIGLU_ALPHA * gate)
            acc = acc + jnp.dot(act.astype(BF16), wd_bf[cg, :], preferred_element_type=F32)
        ys_ref[...] = acc

    @pl.when(jnp.logical_not(used))
    def _():
        ys_ref[...] = jnp.zeros_like(ys_ref)


def _experts(layer, block_e, n_used, xs, w_gu, b_gu, w_down, b_down):
    p = xs.shape[0]
    bm = MOE_BLOCK
    nb = p // bm
    depth = w_gu.shape[0]
    last = lambda blk, nu: jnp.minimum(blk, nu[0] - 1)
    grid_spec = pltpu.PrefetchScalarGridSpec(
        num_scalar_prefetch=2,
        grid=(nb,),
        in_specs=[pl.BlockSpec((bm, D_MODEL), lambda blk, be, nu: (last(blk, nu), 0)),
                  pl.BlockSpec((1, 1, D_MODEL, 2 * D_EXPERT), lambda blk, be, nu: (layer, be[blk], 0, 0)),
                  pl.BlockSpec((1, 1, 1, 2 * D_EXPERT), lambda blk, be, nu: (layer, be[blk], 0, 0)),
                  pl.BlockSpec((1, 1, D_EXPERT, D_MODEL), lambda blk, be, nu: (layer, be[blk], 0, 0)),
                  pl.BlockSpec((1, 1, 1, D_MODEL), lambda blk, be, nu: (layer, be[blk], 0, 0))],
        out_specs=pl.BlockSpec((bm, D_MODEL), lambda blk, be, nu: (blk, 0)),
        scratch_shapes=[pltpu.VMEM((D_MODEL, 2 * D_EXPERT), BF16), pltpu.VMEM((D_EXPERT, D_MODEL), BF16)],
    )
    return pl.pallas_call(
        _expert_kernel,
        grid_spec=grid_spec,
        out_shape=jax.ShapeDtypeStruct((p, D_MODEL), F32),
        compiler_params=_params(1),
        name="moe_experts",
    )(block_e, n_used, xs, w_gu, b_gu.reshape(depth, N_EXPERTS, 1, 2 * D_EXPERT), w_down,
      b_down.reshape(depth, N_EXPERTS, 1, D_MODEL))


def _finish_kernel(x1_ref, y0_ref, y1_ref, y2_ref, y3_ref, gate_ref, p_ref, lng_ref, lnb_ref,
                   wpg_ref, wpp_ref, out_ref, wpg_bf):
    @pl.when(pl.program_id(0) == 0)
    def _():
        wpg_bf[...] = wpg_ref[...].astype(BF16)

    for rows in _sub_tiles(out_ref.shape[0]):
        gates = gate_ref[rows, :]
        moe = y0_ref[rows, :] * gates[:, 0:1]
        for kk, y_ref in enumerate((y1_ref, y2_ref, y3_ref), start=1):
            moe = moe + y_ref[rows, :] * gates[:, kk:kk + 1]
        x2 = _layer_norm(DEEPNORM_ALPHA * x1_ref[rows, :] + moe, lng_ref[...], lnb_ref[...])
        pg = jax.nn.sigmoid(jnp.dot(x2.astype(BF16), wpg_bf[...], preferred_element_type=F32))
        pp = _bdot(p_ref[rows, :], wpp_ref[...])
        out_ref[rows, :] = x2 + pg * pp


def _finish(x1_all, row0, m, ysg, gates, p_all, layer, ln_g, ln_b, w_pg, w_pp, tm):
    nt = m // tm
    assert row0 % tm == 0
    row = lambda w: pl.BlockSpec((tm, w), lambda i: (i, 0))
    ple = p_all.shape[2]
    y_specs = [pl.BlockSpec((tm, D_MODEL), lambda i, kk=kk: (kk * nt + i, 0)) for kk in range(TOP_K)]
    x1_spec = pl.BlockSpec((tm, D_MODEL), lambda i: (i + row0 // tm, 0))
    p_spec = pl.BlockSpec((None, tm, ple), lambda i: (layer, i, 0))
    return pl.pallas_call(
        _finish_kernel,
        grid=(nt,),
        in_specs=[x1_spec] + y_specs + [row(TOP_K), p_spec, _full((1, D_MODEL)), _full((1, D_MODEL)),
                                        _full((D_MODEL, D_MODEL)), _full((ple, D_MODEL))],
        out_specs=row(D_MODEL),
        out_shape=jax.ShapeDtypeStruct((m, D_MODEL), F32),
        scratch_shapes=[pltpu.VMEM((D_MODEL, D_MODEL), BF16)],
        compiler_params=_params(1),
        name="finish",
    )(x1_all, ysg, ysg, ysg, ysg, gates, p_all, ln_g, ln_b, w_pg, w_pp)


def _qkv_kernel(x_ref, w_ref, q_ref, k_ref, v_ref, w_bf):
    @pl.when(pl.program_id(0) == 0)
    def _():
        w_bf[...] = w_ref[...].astype(BF16)

    xb = x_ref[...].astype(BF16)
    q_ref[...] = jnp.dot(xb, w_bf[:, 0:D_MODEL], preferred_element_type=F32) * (MOBA_DH ** -0.5)
    k_ref[...] = jnp.dot(xb, w_bf[:, D_MODEL:2 * D_MODEL], preferred_element_type=F32)
    v_ref[...] = jnp.dot(xb, w_bf[:, 2 * D_MODEL:3 * D_MODEL], preferred_element_type=F32)


def _qkv(x, w_qkv, tm):
    m = x.shape[0]
    row = pl.BlockSpec((tm, D_MODEL), lambda i: (i, 0))
    shp = jax.ShapeDtypeStruct((m, D_MODEL), F32)
    return pl.pallas_call(
        _qkv_kernel,
        grid=(m // tm,),
        in_specs=[row, _full((D_MODEL, 3 * D_MODEL))],
        out_specs=[row, row, row],
        out_shape=[shp, shp, shp],
        scratch_shapes=[pltpu.VMEM((D_MODEL, 3 * D_MODEL), BF16)],
        compiler_params=_params(1),
        name="qkv",
    )(x, w_qkv)


def _topk_mask(gate, n_valid, axis):
    nb = gate.shape[axis]
    pos_i = lax.broadcasted_iota(jnp.int32, gate.shape, axis)
    valid = pos_i < n_valid
    pos = pos_i.astype(F32)
    cur = jnp.where(valid, gate, -jnp.inf)
    sel = jnp.zeros(gate.shape, F32)
    for _ in range(min(MOBA_TOPK, nb)):
        best = jnp.max(cur, axis=axis, keepdims=True)
        first = jnp.min(jnp.where(cur == best, pos, float(nb)), axis=axis, keepdims=True)
        pick = jnp.logical_and(pos == first, valid)
        sel = jnp.where(pick, 1.0, sel)
        cur = jnp.where(pick, -jnp.inf, cur)
    return sel


MOBA_AUG_MASK0 = 8
MOBA_NEG = -30720.0


def _split3(x):
    hi = x.astype(BF16).astype(F32)
    mid = (x - hi).astype(BF16).astype(F32)
    lo = (x - hi - mid).astype(BF16).astype(F32)
    return hi, mid, lo


def _moba_prompt_kernel(q_ref, k_ref, v_ref, slope_ref, o_ref, qt_scr, vtp_scr, ka_scr, km_scr, s_scr):
    hp = pl.program_id(1)
    qt = pl.program_id(2)
    nb = km_scr.shape[0]
    blk = MOBA_BLOCK
    qw = 2 * blk
    dh = MOBA_DH
    pair = 2 * dh

    @pl.when(qt == 0)
    def _():
        lane = lax.broadcasted_iota(jnp.int32, (blk, pair), 1)
        key_off = lax.broadcasted_iota(jnp.int32, (blk, pair), 0).astype(F32)
        for n in range(nb):
            rows = slice(n * blk, (n + 1) * blk)
            half = slice((n % 2) * blk, (n % 2 + 1) * blk)
            qt_scr[n // 2, :, half] = q_ref[rows, :].T.astype(BF16)
            vtp_scr[n // 2, :, half] = v_ref[rows, :].T.astype(BF16)
            kblk = k_ref[rows, :]
            km_scr[n:n + 1, :] = jnp.mean(kblk, axis=0, keepdims=True)
            for hh in range(2):
                a = lane - (1 - hh) * dh
                aug = jnp.where(a < 3, key_off,
                                jnp.where(a < 6, float(n * blk),
                                          jnp.where(a == MOBA_AUG_MASK0 + n, 1.0, 0.0)))
                in_head = jnp.logical_and(lane >= hh * dh, lane < (hh + 1) * dh)
                ka_scr[hh, rows, :] = jnp.where(in_head, kblk, aug).astype(BF16)

    key_i = lax.broadcasted_iota(jnp.int32, (qw, qw), 0)
    qry_i = lax.broadcasted_iota(jnp.int32, (qw, qw), 1)
    same_block = (key_i >= blk) == (qry_i >= blk)
    causal = jnp.logical_and(same_block, key_i <= qry_i)
    first_key_second_qry = jnp.logical_and(key_i < blk, qry_i >= blk)
    col = lax.broadcasted_iota(jnp.int32, (1, qw), 1)
    cur_blk = 2 * qt + jnp.where(col >= blk, 1, 0)
    blk_row = lax.broadcasted_iota(jnp.int32, (nb, qw), 0)

    own = pl.ds(pl.multiple_of(qt * qw, qw), qw)
    q_t = qt_scr[qt].astype(F32)
    v_t_d = vtp_scr[qt]
    lane_k = lax.broadcasted_iota(jnp.int32, (nb, pair), 1)
    r8 = lax.broadcasted_iota(jnp.int32, (8, qw), 0)
    is_hi = jnp.logical_or(r8 == 0, r8 == 3)
    is_mid = jnp.logical_or(r8 == 1, r8 == 4)
    pad_rows = jnp.zeros((dh - MOBA_AUG_MASK0 - nb, qw), F32)

    q_aug, init = [], []
    for hh in range(2):
        hrows = slice(hh * dh, (hh + 1) * dh)
        s_hi, s_mid, s_lo = _split3(slope_ref[pl.ds(2 * hp + hh, 1), :])
        slope_rows = jnp.where(is_hi, s_hi, jnp.where(is_mid, s_mid, s_lo))
        slope_rows = jnp.where(r8 < 6, slope_rows, 0.0)

        def with_aug(aug_rows, hh=hh, hrows=hrows):
            parts = [q_t[hrows], aug_rows] if hh == 0 else [aug_rows, q_t[hrows]]
            return jnp.concatenate(parts, axis=0).astype(BF16)

        q_diag = with_aug(jnp.concatenate([slope_rows, jnp.zeros((dh - 8, qw), F32)], axis=0))
        in_head = jnp.logical_and(lane_k >= hh * dh, lane_k < (hh + 1) * dh)
        km_h = jnp.where(in_head, km_scr[...], 0.0).astype(BF16)
        gate_t = jnp.dot(km_h, q_diag, preferred_element_type=F32)
        sel = _topk_mask(gate_t, cur_blk, 0)
        mask_rows = jnp.where(sel > 0.0, 0.0, MOBA_NEG)
        q_aug.append(with_aug(jnp.concatenate([slope_rows, mask_rows, pad_rows], axis=0)))

        first_sel = jnp.sum(jnp.where(blk_row == 2 * qt, sel, 0.0), axis=0, keepdims=True)
        allowed = jnp.logical_or(causal, jnp.logical_and(first_key_second_qry, first_sel > 0.0))
        s = jnp.dot(ka_scr[hh, own, :], q_diag, preferred_element_type=F32)
        s = jnp.where(allowed, s, -jnp.inf)
        m0 = jnp.max(s, axis=0, keepdims=True)
        p = jnp.exp(s - m0)
        l0 = jnp.sum(p, axis=0, keepdims=True)
        acc0 = jnp.dot(v_t_d[hrows, :], p.astype(BF16), preferred_element_type=F32)
        init += [m0, l0, acc0]

    def scores(pair_idx, slot):
        keys = pl.ds(pl.multiple_of(pair_idx * qw, qw), qw)
        col_max = []
        for hh in range(2):
            s = jnp.dot(ka_scr[hh, keys, :], q_aug[hh], preferred_element_type=F32)
            s_scr[slot, hh] = s
            col_max.append(jnp.max(s, axis=0, keepdims=True))
        return col_max

    n_pairs = qt

    def body(jj, carry):
        slot = jj % 2
        v_pair = vtp_scr[jj]
        out = []
        for hh in range(2):
            m, l, acc, mx = carry[4 * hh:4 * hh + 4]
            m_new = jnp.maximum(m, mx)
            alpha = jnp.exp(m - m_new)
            p = jnp.exp(s_scr[slot, hh] - m_new)
            l = alpha * l + jnp.sum(p, axis=0, keepdims=True)
            acc = alpha * acc + jnp.dot(v_pair[hh * dh:(hh + 1) * dh, :], p.astype(BF16),
                                        preferred_element_type=F32)
            out.append([m_new, l, acc])
        nxt = scores(jnp.minimum(jj + 1, nb // 2 - 1), 1 - slot)
        return tuple(out[0] + [nxt[0]] + out[1] + [nxt[1]])

    mx0 = scores(0, 0)
    res = lax.fori_loop(0, n_pairs, body, tuple(init[0:3] + [mx0[0]] + init[3:6] + [mx0[1]]))
    o_ref[...] = jnp.concatenate([res[2] / res[1], res[6] / res[5]], axis=0).T


def _moba_prompt(q, k, v, slopes, batch, seq):
    m = q.shape[0]
    nb = seq // MOBA_BLOCK
    assert nb % 2 == 0 and MOBA_AUG_MASK0 + nb <= MOBA_DH
    pair = 2 * MOBA_DH
    n_pairs = MOBA_HEADS // 2
    qw = 2 * MOBA_BLOCK
    nt = nb // 2
    slope_rows = jnp.broadcast_to(slopes[:, None], (MOBA_HEADS, qw))
    seq_spec = pl.BlockSpec((seq, pair), lambda b, hp, qt: (b, hp))
    return pl.pallas_call(
        _moba_prompt_kernel,
        grid=(batch, n_pairs, nt),
        in_specs=[seq_spec, seq_spec, seq_spec, _full((MOBA_HEADS, qw))],
        out_specs=pl.BlockSpec((qw, pair), lambda b, hp, qt: (b * nt + qt, hp)),
        out_shape=jax.ShapeDtypeStruct((m, D_MODEL), F32),
        scratch_shapes=[pltpu.VMEM((nt, pair, qw), BF16), pltpu.VMEM((nt, pair, qw), BF16),
                        pltpu.VMEM((2, seq, pair), BF16), pltpu.VMEM((nb, pair), F32),
                        pltpu.VMEM((2, 2, qw, qw), F32)],
        compiler_params=_params(3),
        name="moba_prompt",
    )(q, k, v, slope_rows)


def _moba_sample_kernel(n_pages, pos0, pt_ref, q_ref, kn_ref, vn_ref, slope_ref, *refs):
    del pt_ref
    k_pages = refs[:n_pages]
    v_pages = refs[n_pages:2 * n_pages]
    o_ref = refs[2 * n_pages]
    per_blk = MOBA_BLOCK // PAGE_SIZE
    n_blk = n_pages // per_blk
    h_i = lax.broadcasted_iota(jnp.int32, (MOBA_HEADS, D_MODEL), 0)
    d_i = lax.broadcasted_iota(jnp.int32, (MOBA_HEADS, D_MODEL), 1)
    own_head = d_i // MOBA_DH == h_i
    q_bd = jnp.where(own_head, q_ref[0], 0.0)
    q_bd16 = q_bd.astype(BF16)
    slope = slope_ref[...]

    lane_n = lax.broadcasted_iota(jnp.int32, (D_MODEL, n_blk), 1)
    means = jnp.zeros((D_MODEL, n_blk), F32)
    for n in range(n_blk):
        tot = jnp.sum(sum(k_pages[n * per_blk + i][...] for i in range(per_blk)), axis=1, keepdims=True)
        means = jnp.where(lane_n == n, tot / float(MOBA_BLOCK), means)
    gate = jnp.dot(q_bd16, means.astype(BF16), preferred_element_type=F32)
    sel = _topk_mask(gate, n_blk, 1)

    lane = lax.broadcasted_iota(jnp.int32, (1, PAGE_SIZE), 1)
    scores = []
    for pg in range(n_pages):
        s = jnp.dot(q_bd16, k_pages[pg][...].astype(BF16), preferred_element_type=F32)
        dist = (pos0 - pg * PAGE_SIZE - lane).astype(F32)
        s = s - slope * dist
        n = pg // per_blk
        scores.append(jnp.where(sel[:, n:n + 1] > 0.0, s, -jnp.inf))
    k_new = kn_ref[0].astype(BF16).astype(F32)
    s_new = jnp.sum(q_bd16.astype(F32) * k_new, axis=1, keepdims=True)
    m = s_new
    for s in scores:
        m = jnp.maximum(m, jnp.max(s, axis=1, keepdims=True))
    p_new = jnp.exp(s_new - m)
    l = p_new
    d_e = lax.broadcasted_iota(jnp.int32, (D_MODEL, MOBA_HEADS), 0)
    h_e = lax.broadcasted_iota(jnp.int32, (D_MODEL, MOBA_HEADS), 1)
    expand = jnp.where(d_e // MOBA_DH == h_e, 1.0, 0.0).astype(BF16)
    acc = jnp.zeros((D_MODEL, PAGE_SIZE), F32)
    for pg in range(n_pages):
        p = jnp.exp(scores[pg] - m)
        l = l + jnp.sum(p, axis=1, keepdims=True)
        acc = acc + jnp.dot(expand, p.astype(BF16), preferred_element_type=F32) * v_pages[pg][...]
    o_past = jnp.sum(acc.T, axis=0, keepdims=True)
    row_of = lambda col: jnp.sum(jnp.where(own_head, col, 0.0), axis=0, keepdims=True)
    o_new = row_of(p_new.astype(BF16).astype(F32)) * vn_ref[0]
    o_ref[0] = (o_past + o_new) / row_of(l)


def _moba_sample(q, k_new, v_new, cache_k, cache_v, page_table, slopes, pos0):
    m = q.shape[0]
    n_pages = page_table.shape[1]
    assert pos0 == n_pages * PAGE_SIZE and pos0 % MOBA_BLOCK == 0
    n_phys = cache_k.shape[0]
    ck = jnp.transpose(cache_k, (0, 2, 3, 1)).reshape(n_phys, D_MODEL, PAGE_SIZE)
    cv = jnp.transpose(cache_v, (0, 2, 3, 1)).reshape(n_phys, D_MODEL, PAGE_SIZE)
    vec = pl.BlockSpec((1, 1, D_MODEL), lambda b, pt: (b, 0, 0))
    page_specs = [pl.BlockSpec((None, D_MODEL, PAGE_SIZE), lambda b, pt, pg=pg: (pt[b, pg], 0, 0))
                  for pg in range(n_pages)]
    grid_spec = pltpu.PrefetchScalarGridSpec(
        num_scalar_prefetch=1,
        grid=(m,),
        in_specs=[vec, vec, vec, pl.BlockSpec((MOBA_HEADS, 1), lambda b, pt: (0, 0))] + page_specs + page_specs,
        out_specs=vec,
    )
    out = pl.pallas_call(
        functools.partial(_moba_sample_kernel, n_pages, pos0),
        grid_spec=grid_spec,
        out_shape=jax.ShapeDtypeStruct((m, 1, D_MODEL), F32),
        compiler_params=_params(1),
        name="moba_sample",
    )(page_table, q.reshape(m, 1, D_MODEL), k_new.reshape(m, 1, D_MODEL), v_new.reshape(m, 1, D_MODEL),
      slopes.reshape(MOBA_HEADS, 1), *([ck] * n_pages), *([cv] * n_pages))
    return out.reshape(m, D_MODEL)


def _layer_tail(layer, xp, xs, mix_p, mix_s, p_p, p_s, w_out, ln_g, ln_b, w_router, b_router,
                w_gu, b_gu, w_down, b_down, w_pg, w_pp):
    n_p, n_s = xp.shape[0], xs.shape[0]
    lg0, lb0 = ln_g[0:1], ln_b[0:1]
    lg1, lb1 = ln_g[1:2], ln_b[1:2]
    br = b_router.reshape(1, N_EXPERTS)
    x1, idxp, gatep = _mix(xp, mix_p, w_out, lg0, lb0, w_router, br, ROW_TILE, n_total=n_p + n_s)
    x1, idxs, gates = _mix(xs, mix_s, w_out, lg0, lb0, w_router, br, n_s, x1_buf=x1)

    idx = jnp.concatenate([idxp, idxs], axis=0)
    rank, counts = _rank(idx)
    counts = counts[0]
    bm = MOE_BLOCK
    n_tok = n_p + n_s
    n_blocks = (n_tok * TOP_K + N_EXPERTS * (bm - 1)) // bm
    padded = (counts + bm - 1) // bm * bm
    pend = jnp.cumsum(padded)
    pstart = pend - padded
    dest = pstart[idx] + rank
    n_used = (pend[-1] // bm).astype(jnp.int32)
    blk_ids = jnp.minimum(jnp.arange(n_blocks, dtype=jnp.int32), n_used - 1)
    block_e = jnp.sum((pend[None, :] <= (blk_ids * bm)[:, None]).astype(jnp.int32), axis=1)
    block_e = jnp.minimum(block_e, N_EXPERTS - 1)
    order = jnp.argsort(idx.reshape(-1), stable=True).astype(jnp.int32)
    start = jnp.cumsum(counts) - counts
    r_in_group = (blk_ids * bm - pstart[block_e])[:, None] + jnp.arange(bm, dtype=jnp.int32)[None, :]
    live = r_in_group < counts[block_e][:, None]
    src = jnp.where(live, start[block_e][:, None] + r_in_group, 0)
    row_tok = jnp.where(live, order.at[src].get(mode="promise_in_bounds") // TOP_K, 0).reshape(-1)
    gather = lambda src, rows: src.at[rows].get(mode="promise_in_bounds")
    xs_rows = gather(x1, row_tok)
    ys = _experts(layer, block_e, n_used.reshape(1), xs_rows, w_gu, b_gu, w_down, b_down)
    ysg_p = gather(ys, dest[:n_p].T.reshape(-1))
    ysg_s = gather(ys, dest[n_p:].T.reshape(-1))
    yp = _finish(x1, 0, n_p, ysg_p, gatep, p_p, layer, lg1, lb1, w_pg, w_pp, ROW_TILE)
    ys_out = _finish(x1, n_p, n_s, ysg_s, gates, p_s, layer, lg1, lb1, w_pg, w_pp, n_s)
    return yp, ys_out


def kernel(x_prompt, x_sample, state_gla, state_pool, cache_k, cache_v, page_table, p_prompt, p_sample, w_in_ab, gla_w_alpha, gla_b_alpha, gla_norm_g, pool_w, pool_scale, w_out_ab, w_qkv_c, w_out_c, ln_g, ln_b, moe_w_router, moe_b_router, moe_w_gu, moe_b_gu, moe_w_down, moe_b_down, ple_w_gate, ple_w_proj):
    batch, seq, _ = x_prompt.shape
    n_s = x_sample.shape[0]
    n_p = batch * seq
    pos0 = page_table.shape[1] * PAGE_SIZE
    xp = x_prompt.reshape(n_p, D_MODEL)
    xs = x_sample.reshape(n_s, D_MODEL)

    def tail(i, xp, xs, mix_p, mix_s, w_out):
        depth = p_prompt.shape[0]
        return _layer_tail(i, xp, xs, mix_p, mix_s, p_prompt.reshape(depth, n_p, -1), p_sample.reshape(depth, n_s, -1),
                           w_out, ln_g[i], ln_b[i], moe_w_router[i], moe_b_router[i], moe_w_gu,
                           moe_b_gu, moe_w_down, moe_b_down, ple_w_gate[i], ple_w_proj[i])

    w_in = w_in_ab[0]
    c_a = 2 * QK_A + 2 * V_A
    w_main = jnp.concatenate([w_in[:, :c_a], w_in[:, c_a + GLA_LOWRANK:]], axis=1)
    w_a = w_in[:, c_a:c_a + GLA_LOWRANK]
    b_alpha = gla_b_alpha[0].reshape(1, QK_A)
    norm_g = gla_norm_g[0].reshape(1, GLA_DV)
    scale = pool_scale[0].reshape(1, POOL_CH)
    qkl_p, v_p, sg_p, u_p = _proj_ab(xp, w_main, w_a, gla_w_alpha[0], b_alpha, ROW_TILE)
    qkl_s, v_s, sg_s, u_s = _proj_ab(xs, w_main, w_a, gla_w_alpha[0], b_alpha, n_s)
    mix_p, gla_p = _gla_prompt(qkl_p, v_p, sg_p, norm_g, batch, seq, ROW_TILE)
    mix_p = _pool_prompt(u_p, mix_p, pool_w[0], scale, batch, seq, ROW_TILE)
    mix_s, gla_s = _gla_sample(qkl_s, v_s, sg_s, norm_g, state_gla[0].reshape(n_s, QK_A, GLA_DV))
    mix_s, pool_s = _pool_sample(u_s, state_pool[0], mix_s, pool_w[0], scale, pos0)
    pool_p = u_p.reshape(batch, seq, POOL_CH)[:, seq - POOL_HIST:]
    xp, xs = tail(0, xp, xs, mix_p, mix_s, w_out_ab[0])

    slopes = jnp.exp2(-8.0 * jnp.arange(1, MOBA_HEADS + 1, dtype=F32) / MOBA_HEADS)
    q_p, k_p, v_p2 = _qkv(xp, w_qkv_c[0], ROW_TILE)
    q_s, k_s, v_s2 = _qkv(xs, w_qkv_c[0], n_s)
    o_p = _moba_prompt(q_p, k_p, v_p2, slopes, batch, seq)
    o_s = _moba_sample(q_s, k_s, v_s2, cache_k[0], cache_v[0], page_table, slopes, pos0)
    xp, xs = tail(1, xp, xs, o_p, o_s, w_out_c[0])

    hd = (MOBA_HEADS, MOBA_DH)
    return (xp.reshape(batch, seq, D_MODEL), xs.reshape(n_s, 1, D_MODEL),
            gla_p.reshape(1, batch, GLA_HEADS, GLA_DK, GLA_DV), gla_s.reshape(1, n_s, GLA_HEADS, GLA_DK, GLA_DV),
            pool_p[None], pool_s[None],
            k_p.reshape(1, batch, seq, *hd), v_p2.reshape(1, batch, seq, *hd),
            k_s.reshape(1, n_s, 1, *hd), v_s2.reshape(1, n_s, 1, *hd))
```

```python
import functools
import math

import numpy as np
import jax
import jax.numpy as jnp
from jax import lax
from jax.experimental import pallas as pl
from jax.experimental.pallas import tpu as pltpu

F32 = jnp.float32
BF16 = jnp.bfloat16
HI = lax.Precision.HIGHEST

D_MODEL = 1024
GLA_HEADS = 4
GLA_DK = 64
GLA_DV = 128
GLA_LOWRANK = 16
GLA_TAU = 16.0
QK_A = GLA_HEADS * GLA_DK
V_A = GLA_HEADS * GLA_DV
POOL_WINDOWS = (2, 4, 8, 16)
POOL_CH = 512
POOL_GROUP = 128
POOL_HIST = 15
MOBA_HEADS = 16
MOBA_DH = 64
MOBA_BLOCK = 256
MOBA_TOPK = 3
PAGE_SIZE = 128
N_EXPERTS = 32
TOP_K = 4
D_EXPERT = 1024
SWIGLU_LIMIT = 7.0
SWIGLU_ALPHA = 1.702
DEPTH = 2
DEEPNORM_ALPHA = (2 * DEPTH) ** 0.25
LN_EPS = 1e-5

ROW_TILE = 512
SUB_TILE = 256
GLA_CHUNK = 128
GLA_LEVELS = 7
MOE_BLOCK = 512
RANK_TILE = 384
VMEM_LIMIT = 56 * 1024 * 1024


def _params(n_axes, vmem=VMEM_LIMIT):
    return pltpu.CompilerParams(dimension_semantics=("arbitrary",) * n_axes, vmem_limit_bytes=vmem)


def _bdot(a, b):
    return jnp.dot(a.astype(BF16), b.astype(BF16), preferred_element_type=F32)


def _layer_norm(y, g, b):
    mu = jnp.mean(y, axis=-1, keepdims=True)
    yc = y - mu
    var = jnp.mean(yc * yc, axis=-1, keepdims=True)
    return yc * lax.rsqrt(var + LN_EPS) * g + b


def _full(shape):
    n = len(shape)
    return pl.BlockSpec(shape, lambda *_: (0,) * n)


def _proj_ab_kernel(x_ref, wm_ref, wa_ref, walpha_ref, balpha_ref,
                    qkl_ref, v_ref, sg_ref, u_ref, wm_bf, wa_bf):
    @pl.when(pl.program_id(0) == 0)
    def _():
        wm_bf[...] = wm_ref[...].astype(BF16)
        wa_bf[...] = wa_ref[...].astype(BF16)

    xb = x_ref[...].astype(BF16)
    qk = jnp.dot(xb, wm_bf[:, 0:2 * QK_A], preferred_element_type=F32)
    qkl_ref[:, 0:QK_A] = qk[:, 0:QK_A] * (GLA_DK ** -0.5)
    qkl_ref[:, QK_A:2 * QK_A] = qk[:, QK_A:2 * QK_A]
    a_lr = jnp.dot(xb, wa_bf[...], preferred_element_type=F32)
    z = _bdot(a_lr, walpha_ref[...]) + balpha_ref[...]
    log_sig = jnp.minimum(z, 0.0) - jnp.log1p(jnp.exp(-jnp.abs(z)))
    qkl_ref[:, 2 * QK_A:3 * QK_A] = log_sig / GLA_TAU
    c0 = 2 * QK_A
    v_ref[...] = jnp.dot(xb, wm_bf[:, c0:c0 + V_A], preferred_element_type=F32)
    g = jnp.dot(xb, wm_bf[:, c0 + V_A:c0 + 2 * V_A], preferred_element_type=F32)
    sg_ref[...] = g * jax.nn.sigmoid(g)
    u_ref[...] = jnp.dot(xb, wm_bf[:, c0 + 2 * V_A:c0 + 2 * V_A + POOL_CH], preferred_element_type=F32)


def _proj_ab(x, w_main, w_a, w_alpha, b_alpha, tm):
    m = x.shape[0]
    nmain = w_main.shape[1]
    row = lambda w: pl.BlockSpec((tm, w), lambda i: (i, 0))
    return pl.pallas_call(
        _proj_ab_kernel,
        grid=(m // tm,),
        in_specs=[row(D_MODEL), _full((D_MODEL, nmain)), _full((D_MODEL, GLA_LOWRANK)),
                  _full((GLA_LOWRANK, QK_A)), _full((1, QK_A))],
        out_specs=[row(3 * QK_A), row(V_A), row(V_A), row(POOL_CH)],
        out_shape=[jax.ShapeDtypeStruct((m, 3 * QK_A), F32), jax.ShapeDtypeStruct((m, V_A), F32),
                   jax.ShapeDtypeStruct((m, V_A), F32), jax.ShapeDtypeStruct((m, POOL_CH), F32)],
        scratch_shapes=[pltpu.VMEM((D_MODEL, nmain), BF16), pltpu.VMEM((D_MODEL, GLA_LOWRANK), BF16)],
        compiler_params=_params(1),
        name="proj_ab",
    )(x, w_main, w_a, w_alpha, b_alpha)


def _gla_tables():
    c = GLA_CHUNK
    i = np.arange(c)[:, None]
    s = np.arange(c)[None, :]
    mats = [(s <= i), (s > i)]
    for lev in range(GLA_LEVELS):
        p = GLA_LEVELS - 1 - lev
        half = 1 << p
        start = (i >> (p + 1)) << (p + 1)
        mid = start + half - 1
        upper = i >= start + half
        mats.append(np.where(upper, (s > mid) & (s <= i), (s > i) & (s <= mid)))
    seg = np.concatenate(mats, axis=0).astype(np.float32)
    j = np.arange(c)[None, :]
    x = i ^ j
    lvl = np.full((c, c), GLA_LEVELS + 1, np.int32)
    lvl[np.arange(c), np.arange(c)] = GLA_LEVELS
    for lev in range(GLA_LEVELS):
        p = GLA_LEVELS - 1 - lev
        lvl = np.where(((x >> p) == 1) & (((i >> p) & 1) == 1), lev, lvl)
    lvl4 = np.tile(lvl, (GLA_HEADS, 1)).astype(np.int32)
    lane_head = (np.arange(QK_A) // GLA_DK)[None, :]
    row_head = (np.arange(GLA_HEADS * c) // c)[:, None]
    hm4 = (lane_head == row_head).astype(np.float32)
    return seg, lvl4, hm4


def _gla_chunk(q, k, la, v, s_all, seg, lvl4, hm4):
    c = GLA_CHUNK
    e = sum(jnp.dot(seg, part.astype(BF16), preferred_element_type=F32) for part in _split3(la))
    w = jnp.exp(e)
    w_b = w[0:c]
    w_k = w[c:2 * c]

    def stack_heads(t):
        return (jnp.concatenate([t] * GLA_HEADS, axis=0) * hm4).astype(BF16)

    o_inter = jnp.dot(stack_heads(q * w_b), s_all.astype(BF16), preferred_element_type=F32)
    a = jnp.zeros((GLA_HEADS * c, c), F32)
    for lev in range(GLA_LEVELS + 1):
        if lev < GLA_LEVELS:
            w_l = w[(2 + lev) * c:(3 + lev) * c]
            ql, kl = q * w_l, k * w_l
        else:
            ql, kl = q, k
        p_l = lax.dot_general(stack_heads(ql), kl.astype(BF16), (((1,), (1,)), ((), ())),
                              preferred_element_type=F32)
        a = jnp.where(lvl4 == lev, p_l, a)
    a = a.astype(BF16)
    outs = []
    for h in range(GLA_HEADS):
        v_h = v[:, h * GLA_DV:(h + 1) * GLA_DV].astype(BF16)
        o_h = o_inter[h * c:(h + 1) * c] + jnp.dot(a[h * c:(h + 1) * c], v_h, preferred_element_type=F32)
        outs.append(o_h)
    ks_t = (k * w_k).T.astype(BF16)
    kv = jnp.dot(ks_t, v.astype(BF16), preferred_element_type=F32)
    dec = jnp.exp(jnp.sum(la.T, axis=1, keepdims=True))
    new_rows = []
    for h in range(GLA_HEADS):
        rows = slice(h * GLA_DK, (h + 1) * GLA_DK)
        new_rows.append(dec[rows] * s_all[rows] + kv[rows, h * GLA_DV:(h + 1) * GLA_DV])
    return outs, jnp.concatenate(new_rows, axis=0)


def _gla_finish(o_h, norm_g, sg_h):
    o_h = o_h * lax.rsqrt(jnp.mean(o_h * o_h, axis=-1, keepdims=True) + LN_EPS) * norm_g
    return o_h * sg_h


def _gla_prompt_kernel(qkl_ref, v_ref, sg_ref, ng_ref, seg_ref, lvl_ref, hm_ref,
                       mix_ref, state_ref, s_scr):
    t = pl.program_id(1)

    @pl.when(t == 0)
    def _():
        s_scr[...] = jnp.zeros_like(s_scr)

    seg = seg_ref[...]
    lvl4 = lvl_ref[...]
    hm4 = hm_ref[...]
    norm_g = ng_ref[...]
    n_chunks = qkl_ref.shape[0] // GLA_CHUNK

    s_all = s_scr[...]
    for ci in range(n_chunks):
        rows = pl.ds(ci * GLA_CHUNK, GLA_CHUNK)
        q = qkl_ref[rows, 0:QK_A]
        k = qkl_ref[rows, QK_A:2 * QK_A]
        la = qkl_ref[rows, 2 * QK_A:3 * QK_A]
        v = v_ref[rows, :]
        outs, s_all = _gla_chunk(q, k, la, v, s_all, seg, lvl4, hm4)
        if ci == n_chunks - 1:
            s_scr[...] = s_all
        for h in range(GLA_HEADS):
            cols = slice(h * GLA_DV, (h + 1) * GLA_DV)
            mix_ref[rows, cols] = _gla_finish(outs[h], norm_g, sg_ref[rows, cols])

    @pl.when(t == pl.num_programs(1) - 1)
    def _():
        state_ref[0] = s_scr[...]


def _gla_prompt(qkl, v, sg, norm_g, batch, seq, tm):
    m = qkl.shape[0]
    nt = seq // tm
    seg, lvl4, hm4 = _gla_tables()
    row = lambda w: pl.BlockSpec((tm, w), lambda b, t: (b * nt + t, 0))
    return pl.pallas_call(
        _gla_prompt_kernel,
        grid=(batch, nt),
        in_specs=[row(3 * QK_A), row(V_A), row(V_A), _full((1, GLA_DV)),
                  _full(seg.shape), _full(lvl4.shape), _full(hm4.shape)],
        out_specs=[pl.BlockSpec((tm, V_A), lambda b, t: (b * nt + t, 0)),
                   pl.BlockSpec((1, QK_A, GLA_DV), lambda b, t: (b, 0, 0))],
        out_shape=[jax.ShapeDtypeStruct((m, D_MODEL), F32),
                   jax.ShapeDtypeStruct((batch, QK_A, GLA_DV), F32)],
        scratch_shapes=[pltpu.VMEM((QK_A, GLA_DV), F32)],
        compiler_params=_params(2),
        name="gla_prompt",
    )(qkl, v, sg, norm_g, jnp.asarray(seg, BF16), jnp.asarray(lvl4), jnp.asarray(hm4))


GLA_SAMPLE_ROWS = 8


def _gla_sample_kernel(qkl_ref, v_ref, sg_ref, ng_ref, s0_ref, mix_ref, s1_ref):
    ones = jnp.ones((QK_A, GLA_DV), F32)
    r_i = lax.broadcasted_iota(jnp.int32, (QK_A, QK_A), 0)
    c_i = lax.broadcasted_iota(jnp.int32, (QK_A, QK_A), 1)
    eye = r_i == c_i
    norm_g = ng_ref[...]

    def col_bcast(row):
        diag = jnp.where(eye, jnp.broadcast_to(row, (QK_A, QK_A)), 0.0)
        return jnp.dot(diag, ones, precision=HI, preferred_element_type=F32)

    for r in range(GLA_SAMPLE_ROWS):
        q = col_bcast(qkl_ref[r:r + 1, 0:QK_A])
        k = col_bcast(qkl_ref[r:r + 1, QK_A:2 * QK_A])
        dec = jnp.exp(col_bcast(qkl_ref[r:r + 1, 2 * QK_A:3 * QK_A]))
        for h in range(GLA_HEADS):
            rows = slice(h * GLA_DK, (h + 1) * GLA_DK)
            cols = slice(h * GLA_DV, (h + 1) * GLA_DV)
            s_new = dec[rows] * s0_ref[r, rows, :] + k[rows] * v_ref[r:r + 1, cols]
            s1_ref[r, rows, :] = s_new
            o_h = jnp.sum(q[rows] * s_new, axis=0, keepdims=True)
            mix_ref[r:r + 1, cols] = _gla_finish(o_h, norm_g, sg_ref[r:r + 1, cols])


def _gla_sample(qkl, v, sg, norm_g, s0):
    m = qkl.shape[0]
    rb = GLA_SAMPLE_ROWS
    row = lambda w: pl.BlockSpec((rb, w), lambda i: (i, 0))
    st = pl.BlockSpec((rb, QK_A, GLA_DV), lambda i: (i, 0, 0))
    return pl.pallas_call(
        _gla_sample_kernel,
        grid=(m // rb,),
        in_specs=[row(3 * QK_A), row(V_A), row(V_A), _full((1, GLA_DV)), st],
        out_specs=[pl.BlockSpec((rb, V_A), lambda i: (i, 0)), st],
        out_shape=[jax.ShapeDtypeStruct((m, D_MODEL), F32),
                   jax.ShapeDtypeStruct((m, QK_A, GLA_DV), F32)],
        compiler_params=_params(1),
        name="gla_sample",
    )(qkl, v, sg, norm_g, s0)


def _pool_project(d_groups, wp_ref, scale_ref, mix_ref):
    for g in range(len(POOL_WINDOWS)):
        cols = slice(g * POOL_GROUP, (g + 1) * POOL_GROUP)
        mix_ref[:, cols] = _bdot(d_groups[g], wp_ref[g]) * scale_ref[:, cols]


def _pool_prompt_kernel(u_ref, prev_ref, wp_ref, scale_ref, mixin_ref, mix_ref):
    del mixin_ref
    t = pl.program_id(1)
    tm = u_ref.shape[0]
    hist = prev_ref.shape[0]
    prev = jnp.where(t > 0, prev_ref[...], 0.0)
    z = jnp.concatenate([prev, u_ref[...]], axis=0)
    pos = t * tm + lax.broadcasted_iota(jnp.int32, (tm, 1), 0)
    d_groups = []
    for g, w in enumerate(POOL_WINDOWS):
        cols = slice(g * POOL_GROUP, (g + 1) * POOL_GROUP)
        s = z[:, cols]
        shift = 1
        while shift < w:
            s = s + pltpu.roll(s, shift, 0)
            shift *= 2
        cnt = jnp.minimum(w, pos + 1).astype(F32)
        d_groups.append(s[hist:] / cnt - z[hist:, cols])
    _pool_project(d_groups, wp_ref, scale_ref, mix_ref)


def _pool_prompt(u, mix, w_pool, scale, batch, seq, tm):
    nt = seq // tm
    hist = 16
    assert hist > POOL_HIST and tm % hist == 0
    per = tm // hist
    return pl.pallas_call(
        _pool_prompt_kernel,
        grid=(batch, nt),
        in_specs=[pl.BlockSpec((tm, POOL_CH), lambda b, t: (b * nt + t, 0)),
                  pl.BlockSpec((hist, POOL_CH), lambda b, t: (jnp.maximum((b * nt + t) * per - 1, 0), 0)),
                  _full(w_pool.shape), _full((1, POOL_CH)),
                  pl.BlockSpec(memory_space=pl.ANY)],
        out_specs=pl.BlockSpec((tm, POOL_CH), lambda b, t: (b * nt + t, 1)),
        out_shape=jax.ShapeDtypeStruct(mix.shape, F32),
        input_output_aliases={4: 0},
        compiler_params=_params(2),
        name="pool_prompt",
    )(u, u, w_pool, scale, mix)


def _pool_sample_kernel(u_ref, st_ref, wp_ref, scale_ref, mixin_ref, mix_ref, st_out_ref):
    del mixin_ref
    u = u_ref[...]
    d_groups = []
    for g, w in enumerate(POOL_WINDOWS):
        cols = slice(g * POOL_GROUP, (g + 1) * POOL_GROUP)
        win = u[:, cols] + jnp.sum(st_ref[:, POOL_HIST - (w - 1):POOL_HIST, cols], axis=1)
        d_groups.append(win / float(w) - u[:, cols])
    _pool_project(d_groups, wp_ref, scale_ref, mix_ref)
    st_out_ref[:, 0:POOL_HIST - 1, :] = st_ref[:, 1:POOL_HIST, :]
    st_out_ref[:, POOL_HIST - 1:POOL_HIST, :] = u[:, None, :]


def _pool_sample(u, st, mix, w_pool, scale, pos0):
    m = u.shape[0]
    assert pos0 + 1 >= max(POOL_WINDOWS)
    return pl.pallas_call(
        _pool_sample_kernel,
        grid=(1,),
        in_specs=[_full((m, POOL_CH)), _full(st.shape), _full(w_pool.shape), _full((1, POOL_CH)),
                  pl.BlockSpec(memory_space=pl.ANY)],
        out_specs=[pl.BlockSpec((m, POOL_CH), lambda i: (0, 1)), _full(st.shape)],
        out_shape=[jax.ShapeDtypeStruct(mix.shape, F32), jax.ShapeDtypeStruct(st.shape, F32)],
        input_output_aliases={4: 0},
        compiler_params=_params(1),
        name="pool_sample",
    )(u, st, w_pool, scale, mix)


def _sub_tiles(tm):
    ts = SUB_TILE if tm % SUB_TILE == 0 else tm
    return [slice(s, s + ts) for s in range(0, tm, ts)]


def _mix_kernel(x_ref, mix_ref, w_ref, lng_ref, lnb_ref, wr_ref, br_ref, *rest):
    x1_ref, idx_ref, gate_ref, cnt_ref, w_bf = rest[-5:]

    @pl.when(pl.program_id(0) == 0)
    def _():
        w_bf[...] = w_ref[...].astype(BF16)
        cnt_ref[...] = jnp.zeros_like(cnt_ref)

    for rows in _sub_tiles(x_ref.shape[0]):
        h = jnp.dot(mix_ref[rows, :].astype(BF16), w_bf[...], preferred_element_type=F32)
        x1 = _layer_norm(DEEPNORM_ALPHA * x_ref[rows, :] + h, lng_ref[...], lnb_ref[...])
        x1_ref[rows, :] = x1
        logits = _bdot(x1, wr_ref[...]) + br_ref[...]
        ts = logits.shape[0]
        lane = lax.broadcasted_iota(jnp.int32, (ts, N_EXPERTS), 1)
        lane_k = lax.broadcasted_iota(jnp.int32, (ts, TOP_K), 1)
        idx_out = jnp.zeros((ts, TOP_K), jnp.int32)
        val_out = jnp.zeros((ts, TOP_K), F32)
        cur = logits
        chosen = jnp.zeros((ts, N_EXPERTS), F32)
        for kk in range(TOP_K):
            mval = jnp.max(cur, axis=1, keepdims=True)
            midx = jnp.min(jnp.where(cur == mval, lane.astype(F32), float(N_EXPERTS)), axis=1,
                           keepdims=True).astype(jnp.int32)
            idx_out = jnp.where(lane_k == kk, midx, idx_out)
            val_out = jnp.where(lane_k == kk, mval, val_out)
            picked = lane == midx
            cur = jnp.where(picked, -jnp.inf, cur)
            chosen = jnp.where(picked, 1.0, chosen)
        cnt_ref[...] += jnp.sum(chosen, axis=0, keepdims=True)
        ex = jnp.exp(val_out - val_out[:, 0:1])
        idx_ref[rows, :] = idx_out
        gate_ref[rows, :] = ex / jnp.sum(ex, axis=1, keepdims=True)


def _mix(x, mix, w_out, ln_g, ln_b, w_router, b_router, tm, x1_buf=None, n_total=None):
    m = x.shape[0]
    n_total = n_total if x1_buf is None else x1_buf.shape[0]
    off_blocks = 0 if x1_buf is None else (n_total - m) // tm
    row = lambda w: pl.BlockSpec((tm, w), lambda i: (i, 0))
    in_specs = [row(D_MODEL), row(D_MODEL), _full((D_MODEL, D_MODEL)), _full((1, D_MODEL)),
                _full((1, D_MODEL)), _full((D_MODEL, N_EXPERTS)), _full((1, N_EXPERTS))]
    args = [x, mix, w_out, ln_g, ln_b, w_router, b_router]
    aliases = {}
    if x1_buf is not None:
        assert (n_total - m) % tm == 0
        in_specs.append(pl.BlockSpec(memory_space=pl.ANY))
        args.append(x1_buf)
        aliases = {len(args) - 1: 0}
    return pl.pallas_call(
        _mix_kernel,
        grid=(m // tm,),
        in_specs=in_specs,
        out_specs=[pl.BlockSpec((tm, D_MODEL), lambda i: (i + off_blocks, 0)), row(TOP_K), row(TOP_K),
                   _full((1, N_EXPERTS))],
        out_shape=[jax.ShapeDtypeStruct((n_total, D_MODEL), F32),
                   jax.ShapeDtypeStruct((m, TOP_K), jnp.int32), jax.ShapeDtypeStruct((m, TOP_K), F32),
                   jax.ShapeDtypeStruct((1, N_EXPERTS), F32)],
        scratch_shapes=[pltpu.VMEM((D_MODEL, D_MODEL), BF16)],
        input_output_aliases=aliases,
        compiler_params=_params(1),
        name="mix",
    )(*args)


def _rank_kernel(idx_ref, pstart_ref, rank_ref, carry):
    i = pl.program_id(0)

    @pl.when(i == 0)
    def _():
        carry[...] = pstart_ref[...].astype(F32)

    idx = idx_ref[...]
    tm = idx.shape[0]
    lane = lax.broadcasted_iota(jnp.int32, (tm, N_EXPERTS), 1)
    onehots = [(idx[:, kk:kk + 1] == lane) for kk in range(TOP_K)]
    member = sum(jnp.where(o, 1.0, 0.0) for o in onehots)
    r_i = lax.broadcasted_iota(jnp.int32, (tm, tm), 0)
    c_i = lax.broadcasted_iota(jnp.int32, (tm, tm), 1)
    strict_lower = jnp.where(c_i < r_i, 1.0, 0.0).astype(BF16)
    before = jnp.dot(strict_lower, member.astype(BF16), preferred_element_type=F32) + carry[...]
    lane_k = lax.broadcasted_iota(jnp.int32, (tm, TOP_K), 1)
    rank = jnp.zeros((tm, TOP_K), F32)
    for kk in range(TOP_K):
        r_k = jnp.sum(jnp.where(onehots[kk], before, 0.0), axis=1, keepdims=True)
        rank = jnp.where(lane_k == kk, r_k, rank)
    rank_ref[...] = rank.astype(jnp.int32)
    carry[...] = carry[...] + jnp.sum(member, axis=0, keepdims=True)


def _rank(idx, pstart):
    n = idx.shape[0]
    tm = RANK_TILE
    return pl.pallas_call(
        _rank_kernel,
        grid=(n // tm,),
        in_specs=[pl.BlockSpec((tm, TOP_K), lambda i: (i, 0)), _full((1, N_EXPERTS))],
        out_specs=pl.BlockSpec((tm, TOP_K), lambda i: (i, 0)),
        out_shape=jax.ShapeDtypeStruct((n, TOP_K), jnp.int32),
        scratch_shapes=[pltpu.VMEM((1, N_EXPERTS), F32)],
        compiler_params=_params(1),
        name="moe_rank",
    )(idx, pstart.reshape(1, N_EXPERTS))


def _expert_kernel(be_ref, nu_ref, xs_ref, wgu_ref, bgu_ref, wd_ref, bd_ref, ys_ref, wgu_bf, wd_bf):
    blk = pl.program_id(0)
    prev = be_ref[jnp.maximum(blk - 1, 0)]
    fresh = jnp.logical_or(blk == 0, be_ref[blk] != prev)
    used = blk < nu_ref[0]

    @pl.when(jnp.logical_and(fresh, used))
    def _():
        wgu_bf[...] = wgu_ref[0, 0].astype(BF16)
        wd_bf[...] = wd_ref[0, 0].astype(BF16)

    @pl.when(used)
    def _():
        xb = xs_ref[...].astype(BF16)
        acc = jnp.zeros(ys_ref.shape, F32) + bd_ref[0, 0]
        half = D_EXPERT // 2
        for c in range(2):
            cg = slice(c * half, (c + 1) * half)
            cu = slice(D_EXPERT + c * half, D_EXPERT + (c + 1) * half)
            gate = jnp.dot(xb, wgu_bf[:, cg], preferred_element_type=F32) + bgu_ref[0, 0, :, cg]
            up = jnp.dot(xb, wgu_bf[:, cu], preferred_element_type=F32) + bgu_ref[0, 0, :, cu]
            gate = jnp.minimum(gate, SWIGLU_LIMIT)
            up = jnp.clip(up, -SWIGLU_LIMIT, SWIGLU_LIMIT)
            act = (up + 1.0) * gate * jax.nn.sigmoid(SWIGLU_ALPHA * gate)
            acc = acc + jnp.dot(act.astype(BF16), wd_bf[cg, :], preferred_element_type=F32)
        ys_ref[...] = acc

    @pl.when(jnp.logical_not(used))
    def _():
        ys_ref[...] = jnp.zeros_like(ys_ref)


def _experts(layer, block_e, n_used, xs, w_gu, b_gu, w_down, b_down):
    p = xs.shape[0]
    bm = MOE_BLOCK
    nb = p // bm
    depth = w_gu.shape[0]
    last = lambda blk, nu: jnp.minimum(blk, nu[0] - 1)
    grid_spec = pltpu.PrefetchScalarGridSpec(
        num_scalar_prefetch=2,
        grid=(nb,),
        in_specs=[pl.BlockSpec((bm, D_MODEL), lambda blk, be, nu: (last(blk, nu), 0)),
                  pl.BlockSpec((1, 1, D_MODEL, 2 * D_EXPERT), lambda blk, be, nu: (layer, be[blk], 0, 0)),
                  pl.BlockSpec((1, 1, 1, 2 * D_EXPERT), lambda blk, be, nu: (layer, be[blk], 0, 0)),
                  pl.BlockSpec((1, 1, D_EXPERT, D_MODEL), lambda blk, be, nu: (layer, be[blk], 0, 0)),
                  pl.BlockSpec((1, 1, 1, D_MODEL), lambda blk, be, nu: (layer, be[blk], 0, 0))],
        out_specs=pl.BlockSpec((bm, D_MODEL), lambda blk, be, nu: (blk, 0)),
        scratch_shapes=[pltpu.VMEM((D_MODEL, 2 * D_EXPERT), BF16), pltpu.VMEM((D_EXPERT, D_MODEL), BF16)],
    )
    return pl.pallas_call(
        _expert_kernel,
        grid_spec=grid_spec,
        out_shape=jax.ShapeDtypeStruct((p, D_MODEL), F32),
        compiler_params=_params(1),
        name="moe_experts",
    )(block_e, n_used, xs, w_gu, b_gu.reshape(depth, N_EXPERTS, 1, 2 * D_EXPERT), w_down,
      b_down.reshape(depth, N_EXPERTS, 1, D_MODEL))


def _finish_kernel(x1_ref, y0_ref, y1_ref, y2_ref, y3_ref, gate_ref, p_ref, lng_ref, lnb_ref,
                   wpg_ref, wpp_ref, out_ref, wpg_bf):
    @pl.when(pl.program_id(0) == 0)
    def _():
        wpg_bf[...] = wpg_ref[...].astype(BF16)

    for rows in _sub_tiles(out_ref.shape[0]):
        gates = gate_ref[rows, :]
        moe = y0_ref[rows, :] * gates[:, 0:1]
        for kk, y_ref in enumerate((y1_ref, y2_ref, y3_ref), start=1):
            moe = moe + y_ref[rows, :] * gates[:, kk:kk + 1]
        x2 = _layer_norm(DEEPNORM_ALPHA * x1_ref[rows, :] + moe, lng_ref[...], lnb_ref[...])
        pg = jax.nn.sigmoid(jnp.dot(x2.astype(BF16), wpg_bf[...], preferred_element_type=F32))
        pp = _bdot(p_ref[rows, :], wpp_ref[...])
        out_ref[rows, :] = x2 + pg * pp


def _finish(x1_all, row0, m, ysg, gates, p_all, layer, ln_g, ln_b, w_pg, w_pp, tm):
    nt = m // tm
    assert row0 % tm == 0
    row = lambda w: pl.BlockSpec((tm, w), lambda i: (i, 0))
    ple = p_all.shape[2]
    y_specs = [pl.BlockSpec((tm, D_MODEL), lambda i, kk=kk: (kk * nt + i, 0)) for kk in range(TOP_K)]
    x1_spec = pl.BlockSpec((tm, D_MODEL), lambda i: (i + row0 // tm, 0))
    p_spec = pl.BlockSpec((None, tm, ple), lambda i: (layer, i, 0))
    return pl.pallas_call(
        _finish_kernel,
        grid=(nt,),
        in_specs=[x1_spec] + y_specs + [row(TOP_K), p_spec, _full((1, D_MODEL)), _full((1, D_MODEL)),
                                        _full((D_MODEL, D_MODEL)), _full((ple, D_MODEL))],
        out_specs=row(D_MODEL),
        out_shape=jax.ShapeDtypeStruct((m, D_MODEL), F32),
        scratch_shapes=[pltpu.VMEM((D_MODEL, D_MODEL), BF16)],
        compiler_params=_params(1),
        name="finish",
    )(x1_all, ysg, ysg, ysg, ysg, gates, p_all, ln_g, ln_b, w_pg, w_pp)


def _qkv_kernel(x_ref, w_ref, q_ref, k_ref, v_ref, w_bf):
    @pl.when(pl.program_id(0) == 0)
    def _():
        w_bf[...] = w_ref[...].astype(BF16)

    xb = x_ref[...].astype(BF16)
    q_ref[...] = jnp.dot(xb, w_bf[:, 0:D_MODEL], preferred_element_type=F32) * (MOBA_DH ** -0.5)
    k_ref[...] = jnp.dot(xb, w_bf[:, D_MODEL:2 * D_MODEL], preferred_element_type=F32)
    v_ref[...] = jnp.dot(xb, w_bf[:, 2 * D_MODEL:3 * D_MODEL], preferred_element_type=F32)


def _qkv(x, w_qkv, tm):
    m = x.shape[0]
    row = pl.BlockSpec((tm, D_MODEL), lambda i: (i, 0))
    shp = jax.ShapeDtypeStruct((m, D_MODEL), F32)
    return pl.pallas_call(
        _qkv_kernel,
        grid=(m // tm,),
        in_specs=[row, _full((D_MODEL, 3 * D_MODEL))],
        out_specs=[row, row, row],
        out_shape=[shp, shp, shp],
        scratch_shapes=[pltpu.VMEM((D_MODEL, 3 * D_MODEL), BF16)],
        compiler_params=_params(1),
        name="qkv",
    )(x, w_qkv)


def _topk_mask(gate, n_valid, axis):
    nb = gate.shape[axis]
    pos_i = lax.broadcasted_iota(jnp.int32, gate.shape, axis)
    valid = pos_i < n_valid
    pos = pos_i.astype(F32)
    cur = jnp.where(valid, gate, -jnp.inf)
    sel = jnp.zeros(gate.shape, F32)
    for _ in range(min(MOBA_TOPK, nb)):
        best = jnp.max(cur, axis=axis, keepdims=True)
        first = jnp.min(jnp.where(cur == best, pos, float(nb)), axis=axis, keepdims=True)
        pick = jnp.logical_and(pos == first, valid)
        sel = jnp.where(pick, 1.0, sel)
        cur = jnp.where(pick, -jnp.inf, cur)
    return sel


MOBA_AUG_MASK0 = 8
MOBA_NEG = -30720.0


def _split3(x):
    hi = x.astype(BF16).astype(F32)
    mid = (x - hi).astype(BF16).astype(F32)
    lo = (x - hi - mid).astype(BF16).astype(F32)
    return hi, mid, lo


def _moba_prompt_kernel(q_ref, k_ref, v_ref, slope_ref, o_ref, qt_scr, vtp_scr, ka_scr, km_scr, s_scr):
    hp = pl.program_id(1)
    qt = pl.program_id(2)
    nb = km_scr.shape[0]
    blk = MOBA_BLOCK
    qw = 2 * blk
    dh = MOBA_DH
    pair = 2 * dh

    @pl.when(qt == 0)
    def _():
        lane = lax.broadcasted_iota(jnp.int32, (blk, pair), 1)
        key_off = lax.broadcasted_iota(jnp.int32, (blk, pair), 0).astype(F32)
        for n in range(nb):
            rows = slice(n * blk, (n + 1) * blk)
            half = slice((n % 2) * blk, (n % 2 + 1) * blk)
            qt_scr[n // 2, :, half] = q_ref[rows, :].T.astype(BF16)
            vtp_scr[n // 2, :, half] = v_ref[rows, :].T.astype(BF16)
            kblk = k_ref[rows, :]
            km_scr[n:n + 1, :] = jnp.mean(kblk, axis=0, keepdims=True)
            for hh in range(2):
                a = lane - (1 - hh) * dh
                aug = jnp.where(a < 3, key_off,
                                jnp.where(a < 6, float(n * blk),
                                          jnp.where(a == MOBA_AUG_MASK0 + n, 1.0, 0.0)))
                in_head = jnp.logical_and(lane >= hh * dh, lane < (hh + 1) * dh)
                ka_scr[hh, rows, :] = jnp.where(in_head, kblk, aug).astype(BF16)

    key_i = lax.broadcasted_iota(jnp.int32, (qw, qw), 0)
    qry_i = lax.broadcasted_iota(jnp.int32, (qw, qw), 1)
    same_block = (key_i >= blk) == (qry_i >= blk)
    causal = jnp.logical_and(same_block, key_i <= qry_i)
    first_key_second_qry = jnp.logical_and(key_i < blk, qry_i >= blk)
    col = lax.broadcasted_iota(jnp.int32, (1, qw), 1)
    cur_blk = 2 * qt + jnp.where(col >= blk, 1, 0)
    blk_row = lax.broadcasted_iota(jnp.int32, (nb, qw), 0)

    own = pl.ds(pl.multiple_of(qt * qw, qw), qw)
    q_t = qt_scr[qt].astype(F32)
    v_t_d = vtp_scr[qt]
    lane_k = lax.broadcasted_iota(jnp.int32, (nb, pair), 1)
    r8 = lax.broadcasted_iota(jnp.int32, (8, qw), 0)
    is_hi = jnp.logical_or(r8 == 0, r8 == 3)
    is_mid = jnp.logical_or(r8 == 1, r8 == 4)
    pad_rows = jnp.zeros((dh - MOBA_AUG_MASK0 - nb, qw), F32)

    q_aug, init = [], []
    for hh in range(2):
        hrows = slice(hh * dh, (hh + 1) * dh)
        s_hi, s_mid, s_lo = _split3(slope_ref[pl.ds(2 * hp + hh, 1), :])
        slope_rows = jnp.where(is_hi, s_hi, jnp.where(is_mid, s_mid, s_lo))
        slope_rows = jnp.where(r8 < 6, slope_rows, 0.0)

        def with_aug(aug_rows, hh=hh, hrows=hrows):
            parts = [q_t[hrows], aug_rows] if hh == 0 else [aug_rows, q_t[hrows]]
            return jnp.concatenate(parts, axis=0).astype(BF16)

        q_diag = with_aug(jnp.concatenate([slope_rows, jnp.zeros((dh - 8, qw), F32)], axis=0))
        in_head = jnp.logical_and(lane_k >= hh * dh, lane_k < (hh + 1) * dh)
        km_h = jnp.where(in_head, km_scr[...], 0.0).astype(BF16)
        gate_t = jnp.dot(km_h, q_diag, preferred_element_type=F32)
        sel = _topk_mask(gate_t, cur_blk, 0)
        mask_rows = jnp.where(sel > 0.0, 0.0, MOBA_NEG)
        q_aug.append(with_aug(jnp.concatenate([slope_rows, mask_rows, pad_rows], axis=0)))

        first_sel = jnp.sum(jnp.where(blk_row == 2 * qt, sel, 0.0), axis=0, keepdims=True)
        allowed = jnp.logical_or(causal, jnp.logical_and(first_key_second_qry, first_sel > 0.0))
        s = jnp.dot(ka_scr[hh, own, :], q_diag, preferred_element_type=F32)
        s = jnp.where(allowed, s, -jnp.inf)
        m0 = jnp.max(s, axis=0, keepdims=True)
        p = jnp.exp(s - m0)
        l0 = jnp.sum(p, axis=0, keepdims=True)
        acc0 = jnp.dot(v_t_d[hrows, :], p.astype(BF16), preferred_element_type=F32)
        init += [m0, l0, acc0]

    def scores(pair_idx, slot):
        keys = pl.ds(pl.multiple_of(pair_idx * qw, qw), qw)
        col_max = []
        for hh in range(2):
            s = jnp.dot(ka_scr[hh, keys, :], q_aug[hh], preferred_element_type=F32)
            s_scr[slot, hh] = s
            col_max.append(jnp.max(s, axis=0, keepdims=True))
        return col_max

    n_pairs = qt

    def body(jj, carry):
        slot = jj % 2
        v_pair = vtp_scr[jj]
        out = []
        for hh in range(2):
            m, l, acc, mx = carry[4 * hh:4 * hh + 4]
            m_new = jnp.maximum(m, mx)
            alpha = jnp.exp(m - m_new)
            p = jnp.exp(s_scr[slot, hh] - m_new)
            l = alpha * l + jnp.sum(p, axis=0, keepdims=True)
            acc = alpha * acc + jnp.dot(v_pair[hh * dh:(hh + 1) * dh, :], p.astype(BF16),
                                        preferred_element_type=F32)
            out.append([m_new, l, acc])
        nxt = scores(jnp.minimum(jj + 1, nb // 2 - 1), 1 - slot)
        return tuple(out[0] + [nxt[0]] + out[1] + [nxt[1]])

    mx0 = scores(0, 0)
    res = lax.fori_loop(0, n_pairs, body, tuple(init[0:3] + [mx0[0]] + init[3:6] + [mx0[1]]))
    o_ref[...] = jnp.concatenate([res[2] / res[1], res[6] / res[5]], axis=0).T


def _moba_prompt(q, k, v, slopes, batch, seq):
    m = q.shape[0]
    nb = seq // MOBA_BLOCK
    assert nb % 2 == 0 and MOBA_AUG_MASK0 + nb <= MOBA_DH
    pair = 2 * MOBA_DH
    n_pairs = MOBA_HEADS // 2
    qw = 2 * MOBA_BLOCK
    nt = nb // 2
    slope_rows = jnp.broadcast_to(slopes[:, None], (MOBA_HEADS, qw))
    seq_spec = pl.BlockSpec((seq, pair), lambda b, hp, qt: (b, hp))
    return pl.pallas_call(
        _moba_prompt_kernel,
        grid=(batch, n_pairs, nt),
        in_specs=[seq_spec, seq_spec, seq_spec, _full((MOBA_HEADS, qw))],
        out_specs=pl.BlockSpec((qw, pair), lambda b, hp, qt: (b * nt + qt, hp)),
        out_shape=jax.ShapeDtypeStruct((m, D_MODEL), F32),
        scratch_shapes=[pltpu.VMEM((nt, pair, qw), BF16), pltpu.VMEM((nt, pair, qw), BF16),
                        pltpu.VMEM((2, seq, pair), BF16), pltpu.VMEM((nb, pair), F32),
                        pltpu.VMEM((2, 2, qw, qw), F32)],
        compiler_params=_params(3),
        name="moba_prompt",
    )(q, k, v, slope_rows)


def _moba_sample_kernel(n_pages, pos0, pt_ref, q_ref, kn_ref, vn_ref, slope_ref, *refs):
    del pt_ref
    k_pages = refs[:n_pages]
    v_pages = refs[n_pages:2 * n_pages]
    o_ref = refs[2 * n_pages]
    per_blk = MOBA_BLOCK // PAGE_SIZE
    n_blk = n_pages // per_blk
    h_i = lax.broadcasted_iota(jnp.int32, (MOBA_HEADS, D_MODEL), 0)
    d_i = lax.broadcasted_iota(jnp.int32, (MOBA_HEADS, D_MODEL), 1)
    own_head = d_i // MOBA_DH == h_i
    q_bd = jnp.where(own_head, q_ref[0], 0.0)
    q_bd16 = q_bd.astype(BF16)
    slope = slope_ref[...]

    lane_n = lax.broadcasted_iota(jnp.int32, (D_MODEL, n_blk), 1)
    means = jnp.zeros((D_MODEL, n_blk), F32)
    for n in range(n_blk):
        tot = jnp.sum(sum(k_pages[n * per_blk + i][...] for i in range(per_blk)), axis=1, keepdims=True)
        means = jnp.where(lane_n == n, tot / float(MOBA_BLOCK), means)
    gate = jnp.dot(q_bd16, means.astype(BF16), preferred_element_type=F32)
    sel = _topk_mask(gate, n_blk, 1)

    lane = lax.broadcasted_iota(jnp.int32, (1, PAGE_SIZE), 1)
    scores = []
    for pg in range(n_pages):
        s = jnp.dot(q_bd16, k_pages[pg][...].astype(BF16), preferred_element_type=F32)
        dist = (pos0 - pg * PAGE_SIZE - lane).astype(F32)
        s = s - slope * dist
        n = pg // per_blk
        scores.append(jnp.where(sel[:, n:n + 1] > 0.0, s, -jnp.inf))
    k_new = kn_ref[0].astype(BF16).astype(F32)
    s_new = jnp.sum(q_bd16.astype(F32) * k_new, axis=1, keepdims=True)
    m = s_new
    for s in scores:
        m = jnp.maximum(m, jnp.max(s, axis=1, keepdims=True))
    p_new = jnp.exp(s_new - m)
    l = p_new
    d_e = lax.broadcasted_iota(jnp.int32, (D_MODEL, MOBA_HEADS), 0)
    h_e = lax.broadcasted_iota(jnp.int32, (D_MODEL, MOBA_HEADS), 1)
    expand = jnp.where(d_e // MOBA_DH == h_e, 1.0, 0.0).astype(BF16)
    acc = jnp.zeros((D_MODEL, PAGE_SIZE), F32)
    for pg in range(n_pages):
        p = jnp.exp(scores[pg] - m)
        l = l + jnp.sum(p, axis=1, keepdims=True)
        acc = acc + jnp.dot(expand, p.astype(BF16), preferred_element_type=F32) * v_pages[pg][...]
    o_past = jnp.sum(acc.T, axis=0, keepdims=True)
    row_of = lambda col: jnp.sum(jnp.where(own_head, col, 0.0), axis=0, keepdims=True)
    o_new = row_of(p_new.astype(BF16).astype(F32)) * vn_ref[0]
    o_ref[0] = (o_past + o_new) / row_of(l)


def _moba_sample(q, k_new, v_new, cache_k, cache_v, page_table, slopes, pos0):
    m = q.shape[0]
    n_pages = page_table.shape[1]
    assert pos0 == n_pages * PAGE_SIZE and pos0 % MOBA_BLOCK == 0
    n_phys = cache_k.shape[0]
    ck = jnp.transpose(cache_k, (0, 2, 3, 1)).reshape(n_phys, D_MODEL, PAGE_SIZE)
    cv = jnp.transpose(cache_v, (0, 2, 3, 1)).reshape(n_phys, D_MODEL, PAGE_SIZE)
    vec = pl.BlockSpec((1, 1, D_MODEL), lambda b, pt: (b, 0, 0))
    page_specs = [pl.BlockSpec((None, D_MODEL, PAGE_SIZE), lambda b, pt, pg=pg: (pt[b, pg], 0, 0))
                  for pg in range(n_pages)]
    grid_spec = pltpu.PrefetchScalarGridSpec(
        num_scalar_prefetch=1,
        grid=(m,),
        in_specs=[vec, vec, vec, pl.BlockSpec((MOBA_HEADS, 1), lambda b, pt: (0, 0))] + page_specs + page_specs,
        out_specs=vec,
    )
    out = pl.pallas_call(
        functools.partial(_moba_sample_kernel, n_pages, pos0),
        grid_spec=grid_spec,
        out_shape=jax.ShapeDtypeStruct((m, 1, D_MODEL), F32),
        compiler_params=_params(1),
        name="moba_sample",
    )(page_table, q.reshape(m, 1, D_MODEL), k_new.reshape(m, 1, D_MODEL), v_new.reshape(m, 1, D_MODEL),
      slopes.reshape(MOBA_HEADS, 1), *([ck] * n_pages), *([cv] * n_pages))
    return out.reshape(m, D_MODEL)


def _layer_tail(layer, xp, xs, mix_p, mix_s, p_p, p_s, w_out, ln_g, ln_b, w_router, b_router,
                w_gu, b_gu, w_down, b_down, w_pg, w_pp):
    n_p, n_s = xp.shape[0], xs.shape[0]
    lg0, lb0 = ln_g[0:1], ln_b[0:1]
    lg1, lb1 = ln_g[1:2], ln_b[1:2]
    br = b_router.reshape(1, N_EXPERTS)
    x1, idxp, gatep, cnt_p = _mix(xp, mix_p, w_out, lg0, lb0, w_router, br, ROW_TILE, n_total=n_p + n_s)
    x1, idxs, gates, cnt_s = _mix(xs, mix_s, w_out, lg0, lb0, w_router, br, n_s, x1_buf=x1)

    idx = jnp.concatenate([idxp, idxs], axis=0)
    counts = (cnt_p + cnt_s)[0].astype(jnp.int32)
    bm = MOE_BLOCK
    n_tok = n_p + n_s
    n_blocks = (n_tok * TOP_K + N_EXPERTS * (bm - 1)) // bm
    padded = (counts + bm - 1) // bm * bm
    pend = jnp.cumsum(padded)
    pstart = pend - padded
    dest = _rank(idx, pstart)
    n_used = (pend[-1] // bm).astype(jnp.int32)
    blk_ids = jnp.minimum(jnp.arange(n_blocks, dtype=jnp.int32), n_used - 1)
    block_e = jnp.sum((pend[None, :] <= (blk_ids * bm)[:, None]).astype(jnp.int32), axis=1)
    block_e = jnp.minimum(block_e, N_EXPERTS - 1)
    nk = n_tok * TOP_K
    low_bits = (nk - 1).bit_length()
    assert (N_EXPERTS << low_bits) < 2 ** 31
    order = jnp.sort((idx.reshape(-1) << low_bits) + jnp.arange(nk, dtype=jnp.int32)) & ((1 << low_bits) - 1)
    start = jnp.cumsum(counts) - counts
    r_in_group = (blk_ids * bm - pstart[block_e])[:, None] + jnp.arange(bm, dtype=jnp.int32)[None, :]
    live = r_in_group < counts[block_e][:, None]
    src = jnp.where(live, start[block_e][:, None] + r_in_group, 0)
    row_tok = jnp.where(live, order.at[src].get(mode="promise_in_bounds") // TOP_K, 0).reshape(-1)
    gather = lambda src, rows: src.at[rows].get(mode="promise_in_bounds")
    xs_rows = gather(x1, row_tok)
    ys = _experts(layer, block_e, n_used.reshape(1), xs_rows, w_gu, b_gu, w_down, b_down)
    ysg_p = gather(ys, dest[:n_p].T.reshape(-1))
    ysg_s = gather(ys, dest[n_p:].T.reshape(-1))
    yp = _finish(x1, 0, n_p, ysg_p, gatep, p_p, layer, lg1, lb1, w_pg, w_pp, ROW_TILE)
    ys_out = _finish(x1, n_p, n_s, ysg_s, gates, p_s, layer, lg1, lb1, w_pg, w_pp, n_s)
    return yp, ys_out


def kernel(x_prompt, x_sample, state_gla, state_pool, cache_k, cache_v, page_table, p_prompt, p_sample, w_in_ab, gla_w_alpha, gla_b_alpha, gla_norm_g, pool_w, pool_scale, w_out_ab, w_qkv_c, w_out_c, ln_g, ln_b, moe_w_router, moe_b_router, moe_w_gu, moe_b_gu, moe_w_down, moe_b_down, ple_w_gate, ple_w_proj):
    batch, seq, _ = x_prompt.shape
    n_s = x_sample.shape[0]
    n_p = batch * seq
    pos0 = page_table.shape[1] * PAGE_SIZE
    xp = x_prompt.reshape(n_p, D_MODEL)
    xs = x_sample.reshape(n_s, D_MODEL)

    def tail(i, xp, xs, mix_p, mix_s, w_out):
        depth = p_prompt.shape[0]
        return _layer_tail(i, xp, xs, mix_p, mix_s, p_prompt.reshape(depth, n_p, -1), p_sample.reshape(depth, n_s, -1),
                           w_out, ln_g[i], ln_b[i], moe_w_router[i], moe_b_router[i], moe_w_gu,
                           moe_b_gu, moe_w_down, moe_b_down, ple_w_gate[i], ple_w_proj[i])

    w_in = w_in_ab[0]
    c_a = 2 * QK_A + 2 * V_A
    w_main = jnp.concatenate([w_in[:, :c_a], w_in[:, c_a + GLA_LOWRANK:]], axis=1)
    w_a = w_in[:, c_a:c_a + GLA_LOWRANK]
    b_alpha = gla_b_alpha[0].reshape(1, QK_A)
    norm_g = gla_norm_g[0].reshape(1, GLA_DV)
    scale = pool_scale[0].reshape(1, POOL_CH)
    qkl_p, v_p, sg_p, u_p = _proj_ab(xp, w_main, w_a, gla_w_alpha[0], b_alpha, ROW_TILE)
    qkl_s, v_s, sg_s, u_s = _proj_ab(xs, w_main, w_a, gla_w_alpha[0], b_alpha, n_s)
    mix_p, gla_p = _gla_prompt(qkl_p, v_p, sg_p, norm_g, batch, seq, ROW_TILE)
    mix_p = _pool_prompt(u_p, mix_p, pool_w[0], scale, batch, seq, ROW_TILE)
    mix_s, gla_s = _gla_sample(qkl_s, v_s, sg_s, norm_g, state_gla[0].reshape(n_s, QK_A, GLA_DV))
    mix_s, pool_s = _pool_sample(u_s, state_pool[0], mix_s, pool_w[0], scale, pos0)
    pool_p = u_p.reshape(batch, seq, POOL_CH)[:, seq - POOL_HIST:]
    xp, xs = tail(0, xp, xs, mix_p, mix_s, w_out_ab[0])

    slopes = jnp.exp2(-8.0 * jnp.arange(1, MOBA_HEADS + 1, dtype=F32) / MOBA_HEADS)
    q_p, k_p, v_p2 = _qkv(xp, w_qkv_c[0], ROW_TILE)
    q_s, k_s, v_s2 = _qkv(xs, w_qkv_c[0], n_s)
    o_p = _moba_prompt(q_p, k_p, v_p2, slopes, batch, seq)
    o_s = _moba_sample(q_s, k_s, v_s2, cache_k[0], cache_v[0], page_table, slopes, pos0)
    xp, xs = tail(1, xp, xs, o_p, o_s, w_out_c[0])

    hd = (MOBA_HEADS, MOBA_DH)
    return (xp.reshape(batch, seq, D_MODEL), xs.reshape(n_s, 1, D_MODEL),
            gla_p.reshape(1, batch, GLA_HEADS, GLA_DK, GLA_DV), gla_s.reshape(1, n_s, GLA_HEADS, GLA_DK, GLA_DV),
            pool_p[None], pool_s[None],
            k_p.reshape(1, batch, seq, *hd), v_p2.reshape(1, batch, seq, *hd),
            k_s.reshape(1, n_s, 1, *hd), v_s2.reshape(1, n_s, 1, *hd))
```

```python
import functools
import math

import numpy as np
import jax
import jax.numpy as jnp
from jax import lax
from jax.experimental import pallas as pl
from jax.experimental.pallas import tpu as pltpu

F32 = jnp.float32
BF16 = jnp.bfloat16
HI = lax.Precision.HIGHEST

D_MODEL = 1024
GLA_HEADS = 4
GLA_DK = 64
GLA_DV = 128
GLA_LOWRANK = 16
GLA_TAU = 16.0
QK_A = GLA_HEADS * GLA_DK
V_A = GLA_HEADS * GLA_DV
POOL_WINDOWS = (2, 4, 8, 16)
POOL_CH = 512
POOL_GROUP = 128
POOL_HIST = 15
MOBA_HEADS = 16
MOBA_DH = 64
MOBA_BLOCK = 256
MOBA_TOPK = 3
PAGE_SIZE = 128
N_EXPERTS = 32
TOP_K = 4
D_EXPERT = 1024
SWIGLU_LIMIT = 7.0
SWIGLU_ALPHA = 1.702
DEPTH = 2
DEEPNORM_ALPHA = (2 * DEPTH) ** 0.25
LN_EPS = 1e-5

ROW_TILE = 512
SUB_TILE = 256
GLA_CHUNK = 128
GLA_LEVELS = 7
MOE_BLOCK = 512
RANK_TILE = 384
VMEM_LIMIT = 56 * 1024 * 1024


def _params(n_axes, vmem=VMEM_LIMIT):
    return pltpu.CompilerParams(dimension_semantics=("arbitrary",) * n_axes, vmem_limit_bytes=vmem)


def _bdot(a, b):
    return jnp.dot(a.astype(BF16), b.astype(BF16), preferred_element_type=F32)


def _layer_norm(y, g, b):
    mu = jnp.mean(y, axis=-1, keepdims=True)
    yc = y - mu
    var = jnp.mean(yc * yc, axis=-1, keepdims=True)
    return yc * lax.rsqrt(var + LN_EPS) * g + b


def _full(shape):
    n = len(shape)
    return pl.BlockSpec(shape, lambda *_: (0,) * n)


def _proj_ab_kernel(x_ref, wm_ref, wa_ref, walpha_ref, balpha_ref,
                    qkl_ref, v_ref, sg_ref, u_ref, wm_bf, wa_bf):
    @pl.when(pl.program_id(0) == 0)
    def _():
        wm_bf[...] = wm_ref[...].astype(BF16)
        wa_bf[...] = wa_ref[...].astype(BF16)

    xb = x_ref[...].astype(BF16)
    qk = jnp.dot(xb, wm_bf[:, 0:2 * QK_A], preferred_element_type=F32)
    qkl_ref[:, 0:QK_A] = qk[:, 0:QK_A] * (GLA_DK ** -0.5)
    qkl_ref[:, QK_A:2 * QK_A] = qk[:, QK_A:2 * QK_A]
    a_lr = jnp.dot(xb, wa_bf[...], preferred_element_type=F32)
    z = _bdot(a_lr, walpha_ref[...]) + balpha_ref[...]
    log_sig = jnp.minimum(z, 0.0) - jnp.log1p(jnp.exp(-jnp.abs(z)))
    qkl_ref[:, 2 * QK_A:3 * QK_A] = log_sig / GLA_TAU
    c0 = 2 * QK_A
    v_ref[...] = jnp.dot(xb, wm_bf[:, c0:c0 + V_A], preferred_element_type=F32)
    g = jnp.dot(xb, wm_bf[:, c0 + V_A:c0 + 2 * V_A], preferred_element_type=F32)
    sg_ref[...] = g * jax.nn.sigmoid(g)
    u_ref[...] = jnp.dot(xb, wm_bf[:, c0 + 2 * V_A:c0 + 2 * V_A + POOL_CH], preferred_element_type=F32)


def _proj_ab(x, w_main, w_a, w_alpha, b_alpha, tm):
    m = x.shape[0]
    nmain = w_main.shape[1]
    row = lambda w: pl.BlockSpec((tm, w), lambda i: (i, 0))
    return pl.pallas_call(
        _proj_ab_kernel,
        grid=(m // tm,),
        in_specs=[row(D_MODEL), _full((D_MODEL, nmain)), _full((D_MODEL, GLA_LOWRANK)),
                  _full((GLA_LOWRANK, QK_A)), _full((1, QK_A))],
        out_specs=[row(3 * QK_A), row(V_A), row(V_A), row(POOL_CH)],
        out_shape=[jax.ShapeDtypeStruct((m, 3 * QK_A), F32), jax.ShapeDtypeStruct((m, V_A), F32),
                   jax.ShapeDtypeStruct((m, V_A), F32), jax.ShapeDtypeStruct((m, POOL_CH), F32)],
        scratch_shapes=[pltpu.VMEM((D_MODEL, nmain), BF16), pltpu.VMEM((D_MODEL, GLA_LOWRANK), BF16)],
        compiler_params=_params(1),
        name="proj_ab",
    )(x, w_main, w_a, w_alpha, b_alpha)


def _gla_tables():
    c = GLA_CHUNK
    i = np.arange(c)[:, None]
    s = np.arange(c)[None, :]
    mats = [(s <= i), (s > i)]
    for lev in range(GLA_LEVELS):
        p = GLA_LEVELS - 1 - lev
        half = 1 << p
        start = (i >> (p + 1)) << (p + 1)
        mid = start + half - 1
        upper = i >= start + half
        mats.append(np.where(upper, (s > mid) & (s <= i), (s > i) & (s <= mid)))
    seg = np.concatenate(mats, axis=0).astype(np.float32)
    j = np.arange(c)[None, :]
    x = i ^ j
    lvl = np.full((c, c), GLA_LEVELS + 1, np.int32)
    lvl[np.arange(c), np.arange(c)] = GLA_LEVELS
    for lev in range(GLA_LEVELS):
        p = GLA_LEVELS - 1 - lev
        lvl = np.where(((x >> p) == 1) & (((i >> p) & 1) == 1), lev, lvl)
    lvl4 = np.tile(lvl, (GLA_HEADS, 1)).astype(np.int32)
    lane_head = (np.arange(QK_A) // GLA_DK)[None, :]
    row_head = (np.arange(GLA_HEADS * c) // c)[:, None]
    hm4 = (lane_head == row_head).astype(np.float32)
    return seg, lvl4, hm4


def _gla_chunk(q, k, la, v, s_all, seg, lvl4, hm4):
    c = GLA_CHUNK
    e = sum(jnp.dot(seg, part.astype(BF16), preferred_element_type=F32) for part in _split3(la))
    w = jnp.exp(e)
    w_b = w[0:c]
    w_k = w[c:2 * c]

    def stack_heads(t):
        return (jnp.concatenate([t] * GLA_HEADS, axis=0) * hm4).astype(BF16)

    o_inter = jnp.dot(stack_heads(q * w_b), s_all.astype(BF16), preferred_element_type=F32)
    a = jnp.zeros((GLA_HEADS * c, c), F32)
    for lev in range(GLA_LEVELS + 1):
        if lev < GLA_LEVELS:
            w_l = w[(2 + lev) * c:(3 + lev) * c]
            ql, kl = q * w_l, k * w_l
        else:
            ql, kl = q, k
        p_l = lax.dot_general(stack_heads(ql), kl.astype(BF16), (((1,), (1,)), ((), ())),
                              preferred_element_type=F32)
        a = jnp.where(lvl4 == lev, p_l, a)
    a = a.astype(BF16)
    outs = []
    for h in range(GLA_HEADS):
        v_h = v[:, h * GLA_DV:(h + 1) * GLA_DV].astype(BF16)
        o_h = o_inter[h * c:(h + 1) * c] + jnp.dot(a[h * c:(h + 1) * c], v_h, preferred_element_type=F32)
        outs.append(o_h)
    ks_t = (k * w_k).T.astype(BF16)
    kv = jnp.dot(ks_t, v.astype(BF16), preferred_element_type=F32)
    dec = jnp.exp(jnp.sum(la.T, axis=1, keepdims=True))
    new_rows = []
    for h in range(GLA_HEADS):
        rows = slice(h * GLA_DK, (h + 1) * GLA_DK)
        new_rows.append(dec[rows] * s_all[rows] + kv[rows, h * GLA_DV:(h + 1) * GLA_DV])
    return outs, jnp.concatenate(new_rows, axis=0)


def _gla_finish(o_h, norm_g, sg_h):
    o_h = o_h * lax.rsqrt(jnp.mean(o_h * o_h, axis=-1, keepdims=True) + LN_EPS) * norm_g
    return o_h * sg_h


def _gla_prompt_kernel(qkl_ref, v_ref, sg_ref, ng_ref, seg_ref, lvl_ref, hm_ref,
                       mix_ref, state_ref, s_scr):
    t = pl.program_id(1)

    @pl.when(t == 0)
    def _():
        s_scr[...] = jnp.zeros_like(s_scr)

    seg = seg_ref[...]
    lvl4 = lvl_ref[...]
    hm4 = hm_ref[...]
    norm_g = ng_ref[...]
    n_chunks = qkl_ref.shape[0] // GLA_CHUNK

    s_all = s_scr[...]
    for ci in range(n_chunks):
        rows = pl.ds(ci * GLA_CHUNK, GLA_CHUNK)
        q = qkl_ref[rows, 0:QK_A]
        k = qkl_ref[rows, QK_A:2 * QK_A]
        la = qkl_ref[rows, 2 * QK_A:3 * QK_A]
        v = v_ref[rows, :]
        outs, s_all = _gla_chunk(q, k, la, v, s_all, seg, lvl4, hm4)
        if ci == n_chunks - 1:
            s_scr[...] = s_all
        for h in range(GLA_HEADS):
            cols = slice(h * GLA_DV, (h + 1) * GLA_DV)
            mix_ref[rows, cols] = _gla_finish(outs[h], norm_g, sg_ref[rows, cols])

    @pl.when(t == pl.num_programs(1) - 1)
    def _():
        state_ref[0] = s_scr[...]


def _gla_prompt(qkl, v, sg, norm_g, batch, seq, tm):
    m = qkl.shape[0]
    nt = seq // tm
    seg, lvl4, hm4 = _gla_tables()
    row = lambda w: pl.BlockSpec((tm, w), lambda b, t: (b * nt + t, 0))
    return pl.pallas_call(
        _gla_prompt_kernel,
        grid=(batch, nt),
        in_specs=[row(3 * QK_A), row(V_A), row(V_A), _full((1, GLA_DV)),
                  _full(seg.shape), _full(lvl4.shape), _full(hm4.shape)],
        out_specs=[pl.BlockSpec((tm, V_A), lambda b, t: (b * nt + t, 0)),
                   pl.BlockSpec((1, QK_A, GLA_DV), lambda b, t: (b, 0, 0))],
        out_shape=[jax.ShapeDtypeStruct((m, D_MODEL), F32),
                   jax.ShapeDtypeStruct((batch, QK_A, GLA_DV), F32)],
        scratch_shapes=[pltpu.VMEM((QK_A, GLA_DV), F32)],
        compiler_params=_params(2),
        name="gla_prompt",
    )(qkl, v, sg, norm_g, jnp.asarray(seg, BF16), jnp.asarray(lvl4), jnp.asarray(hm4))


GLA_SAMPLE_ROWS = 8


def _gla_sample_kernel(qkl_ref, v_ref, sg_ref, ng_ref, s0_ref, mix_ref, s1_ref):
    ones = jnp.ones((QK_A, GLA_DV), F32)
    r_i = lax.broadcasted_iota(jnp.int32, (QK_A, QK_A), 0)
    c_i = lax.broadcasted_iota(jnp.int32, (QK_A, QK_A), 1)
    eye = r_i == c_i
    norm_g = ng_ref[...]

    def col_bcast(row):
        diag = jnp.where(eye, jnp.broadcast_to(row, (QK_A, QK_A)), 0.0)
        return jnp.dot(diag, ones, precision=HI, preferred_element_type=F32)

    for r in range(GLA_SAMPLE_ROWS):
        q = col_bcast(qkl_ref[r:r + 1, 0:QK_A])
        k = col_bcast(qkl_ref[r:r + 1, QK_A:2 * QK_A])
        dec = jnp.exp(col_bcast(qkl_ref[r:r + 1, 2 * QK_A:3 * QK_A]))
        for h in range(GLA_HEADS):
            rows = slice(h * GLA_DK, (h + 1) * GLA_DK)
            cols = slice(h * GLA_DV, (h + 1) * GLA_DV)
            s_new = dec[rows] * s0_ref[r, rows, :] + k[rows] * v_ref[r:r + 1, cols]
            s1_ref[r, rows, :] = s_new
            o_h = jnp.sum(q[rows] * s_new, axis=0, keepdims=True)
            mix_ref[r:r + 1, cols] = _gla_finish(o_h, norm_g, sg_ref[r:r + 1, cols])


def _gla_sample(qkl, v, sg, norm_g, s0):
    m = qkl.shape[0]
    rb = GLA_SAMPLE_ROWS
    row = lambda w: pl.BlockSpec((rb, w), lambda i: (i, 0))
    st = pl.BlockSpec((rb, QK_A, GLA_DV), lambda i: (i, 0, 0))
    return pl.pallas_call(
        _gla_sample_kernel,
        grid=(m // rb,),
        in_specs=[row(3 * QK_A), row(V_A), row(V_A), _full((1, GLA_DV)), st],
        out_specs=[pl.BlockSpec((rb, V_A), lambda i: (i, 0)), st],
        out_shape=[jax.ShapeDtypeStruct((m, D_MODEL), F32),
                   jax.ShapeDtypeStruct((m, QK_A, GLA_DV), F32)],
        compiler_params=_params(1),
        name="gla_sample",
    )(qkl, v, sg, norm_g, s0)


def _pool_project(d_groups, wp_ref, scale_ref, mix_ref):
    for g in range(len(POOL_WINDOWS)):
        cols = slice(g * POOL_GROUP, (g + 1) * POOL_GROUP)
        mix_ref[:, cols] = _bdot(d_groups[g], wp_ref[g]) * scale_ref[:, cols]


def _pool_prompt_kernel(u_ref, prev_ref, wp_ref, scale_ref, mixin_ref, mix_ref):
    del mixin_ref
    t = pl.program_id(1)
    tm = u_ref.shape[0]
    hist = prev_ref.shape[0]
    prev = jnp.where(t > 0, prev_ref[...], 0.0)
    z = jnp.concatenate([prev, u_ref[...]], axis=0)
    pos = t * tm + lax.broadcasted_iota(jnp.int32, (tm, 1), 0)
    d_groups = []
    for g, w in enumerate(POOL_WINDOWS):
        cols = slice(g * POOL_GROUP, (g + 1) * POOL_GROUP)
        s = z[:, cols]
        shift = 1
        while shift < w:
            s = s + pltpu.roll(s, shift, 0)
            shift *= 2
        cnt = jnp.minimum(w, pos + 1).astype(F32)
        d_groups.append(s[hist:] / cnt - z[hist:, cols])
    _pool_project(d_groups, wp_ref, scale_ref, mix_ref)


def _pool_prompt(u, mix, w_pool, scale, batch, seq, tm):
    nt = seq // tm
    hist = 16
    assert hist > POOL_HIST and tm % hist == 0
    per = tm // hist
    return pl.pallas_call(
        _pool_prompt_kernel,
        grid=(batch, nt),
        in_specs=[pl.BlockSpec((tm, POOL_CH), lambda b, t: (b * nt + t, 0)),
                  pl.BlockSpec((hist, POOL_CH), lambda b, t: (jnp.maximum((b * nt + t) * per - 1, 0), 0)),
                  _full(w_pool.shape), _full((1, POOL_CH)),
                  pl.BlockSpec(memory_space=pl.ANY)],
        out_specs=pl.BlockSpec((tm, POOL_CH), lambda b, t: (b * nt + t, 1)),
        out_shape=jax.ShapeDtypeStruct(mix.shape, F32),
        input_output_aliases={4: 0},
        compiler_params=_params(2),
        name="pool_prompt",
    )(u, u, w_pool, scale, mix)


def _pool_sample_kernel(u_ref, st_ref, wp_ref, scale_ref, mixin_ref, mix_ref, st_out_ref):
    del mixin_ref
    u = u_ref[...]
    d_groups = []
    for g, w in enumerate(POOL_WINDOWS):
        cols = slice(g * POOL_GROUP, (g + 1) * POOL_GROUP)
        win = u[:, cols] + jnp.sum(st_ref[:, POOL_HIST - (w - 1):POOL_HIST, cols], axis=1)
        d_groups.append(win / float(w) - u[:, cols])
    _pool_project(d_groups, wp_ref, scale_ref, mix_ref)
    st_out_ref[:, 0:POOL_HIST - 1, :] = st_ref[:, 1:POOL_HIST, :]
    st_out_ref[:, POOL_HIST - 1:POOL_HIST, :] = u[:, None, :]


def _pool_sample(u, st, mix, w_pool, scale, pos0):
    m = u.shape[0]
    assert pos0 + 1 >= max(POOL_WINDOWS)
    return pl.pallas_call(
        _pool_sample_kernel,
        grid=(1,),
        in_specs=[_full((m, POOL_CH)), _full(st.shape), _full(w_pool.shape), _full((1, POOL_CH)),
                  pl.BlockSpec(memory_space=pl.ANY)],
        out_specs=[pl.BlockSpec((m, POOL_CH), lambda i: (0, 1)), _full(st.shape)],
        out_shape=[jax.ShapeDtypeStruct(mix.shape, F32), jax.ShapeDtypeStruct(st.shape, F32)],
        input_output_aliases={4: 0},
        compiler_params=_params(1),
        name="pool_sample",
    )(u, st, w_pool, scale, mix)


def _sub_tiles(tm):
    ts = SUB_TILE if tm % SUB_TILE == 0 else tm
    return [slice(s, s + ts) for s in range(0, tm, ts)]


def _mix_kernel(x_ref, mix_ref, w_ref, lng_ref, lnb_ref, wr_ref, br_ref, *rest):
    x1_ref, idx_ref, gate_ref, cnt_ref, w_bf = rest[-5:]

    @pl.when(pl.program_id(0) == 0)
    def _():
        w_bf[...] = w_ref[...].astype(BF16)
        cnt_ref[...] = jnp.zeros_like(cnt_ref)

    for rows in _sub_tiles(x_ref.shape[0]):
        h = jnp.dot(mix_ref[rows, :].astype(BF16), w_bf[...], preferred_element_type=F32)
        x1 = _layer_norm(DEEPNORM_ALPHA * x_ref[rows, :] + h, lng_ref[...], lnb_ref[...])
        x1_ref[rows, :] = x1
        logits = _bdot(x1, wr_ref[...]) + br_ref[...]
        ts = logits.shape[0]
        lane = lax.broadcasted_iota(jnp.int32, (ts, N_EXPERTS), 1)
        lane_k = lax.broadcasted_iota(jnp.int32, (ts, TOP_K), 1)
        idx_out = jnp.zeros((ts, TOP_K), jnp.int32)
        val_out = jnp.zeros((ts, TOP_K), F32)
        cur = logits
        chosen = jnp.zeros((ts, N_EXPERTS), F32)
        for kk in range(TOP_K):
            mval = jnp.max(cur, axis=1, keepdims=True)
            midx = jnp.min(jnp.where(cur == mval, lane.astype(F32), float(N_EXPERTS)), axis=1,
                           keepdims=True).astype(jnp.int32)
            idx_out = jnp.where(lane_k == kk, midx, idx_out)
            val_out = jnp.where(lane_k == kk, mval, val_out)
            picked = lane == midx
            cur = jnp.where(picked, -jnp.inf, cur)
            chosen = jnp.where(picked, 1.0, chosen)
        cnt_ref[...] += jnp.sum(chosen, axis=0, keepdims=True)
        ex = jnp.exp(val_out - val_out[:, 0:1])
        idx_ref[rows, :] = idx_out
        gate_ref[rows, :] = ex / jnp.sum(ex, axis=1, keepdims=True)


def _mix(x, mix, w_out, ln_g, ln_b, w_router, b_router, tm, x1_buf=None, n_total=None):
    m = x.shape[0]
    n_total = n_total if x1_buf is None else x1_buf.shape[0]
    off_blocks = 0 if x1_buf is None else (n_total - m) // tm
    row = lambda w: pl.BlockSpec((tm, w), lambda i: (i, 0))
    in_specs = [row(D_MODEL), row(D_MODEL), _full((D_MODEL, D_MODEL)), _full((1, D_MODEL)),
                _full((1, D_MODEL)), _full((D_MODEL, N_EXPERTS)), _full((1, N_EXPERTS))]
    args = [x, mix, w_out, ln_g, ln_b, w_router, b_router]
    aliases = {}
    if x1_buf is not None:
        assert (n_total - m) % tm == 0
        in_specs.append(pl.BlockSpec(memory_space=pl.ANY))
        args.append(x1_buf)
        aliases = {len(args) - 1: 0}
    return pl.pallas_call(
        _mix_kernel,
        grid=(m // tm,),
        in_specs=in_specs,
        out_specs=[pl.BlockSpec((tm, D_MODEL), lambda i: (i + off_blocks, 0)), row(TOP_K), row(TOP_K),
                   _full((1, N_EXPERTS))],
        out_shape=[jax.ShapeDtypeStruct((n_total, D_MODEL), F32),
                   jax.ShapeDtypeStruct((m, TOP_K), jnp.int32), jax.ShapeDtypeStruct((m, TOP_K), F32),
                   jax.ShapeDtypeStruct((1, N_EXPERTS), F32)],
        scratch_shapes=[pltpu.VMEM((D_MODEL, D_MODEL), BF16)],
        input_output_aliases=aliases,
        compiler_params=_params(1),
        name="mix",
    )(*args)


def _rank_kernel(idx_ref, pstart_ref, rank_ref, carry):
    i = pl.program_id(0)

    @pl.when(i == 0)
    def _():
        carry[...] = pstart_ref[...].astype(F32)

    idx = idx_ref[...]
    tm = idx.shape[0]
    lane = lax.broadcasted_iota(jnp.int32, (tm, N_EXPERTS), 1)
    onehots = [(idx[:, kk:kk + 1] == lane) for kk in range(TOP_K)]
    member = sum(jnp.where(o, 1.0, 0.0) for o in onehots)
    r_i = lax.broadcasted_iota(jnp.int32, (tm, tm), 0)
    c_i = lax.broadcasted_iota(jnp.int32, (tm, tm), 1)
    strict_lower = jnp.where(c_i < r_i, 1.0, 0.0).astype(BF16)
    before = jnp.dot(strict_lower, member.astype(BF16), preferred_element_type=F32) + carry[...]
    lane_k = lax.broadcasted_iota(jnp.int32, (tm, TOP_K), 1)
    rank = jnp.zeros((tm, TOP_K), F32)
    for kk in range(TOP_K):
        r_k = jnp.sum(jnp.where(onehots[kk], before, 0.0), axis=1, keepdims=True)
        rank = jnp.where(lane_k == kk, r_k, rank)
    rank_ref[...] = rank.astype(jnp.int32)
    carry[...] = carry[...] + jnp.sum(member, axis=0, keepdims=True)


def _rank(idx, pstart):
    n = idx.shape[0]
    tm = RANK_TILE
    return pl.pallas_call(
        _rank_kernel,
        grid=(n // tm,),
        in_specs=[pl.BlockSpec((tm, TOP_K), lambda i: (i, 0)), _full((1, N_EXPERTS))],
        out_specs=pl.BlockSpec((tm, TOP_K), lambda i: (i, 0)),
        out_shape=jax.ShapeDtypeStruct((n, TOP_K), jnp.int32),
        scratch_shapes=[pltpu.VMEM((1, N_EXPERTS), F32)],
        compiler_params=_params(1),
        name="moe_rank",
    )(idx, pstart.reshape(1, N_EXPERTS))


def _expert_kernel(be_ref, nu_ref, xs_ref, wgu_ref, bgu_ref, wd_ref, bd_ref, *rest):
    ys_ref, wgu_bf, wd_bf = rest[-3:]
    blk = pl.program_id(0)
    prev = be_ref[jnp.maximum(blk - 1, 0)]
    fresh = jnp.logical_or(blk == 0, be_ref[blk] != prev)
    used = blk < nu_ref[0]

    @pl.when(jnp.logical_and(fresh, used))
    def _():
        wgu_bf[...] = wgu_ref[0, 0].astype(BF16)
        wd_bf[...] = wd_ref[0, 0].astype(BF16)

    @pl.when(used)
    def _():
        xb = xs_ref[...].astype(BF16)
        acc = jnp.zeros(ys_ref.shape, F32) + bd_ref[0, 0]
        half = D_EXPERT // 2
        for c in range(2):
            cg = slice(c * half, (c + 1) * half)
            cu = slice(D_EXPERT + c * half, D_EXPERT + (c + 1) * half)
            gate = jnp.dot(xb, wgu_bf[:, cg], preferred_element_type=F32) + bgu_ref[0, 0, :, cg]
            up = jnp.dot(xb, wgu_bf[:, cu], preferred_element_type=F32) + bgu_ref[0, 0, :, cu]
            gate = jnp.minimum(gate, SWIGLU_LIMIT)
            up = jnp.clip(up, -SWIGLU_LIMIT, SWIGLU_LIMIT)
            act = (up + 1.0) * gate * jax.nn.sigmoid(SWIGLU_ALPHA * gate)
            acc = acc + jnp.dot(act.astype(BF16), wd_bf[cg, :], preferred_element_type=F32)
        ys_ref[...] = acc

    @pl.when(jnp.logical_not(used))
    def _():
        ys_ref[...] = jnp.zeros_like(ys_ref)


def _experts(layer, block_e, n_used, xs, w_gu, b_gu, w_down, b_down, blk0, p_total, ys_buf=None):
    bm = MOE_BLOCK
    nb = xs.shape[0] // bm
    depth = w_gu.shape[0]
    last = lambda blk, nu: jnp.maximum(jnp.minimum(blk, nu[0] - 1), 0)
    in_specs = [pl.BlockSpec((bm, D_MODEL), lambda blk, be, nu: (last(blk, nu), 0)),
                pl.BlockSpec((1, 1, D_MODEL, 2 * D_EXPERT), lambda blk, be, nu: (layer, be[blk], 0, 0)),
                pl.BlockSpec((1, 1, 1, 2 * D_EXPERT), lambda blk, be, nu: (layer, be[blk], 0, 0)),
                pl.BlockSpec((1, 1, D_EXPERT, D_MODEL), lambda blk, be, nu: (layer, be[blk], 0, 0)),
                pl.BlockSpec((1, 1, 1, D_MODEL), lambda blk, be, nu: (layer, be[blk], 0, 0))]
    args = [block_e, n_used, xs, w_gu, b_gu.reshape(depth, N_EXPERTS, 1, 2 * D_EXPERT), w_down,
            b_down.reshape(depth, N_EXPERTS, 1, D_MODEL)]
    aliases = {}
    if ys_buf is not None:
        in_specs.append(pl.BlockSpec(memory_space=pl.ANY))
        args.append(ys_buf)
        aliases = {len(args) - 1: 0}
    grid_spec = pltpu.PrefetchScalarGridSpec(
        num_scalar_prefetch=2,
        grid=(nb,),
        in_specs=in_specs,
        out_specs=pl.BlockSpec((bm, D_MODEL), lambda blk, be, nu: (blk + blk0, 0)),
        scratch_shapes=[pltpu.VMEM((D_MODEL, 2 * D_EXPERT), BF16), pltpu.VMEM((D_EXPERT, D_MODEL), BF16)],
    )
    return pl.pallas_call(
        _expert_kernel,
        grid_spec=grid_spec,
        out_shape=jax.ShapeDtypeStruct((p_total, D_MODEL), F32),
        input_output_aliases=aliases,
        compiler_params=_params(1),
        name="moe_experts",
    )(*args)


def _finish_kernel(x1_ref, y0_ref, y1_ref, y2_ref, y3_ref, gate_ref, p_ref, lng_ref, lnb_ref,
                   wpg_ref, wpp_ref, out_ref, wpg_bf):
    @pl.when(pl.program_id(0) == 0)
    def _():
        wpg_bf[...] = wpg_ref[...].astype(BF16)

    for rows in _sub_tiles(out_ref.shape[0]):
        gates = gate_ref[rows, :]
        moe = y0_ref[rows, :] * gates[:, 0:1]
        for kk, y_ref in enumerate((y1_ref, y2_ref, y3_ref), start=1):
            moe = moe + y_ref[rows, :] * gates[:, kk:kk + 1]
        x2 = _layer_norm(DEEPNORM_ALPHA * x1_ref[rows, :] + moe, lng_ref[...], lnb_ref[...])
        pg = jax.nn.sigmoid(jnp.dot(x2.astype(BF16), wpg_bf[...], preferred_element_type=F32))
        pp = _bdot(p_ref[rows, :], wpp_ref[...])
        out_ref[rows, :] = x2 + pg * pp


def _finish(x1_all, row0, m, ysg, gates, p_all, layer, ln_g, ln_b, w_pg, w_pp, tm):
    nt = m // tm
    assert row0 % tm == 0
    row = lambda w: pl.BlockSpec((tm, w), lambda i: (i, 0))
    ple = p_all.shape[2]
    y_specs = [pl.BlockSpec((tm, D_MODEL), lambda i, kk=kk: (kk * nt + i, 0)) for kk in range(TOP_K)]
    x1_spec = pl.BlockSpec((tm, D_MODEL), lambda i: (i + row0 // tm, 0))
    p_spec = pl.BlockSpec((None, tm, ple), lambda i: (layer, i, 0))
    return pl.pallas_call(
        _finish_kernel,
        grid=(nt,),
        in_specs=[x1_spec] + y_specs + [row(TOP_K), p_spec, _full((1, D_MODEL)), _full((1, D_MODEL)),
                                        _full((D_MODEL, D_MODEL)), _full((ple, D_MODEL))],
        out_specs=row(D_MODEL),
        out_shape=jax.ShapeDtypeStruct((m, D_MODEL), F32),
        scratch_shapes=[pltpu.VMEM((D_MODEL, D_MODEL), BF16)],
        compiler_params=_params(1),
        name="finish",
    )(x1_all, ysg, ysg, ysg, ysg, gates, p_all, ln_g, ln_b, w_pg, w_pp)


def _qkv_kernel(x_ref, w_ref, q_ref, k_ref, v_ref, w_bf):
    @pl.when(pl.program_id(0) == 0)
    def _():
        w_bf[...] = w_ref[...].astype(BF16)

    xb = x_ref[...].astype(BF16)
    q_ref[...] = jnp.dot(xb, w_bf[:, 0:D_MODEL], preferred_element_type=F32) * (MOBA_DH ** -0.5)
    k_ref[...] = jnp.dot(xb, w_bf[:, D_MODEL:2 * D_MODEL], preferred_element_type=F32)
    v_ref[...] = jnp.dot(xb, w_bf[:, 2 * D_MODEL:3 * D_MODEL], preferred_element_type=F32)


def _qkv(x, w_qkv, tm):
    m = x.shape[0]
    row = pl.BlockSpec((tm, D_MODEL), lambda i: (i, 0))
    shp = jax.ShapeDtypeStruct((m, D_MODEL), F32)
    return pl.pallas_call(
        _qkv_kernel,
        grid=(m // tm,),
        in_specs=[row, _full((D_MODEL, 3 * D_MODEL))],
        out_specs=[row, row, row],
        out_shape=[shp, shp, shp],
        scratch_shapes=[pltpu.VMEM((D_MODEL, 3 * D_MODEL), BF16)],
        compiler_params=_params(1),
        name="qkv",
    )(x, w_qkv)


def _topk_mask(gate, n_valid, axis):
    nb = gate.shape[axis]
    pos_i = lax.broadcasted_iota(jnp.int32, gate.shape, axis)
    valid = pos_i < n_valid
    pos = pos_i.astype(F32)
    cur = jnp.where(valid, gate, -jnp.inf)
    sel = jnp.zeros(gate.shape, F32)
    for _ in range(min(MOBA_TOPK, nb)):
        best = jnp.max(cur, axis=axis, keepdims=True)
        first = jnp.min(jnp.where(cur == best, pos, float(nb)), axis=axis, keepdims=True)
        pick = jnp.logical_and(pos == first, valid)
        sel = jnp.where(pick, 1.0, sel)
        cur = jnp.where(pick, -jnp.inf, cur)
    return sel


MOBA_AUG_MASK0 = 8
MOBA_NEG = -1e30


def _split3(x):
    hi = x.astype(BF16).astype(F32)
    mid = (x - hi).astype(BF16).astype(F32)
    lo = (x - hi - mid).astype(BF16).astype(F32)
    return hi, mid, lo


def _moba_prompt_kernel(q_ref, k_ref, v_ref, slope_ref, o_ref, qt_scr, vtp_scr, ka_scr, km_scr, s_scr):
    hp = pl.program_id(1)
    qt = pl.program_id(2)
    nb = km_scr.shape[0]
    blk = MOBA_BLOCK
    qw = 2 * blk
    dh = MOBA_DH
    pair = 2 * dh

    @pl.when(qt == 0)
    def _():
        lane = lax.broadcasted_iota(jnp.int32, (blk, pair), 1)
        key_off = lax.broadcasted_iota(jnp.int32, (blk, pair), 0).astype(F32)
        for n in range(nb):
            rows = slice(n * blk, (n + 1) * blk)
            half = slice((n % 2) * blk, (n % 2 + 1) * blk)
            qt_scr[n // 2, :, half] = q_ref[rows, :].T.astype(BF16)
            vtp_scr[n // 2, :, half] = v_ref[rows, :].T.astype(BF16)
            kblk = k_ref[rows, :]
            km_scr[n:n + 1, :] = jnp.mean(kblk, axis=0, keepdims=True)
            for hh in range(2):
                a = lane - (1 - hh) * dh
                aug = jnp.where(a < 3, key_off,
                                jnp.where(a < 6, float(n * blk),
                                          jnp.where(a == MOBA_AUG_MASK0 + n, 1.0, 0.0)))
                in_head = jnp.logical_and(lane >= hh * dh, lane < (hh + 1) * dh)
                ka_scr[hh, rows, :] = jnp.where(in_head, kblk, aug).astype(BF16)

    key_i = lax.broadcasted_iota(jnp.int32, (qw, qw), 0)
    qry_i = lax.broadcasted_iota(jnp.int32, (qw, qw), 1)
    same_block = (key_i >= blk) == (qry_i >= blk)
    causal = jnp.logical_and(same_block, key_i <= qry_i)
    first_key_second_qry = jnp.logical_and(key_i < blk, qry_i >= blk)
    col = lax.broadcasted_iota(jnp.int32, (1, qw), 1)
    cur_blk = 2 * qt + jnp.where(col >= blk, 1, 0)
    blk_row = lax.broadcasted_iota(jnp.int32, (nb, qw), 0)

    own = pl.ds(pl.multiple_of(qt * qw, qw), qw)
    q_t = qt_scr[qt].astype(F32)
    v_t_d = vtp_scr[qt]
    lane_k = lax.broadcasted_iota(jnp.int32, (nb, pair), 1)
    r8 = lax.broadcasted_iota(jnp.int32, (8, qw), 0)
    is_hi = jnp.logical_or(r8 == 0, r8 == 3)
    is_mid = jnp.logical_or(r8 == 1, r8 == 4)
    pad_rows = jnp.zeros((dh - MOBA_AUG_MASK0 - nb, qw), F32)

    q_aug, init = [], []
    for hh in range(2):
        hrows = slice(hh * dh, (hh + 1) * dh)
        s_hi, s_mid, s_lo = _split3(slope_ref[pl.ds(2 * hp + hh, 1), :])
        slope_rows = jnp.where(is_hi, s_hi, jnp.where(is_mid, s_mid, s_lo))
        slope_rows = jnp.where(r8 < 6, slope_rows, 0.0)

        def with_aug(aug_rows, hh=hh, hrows=hrows):
            parts = [q_t[hrows], aug_rows] if hh == 0 else [aug_rows, q_t[hrows]]
            return jnp.concatenate(parts, axis=0).astype(BF16)

        q_diag = with_aug(jnp.concatenate([slope_rows, jnp.zeros((dh - 8, qw), F32)], axis=0))
        in_head = jnp.logical_and(lane_k >= hh * dh, lane_k < (hh + 1) * dh)
        km_h = jnp.where(in_head, km_scr[...], 0.0).astype(BF16)
        gate_t = jnp.dot(km_h, q_diag, preferred_element_type=F32)
        sel = _topk_mask(gate_t, cur_blk, 0)
        mask_rows = jnp.where(sel > 0.0, 0.0, MOBA_NEG)
        q_aug.append(with_aug(jnp.concatenate([slope_rows, mask_rows, pad_rows], axis=0)))

        first_sel = jnp.sum(jnp.where(blk_row == 2 * qt, sel, 0.0), axis=0, keepdims=True)
        allowed = jnp.logical_or(causal, jnp.logical_and(first_key_second_qry, first_sel > 0.0))
        s = jnp.dot(ka_scr[hh, own, :], q_diag, preferred_element_type=F32)
        s = jnp.where(allowed, s, -jnp.inf)
        m0 = jnp.max(s, axis=0, keepdims=True)
        p = jnp.exp(s - m0)
        l0 = jnp.sum(p, axis=0, keepdims=True)
        acc0 = jnp.dot(v_t_d[hrows, :], p.astype(BF16), preferred_element_type=F32)
        init += [m0, l0, acc0]

    def scores(pair_idx, slot):
        keys = pl.ds(pl.multiple_of(pair_idx * qw, qw), qw)
        col_max = []
        for hh in range(2):
            s = jnp.dot(ka_scr[hh, keys, :], q_aug[hh], preferred_element_type=F32)
            s_scr[slot, hh] = s
            col_max.append(jnp.max(s, axis=0, keepdims=True))
        return col_max

    n_pairs = qt

    def body(jj, carry):
        slot = jj % 2
        v_pair = vtp_scr[jj]
        out = []
        for hh in range(2):
            m, l, acc, mx = carry[4 * hh:4 * hh + 4]
            m_new = jnp.maximum(m, mx)
            alpha = jnp.exp(m - m_new)
            p = jnp.exp(s_scr[slot, hh] - m_new)
            l = alpha * l + jnp.sum(p, axis=0, keepdims=True)
            acc = alpha * acc + jnp.dot(v_pair[hh * dh:(hh + 1) * dh, :], p.astype(BF16),
                                        preferred_element_type=F32)
            out.append([m_new, l, acc])
        nxt = scores(jnp.minimum(jj + 1, nb // 2 - 1), 1 - slot)
        return tuple(out[0] + [nxt[0]] + out[1] + [nxt[1]])

    mx0 = scores(0, 0)
    res = lax.fori_loop(0, n_pairs, body, tuple(init[0:3] + [mx0[0]] + init[3:6] + [mx0[1]]))
    o_ref[...] = jnp.concatenate([res[2] / res[1], res[6] / res[5]], axis=0).T


def _moba_prompt(q, k, v, slopes, batch, seq):
    m = q.shape[0]
    nb = seq // MOBA_BLOCK
    assert nb % 2 == 0 and MOBA_AUG_MASK0 + nb <= MOBA_DH
    pair = 2 * MOBA_DH
    n_pairs = MOBA_HEADS // 2
    qw = 2 * MOBA_BLOCK
    nt = nb // 2
    slope_rows = jnp.broadcast_to(slopes[:, None], (MOBA_HEADS, qw))
    seq_spec = pl.BlockSpec((seq, pair), lambda b, hp, qt: (b, hp))
    return pl.pallas_call(
        _moba_prompt_kernel,
        grid=(batch, n_pairs, nt),
        in_specs=[seq_spec, seq_spec, seq_spec, _full((MOBA_HEADS, qw))],
        out_specs=pl.BlockSpec((qw, pair), lambda b, hp, qt: (b * nt + qt, hp)),
        out_shape=jax.ShapeDtypeStruct((m, D_MODEL), F32),
        scratch_shapes=[pltpu.VMEM((nt, pair, qw), BF16), pltpu.VMEM((nt, pair, qw), BF16),
                        pltpu.VMEM((2, seq, pair), BF16), pltpu.VMEM((nb, pair), F32),
                        pltpu.VMEM((2, 2, qw, qw), F32)],
        compiler_params=_params(3),
        name="moba_prompt",
    )(q, k, v, slope_rows)


def _moba_sample_kernel(n_pages, pos0, pt_ref, q_ref, kn_ref, vn_ref, slope_ref, *refs):
    del pt_ref
    k_pages = refs[:n_pages]
    v_pages = refs[n_pages:2 * n_pages]
    o_ref = refs[2 * n_pages]
    per_blk = MOBA_BLOCK // PAGE_SIZE
    n_blk = n_pages // per_blk
    h_i = lax.broadcasted_iota(jnp.int32, (MOBA_HEADS, D_MODEL), 0)
    d_i = lax.broadcasted_iota(jnp.int32, (MOBA_HEADS, D_MODEL), 1)
    own_head = d_i // MOBA_DH == h_i
    q_bd = jnp.where(own_head, q_ref[0], 0.0)
    q_bd16 = q_bd.astype(BF16)
    slope = slope_ref[...]

    lane_n = lax.broadcasted_iota(jnp.int32, (D_MODEL, n_blk), 1)
    means = jnp.zeros((D_MODEL, n_blk), F32)
    for n in range(n_blk):
        tot = jnp.sum(sum(k_pages[n * per_blk + i][...] for i in range(per_blk)), axis=1, keepdims=True)
        means = jnp.where(lane_n == n, tot / float(MOBA_BLOCK), means)
    gate = jnp.dot(q_bd16, means.astype(BF16), preferred_element_type=F32)
    sel = _topk_mask(gate, n_blk, 1)

    lane = lax.broadcasted_iota(jnp.int32, (1, PAGE_SIZE), 1)
    scores = []
    for pg in range(n_pages):
        s = jnp.dot(q_bd16, k_pages[pg][...].astype(BF16), preferred_element_type=F32)
        dist = (pos0 - pg * PAGE_SIZE - lane).astype(F32)
        s = s - slope * dist
        n = pg // per_blk
        scores.append(jnp.where(sel[:, n:n + 1] > 0.0, s, -jnp.inf))
    k_new = kn_ref[0].astype(BF16).astype(F32)
    s_new = jnp.sum(q_bd16.astype(F32) * k_new, axis=1, keepdims=True)
    m = s_new
    for s in scores:
        m = jnp.maximum(m, jnp.max(s, axis=1, keepdims=True))
    p_new = jnp.exp(s_new - m)
    l = p_new
    d_e = lax.broadcasted_iota(jnp.int32, (D_MODEL, MOBA_HEADS), 0)
    h_e = lax.broadcasted_iota(jnp.int32, (D_MODEL, MOBA_HEADS), 1)
    expand = jnp.where(d_e // MOBA_DH == h_e, 1.0, 0.0).astype(BF16)
    acc = jnp.zeros((D_MODEL, PAGE_SIZE), F32)
    for pg in range(n_pages):
        p = jnp.exp(scores[pg] - m)
        l = l + jnp.sum(p, axis=1, keepdims=True)
        acc = acc + jnp.dot(expand, p.astype(BF16), preferred_element_type=F32) * v_pages[pg][...]
    o_past = jnp.sum(acc.T, axis=0, keepdims=True)
    row_of = lambda col: jnp.sum(jnp.where(own_head, col, 0.0), axis=0, keepdims=True)
    o_new = row_of(p_new.astype(BF16).astype(F32)) * vn_ref[0]
    o_ref[0] = (o_past + o_new) / row_of(l)


def _moba_sample(q, k_new, v_new, cache_k, cache_v, page_table, slopes, pos0):
    m = q.shape[0]
    n_pages = page_table.shape[1]
    assert pos0 == n_pages * PAGE_SIZE and pos0 % MOBA_BLOCK == 0
    n_phys = cache_k.shape[0]
    ck = jnp.transpose(cache_k, (0, 2, 3, 1)).reshape(n_phys, D_MODEL, PAGE_SIZE)
    cv = jnp.transpose(cache_v, (0, 2, 3, 1)).reshape(n_phys, D_MODEL, PAGE_SIZE)
    vec = pl.BlockSpec((1, 1, D_MODEL), lambda b, pt: (b, 0, 0))
    page_specs = [pl.BlockSpec((None, D_MODEL, PAGE_SIZE), lambda b, pt, pg=pg: (pt[b, pg], 0, 0))
                  for pg in range(n_pages)]
    grid_spec = pltpu.PrefetchScalarGridSpec(
        num_scalar_prefetch=1,
        grid=(m,),
        in_specs=[vec, vec, vec, pl.BlockSpec((MOBA_HEADS, 1), lambda b, pt: (0, 0))] + page_specs + page_specs,
        out_specs=vec,
    )
    out = pl.pallas_call(
        functools.partial(_moba_sample_kernel, n_pages, pos0),
        grid_spec=grid_spec,
        out_shape=jax.ShapeDtypeStruct((m, 1, D_MODEL), F32),
        compiler_params=_params(1),
        name="moba_sample",
    )(page_table, q.reshape(m, 1, D_MODEL), k_new.reshape(m, 1, D_MODEL), v_new.reshape(m, 1, D_MODEL),
      slopes.reshape(MOBA_HEADS, 1), *([ck] * n_pages), *([cv] * n_pages))
    return out.reshape(m, D_MODEL)


def _layer_tail(layer, xp, xs, mix_p, mix_s, p_p, p_s, w_out, ln_g, ln_b, w_router, b_router,
                w_gu, b_gu, w_down, b_down, w_pg, w_pp):
    n_p, n_s = xp.shape[0], xs.shape[0]
    lg0, lb0 = ln_g[0:1], ln_b[0:1]
    lg1, lb1 = ln_g[1:2], ln_b[1:2]
    br = b_router.reshape(1, N_EXPERTS)
    x1, idxp, gatep, cnt_p = _mix(xp, mix_p, w_out, lg0, lb0, w_router, br, ROW_TILE, n_total=n_p + n_s)
    x1, idxs, gates, cnt_s = _mix(xs, mix_s, w_out, lg0, lb0, w_router, br, n_s, x1_buf=x1)

    idx = jnp.concatenate([idxp, idxs], axis=0)
    counts = (cnt_p + cnt_s)[0].astype(jnp.int32)
    bm = MOE_BLOCK
    n_tok = n_p + n_s
    n_blocks = (n_tok * TOP_K + N_EXPERTS * (bm - 1)) // bm
    padded = (counts + bm - 1) // bm * bm
    pend = jnp.cumsum(padded)
    pstart = pend - padded
    dest = _rank(idx, pstart)
    n_used = (pend[-1] // bm).astype(jnp.int32)
    blk_ids = jnp.minimum(jnp.arange(n_blocks, dtype=jnp.int32), n_used - 1)
    block_e = jnp.sum((pend[None, :] <= (blk_ids * bm)[:, None]).astype(jnp.int32), axis=1)
    block_e = jnp.minimum(block_e, N_EXPERTS - 1)
    nk = n_tok * TOP_K
    low_bits = (nk - 1).bit_length()
    assert (N_EXPERTS << low_bits) < 2 ** 31
    order = jnp.sort((idx.reshape(-1) << low_bits) + jnp.arange(nk, dtype=jnp.int32)) & ((1 << low_bits) - 1)
    start = jnp.cumsum(counts) - counts
    r_in_group = (blk_ids * bm - pstart[block_e])[:, None] + jnp.arange(bm, dtype=jnp.int32)[None, :]
    live = r_in_group < counts[block_e][:, None]
    src = jnp.where(live, start[block_e][:, None] + r_in_group, 0)
    row_tok = jnp.where(live, order.at[src].get(mode="promise_in_bounds") // TOP_K, 0).reshape(-1)
    gather = lambda src, rows: src.at[rows].get(mode="promise_in_bounds")
    nb_a = n_blocks // 2
    ys = None
    for blk0, nb_h in ((0, nb_a), (nb_a, n_blocks - nb_a)):
        xs_rows = gather(x1, row_tok[blk0 * bm:(blk0 + nb_h) * bm])
        used_h = jnp.clip(n_used - blk0, 0, nb_h).reshape(1)
        ys = _experts(layer, block_e[blk0:blk0 + nb_h], used_h, xs_rows, w_gu, b_gu, w_down, b_down,
                      blk0, n_blocks * bm, ys_buf=ys)
    ysg_p = gather(ys, dest[:n_p].T.reshape(-1))
    ysg_s = gather(ys, dest[n_p:].T.reshape(-1))
    yp = _finish(x1, 0, n_p, ysg_p, gatep, p_p, layer, lg1, lb1, w_pg, w_pp, ROW_TILE)
    ys_out = _finish(x1, n_p, n_s, ysg_s, gates, p_s, layer, lg1, lb1, w_pg, w_pp, n_s)
    return yp, ys_out


def kernel(x_prompt, x_sample, state_gla, state_pool, cache_k, cache_v, page_table, p_prompt, p_sample, w_in_ab, gla_w_alpha, gla_b_alpha, gla_norm_g, pool_w, pool_scale, w_out_ab, w_qkv_c, w_out_c, ln_g, ln_b, moe_w_router, moe_b_router, moe_w_gu, moe_b_gu, moe_w_down, moe_b_down, ple_w_gate, ple_w_proj):
    batch, seq, _ = x_prompt.shape
    n_s = x_sample.shape[0]
    n_p = batch * seq
    pos0 = page_table.shape[1] * PAGE_SIZE
    xp = x_prompt.reshape(n_p, D_MODEL)
    xs = x_sample.reshape(n_s, D_MODEL)

    def tail(i, xp, xs, mix_p, mix_s, w_out):
        depth = p_prompt.shape[0]
        return _layer_tail(i, xp, xs, mix_p, mix_s, p_prompt.reshape(depth, n_p, -1), p_sample.reshape(depth, n_s, -1),
                           w_out, ln_g[i], ln_b[i], moe_w_router[i], moe_b_router[i], moe_w_gu,
                           moe_b_gu, moe_w_down, moe_b_down, ple_w_gate[i], ple_w_proj[i])

    w_in = w_in_ab[0]
    c_a = 2 * QK_A + 2 * V_A
    w_main = jnp.concatenate([w_in[:, :c_a], w_in[:, c_a + GLA_LOWRANK:]], axis=1)
    w_a = w_in[:, c_a:c_a + GLA_LOWRANK]
    b_alpha = gla_b_alpha[0].reshape(1, QK_A)
    norm_g = gla_norm_g[0].reshape(1, GLA_DV)
    scale = pool_scale[0].reshape(1, POOL_CH)
    qkl_p, v_p, sg_p, u_p = _proj_ab(xp, w_main, w_a, gla_w_alpha[0], b_alpha, ROW_TILE)
    qkl_s, v_s, sg_s, u_s = _proj_ab(xs, w_main, w_a, gla_w_alpha[0], b_alpha, n_s)
    mix_p, gla_p = _gla_prompt(qkl_p, v_p, sg_p, norm_g, batch, seq, ROW_TILE)
    mix_p = _pool_prompt(u_p, mix_p, pool_w[0], scale, batch, seq, ROW_TILE)
    mix_s, gla_s = _gla_sample(qkl_s, v_s, sg_s, norm_g, state_gla[0].reshape(n_s, QK_A, GLA_DV))
    mix_s, pool_s = _pool_sample(u_s, state_pool[0], mix_s, pool_w[0], scale, pos0)
    pool_p = u_p.reshape(batch, seq, POOL_CH)[:, seq - POOL_HIST:]
    xp, xs = tail(0, xp, xs, mix_p, mix_s, w_out_ab[0])

    slopes = jnp.exp2(-8.0 * jnp.arange(1, MOBA_HEADS + 1, dtype=F32) / MOBA_HEADS)
    q_p, k_p, v_p2 = _qkv(xp, w_qkv_c[0], ROW_TILE)
    q_s, k_s, v_s2 = _qkv(xs, w_qkv_c[0], n_s)
    o_p = _moba_prompt(q_p, k_p, v_p2, slopes, batch, seq)
    o_s = _moba_sample(q_s, k_s, v_s2, cache_k[0], cache_v[0], page_table, slopes, pos0)
    xp, xs = tail(1, xp, xs, o_p, o_s, w_out_c[0])

    hd = (MOBA_HEADS, MOBA_DH)
    return (xp.reshape(batch, seq, D_MODEL), xs.reshape(n_s, 1, D_MODEL),
            gla_p.reshape(1, batch, GLA_HEADS, GLA_DK, GLA_DV), gla_s.reshape(1, n_s, GLA_HEADS, GLA_DK, GLA_DV),
            pool_p[None], pool_s[None],
            k_p.reshape(1, batch, seq, *hd), v_p2.reshape(1, batch, seq, *hd),
            k_s.reshape(1, n_s, 1, *hd), v_s2.reshape(1, n_s, 1, *hd))
```

```python
import functools
import math

import numpy as np
import jax
import jax.numpy as jnp
from jax import lax
from jax.experimental import pallas as pl
from jax.experimental.pallas import tpu as pltpu

F32 = jnp.float32
BF16 = jnp.bfloat16
HI = lax.Precision.HIGHEST

D_MODEL = 1024
GLA_HEADS = 4
GLA_DK = 64
GLA_DV = 128
GLA_LOWRANK = 16
GLA_TAU = 16.0
QK_A = GLA_HEADS * GLA_DK
V_A = GLA_HEADS * GLA_DV
POOL_WINDOWS = (2, 4, 8, 16)
POOL_CH = 512
POOL_GROUP = 128
POOL_HIST = 15
MOBA_HEADS = 16
MOBA_DH = 64
MOBA_BLOCK = 256
MOBA_TOPK = 3
PAGE_SIZE = 128
N_EXPERTS = 32
TOP_K = 4
D_EXPERT = 1024
SWIGLU_LIMIT = 7.0
SWIGLU_ALPHA = 1.702
DEPTH = 2
DEEPNORM_ALPHA = (2 * DEPTH) ** 0.25
LN_EPS = 1e-5

ROW_TILE = 512
SUB_TILE = 256
GLA_CHUNK = 128
GLA_LEVELS = 7
MOE_BLOCK = 512
RANK_TILE = 384
VMEM_LIMIT = 56 * 1024 * 1024


def _params(n_axes, vmem=VMEM_LIMIT):
    return pltpu.CompilerParams(dimension_semantics=("arbitrary",) * n_axes, vmem_limit_bytes=vmem)


def _bdot(a, b):
    return jnp.dot(a.astype(BF16), b.astype(BF16), preferred_element_type=F32)


def _layer_norm(y, g, b):
    mu = jnp.mean(y, axis=-1, keepdims=True)
    yc = y - mu
    var = jnp.mean(yc * yc, axis=-1, keepdims=True)
    return yc * lax.rsqrt(var + LN_EPS) * g + b


def _full(shape):
    n = len(shape)
    return pl.BlockSpec(shape, lambda *_: (0,) * n)


def _proj_ab_kernel(x_ref, wm_ref, wa_ref, walpha_ref, balpha_ref,
                    qkl_ref, v_ref, sg_ref, u_ref, wm_bf, wa_bf):
    @pl.when(pl.program_id(0) == 0)
    def _():
        wm_bf[...] = wm_ref[...].astype(BF16)
        wa_bf[...] = wa_ref[...].astype(BF16)

    xb = x_ref[...].astype(BF16)
    qk = jnp.dot(xb, wm_bf[:, 0:2 * QK_A], preferred_element_type=F32)
    qkl_ref[:, 0:QK_A] = qk[:, 0:QK_A] * (GLA_DK ** -0.5)
    qkl_ref[:, QK_A:2 * QK_A] = qk[:, QK_A:2 * QK_A]
    a_lr = jnp.dot(xb, wa_bf[...], preferred_element_type=F32)
    z = _bdot(a_lr, walpha_ref[...]) + balpha_ref[...]
    log_sig = jnp.minimum(z, 0.0) - jnp.log1p(jnp.exp(-jnp.abs(z)))
    qkl_ref[:, 2 * QK_A:3 * QK_A] = log_sig / GLA_TAU
    c0 = 2 * QK_A
    v_ref[...] = jnp.dot(xb, wm_bf[:, c0:c0 + V_A], preferred_element_type=F32)
    g = jnp.dot(xb, wm_bf[:, c0 + V_A:c0 + 2 * V_A], preferred_element_type=F32)
    sg_ref[...] = g * jax.nn.sigmoid(g)
    u_ref[...] = jnp.dot(xb, wm_bf[:, c0 + 2 * V_A:c0 + 2 * V_A + POOL_CH], preferred_element_type=F32)


def _proj_ab(x, w_main, w_a, w_alpha, b_alpha, tm):
    m = x.shape[0]
    nmain = w_main.shape[1]
    row = lambda w: pl.BlockSpec((tm, w), lambda i: (i, 0))
    return pl.pallas_call(
        _proj_ab_kernel,
        grid=(m // tm,),
        in_specs=[row(D_MODEL), _full((D_MODEL, nmain)), _full((D_MODEL, GLA_LOWRANK)),
                  _full((GLA_LOWRANK, QK_A)), _full((1, QK_A))],
        out_specs=[row(3 * QK_A), row(V_A), row(V_A), row(POOL_CH)],
        out_shape=[jax.ShapeDtypeStruct((m, 3 * QK_A), F32), jax.ShapeDtypeStruct((m, V_A), F32),
                   jax.ShapeDtypeStruct((m, V_A), F32), jax.ShapeDtypeStruct((m, POOL_CH), F32)],
        scratch_shapes=[pltpu.VMEM((D_MODEL, nmain), BF16), pltpu.VMEM((D_MODEL, GLA_LOWRANK), BF16)],
        compiler_params=_params(1),
        name="proj_ab",
    )(x, w_main, w_a, w_alpha, b_alpha)


def _gla_tables():
    c = GLA_CHUNK
    i = np.arange(c)[:, None]
    s = np.arange(c)[None, :]
    mats = [(s <= i), (s > i)]
    for lev in range(GLA_LEVELS):
        p = GLA_LEVELS - 1 - lev
        half = 1 << p
        start = (i >> (p + 1)) << (p + 1)
        mid = start + half - 1
        upper = i >= start + half
        mats.append(np.where(upper, (s > mid) & (s <= i), (s > i) & (s <= mid)))
    seg = np.concatenate(mats, axis=0).astype(np.float32)
    j = np.arange(c)[None, :]
    x = i ^ j
    lvl = np.full((c, c), GLA_LEVELS + 1, np.int32)
    lvl[np.arange(c), np.arange(c)] = GLA_LEVELS
    for lev in range(GLA_LEVELS):
        p = GLA_LEVELS - 1 - lev
        lvl = np.where(((x >> p) == 1) & (((i >> p) & 1) == 1), lev, lvl)
    lvl4 = np.tile(lvl, (GLA_HEADS, 1)).astype(np.int32)
    lane_head = (np.arange(QK_A) // GLA_DK)[None, :]
    row_head = (np.arange(GLA_HEADS * c) // c)[:, None]
    hm4 = (lane_head == row_head).astype(np.float32)
    return seg, lvl4, hm4


def _gla_chunk(q, k, la, v, s_all, seg, lvl4, hm4):
    c = GLA_CHUNK
    e = sum(jnp.dot(seg, part.astype(BF16), preferred_element_type=F32) for part in _split3(la))
    w = jnp.exp(e)
    w_b = w[0:c]
    w_k = w[c:2 * c]

    def stack_heads(t):
        return (jnp.concatenate([t] * GLA_HEADS, axis=0) * hm4).astype(BF16)

    o_inter = jnp.dot(stack_heads(q * w_b), s_all.astype(BF16), preferred_element_type=F32)
    a = jnp.zeros((GLA_HEADS * c, c), F32)
    for lev in range(GLA_LEVELS + 1):
        if lev < GLA_LEVELS:
            w_l = w[(2 + lev) * c:(3 + lev) * c]
            ql, kl = q * w_l, k * w_l
        else:
            ql, kl = q, k
        p_l = lax.dot_general(stack_heads(ql), kl.astype(BF16), (((1,), (1,)), ((), ())),
                              preferred_element_type=F32)
        a = jnp.where(lvl4 == lev, p_l, a)
    a = a.astype(BF16)
    outs = []
    for h in range(GLA_HEADS):
        v_h = v[:, h * GLA_DV:(h + 1) * GLA_DV].astype(BF16)
        o_h = o_inter[h * c:(h + 1) * c] + jnp.dot(a[h * c:(h + 1) * c], v_h, preferred_element_type=F32)
        outs.append(o_h)
    ks_t = (k * w_k).T.astype(BF16)
    kv = jnp.dot(ks_t, v.astype(BF16), preferred_element_type=F32)
    dec = jnp.exp(jnp.sum(la.T, axis=1, keepdims=True))
    new_rows = []
    for h in range(GLA_HEADS):
        rows = slice(h * GLA_DK, (h + 1) * GLA_DK)
        new_rows.append(dec[rows] * s_all[rows] + kv[rows, h * GLA_DV:(h + 1) * GLA_DV])
    return outs, jnp.concatenate(new_rows, axis=0)


def _gla_finish(o_h, norm_g, sg_h):
    o_h = o_h * lax.rsqrt(jnp.mean(o_h * o_h, axis=-1, keepdims=True) + LN_EPS) * norm_g
    return o_h * sg_h


def _gla_prompt_kernel(qkl_ref, v_ref, sg_ref, ng_ref, seg_ref, lvl_ref, hm_ref,
                       mix_ref, state_ref, s_scr):
    t = pl.program_id(1)

    @pl.when(t == 0)
    def _():
        s_scr[...] = jnp.zeros_like(s_scr)

    seg = seg_ref[...]
    lvl4 = lvl_ref[...]
    hm4 = hm_ref[...]
    norm_g = ng_ref[...]
    n_chunks = qkl_ref.shape[0] // GLA_CHUNK

    s_all = s_scr[...]
    for ci in range(n_chunks):
        rows = pl.ds(ci * GLA_CHUNK, GLA_CHUNK)
        q = qkl_ref[rows, 0:QK_A]
        k = qkl_ref[rows, QK_A:2 * QK_A]
        la = qkl_ref[rows, 2 * QK_A:3 * QK_A]
        v = v_ref[rows, :]
        outs, s_all = _gla_chunk(q, k, la, v, s_all, seg, lvl4, hm4)
        if ci == n_chunks - 1:
            s_scr[...] = s_all
        for h in range(GLA_HEADS):
            cols = slice(h * GLA_DV, (h + 1) * GLA_DV)
            mix_ref[rows, cols] = _gla_finish(outs[h], norm_g, sg_ref[rows, cols])

    @pl.when(t == pl.num_programs(1) - 1)
    def _():
        state_ref[0] = s_scr[...]


def _gla_prompt(qkl, v, sg, norm_g, batch, seq, tm):
    m = qkl.shape[0]
    nt = seq // tm
    seg, lvl4, hm4 = _gla_tables()
    row = lambda w: pl.BlockSpec((tm, w), lambda b, t: (b * nt + t, 0))
    return pl.pallas_call(
        _gla_prompt_kernel,
        grid=(batch, nt),
        in_specs=[row(3 * QK_A), row(V_A), row(V_A), _full((1, GLA_DV)),
                  _full(seg.shape), _full(lvl4.shape), _full(hm4.shape)],
        out_specs=[pl.BlockSpec((tm, V_A), lambda b, t: (b * nt + t, 0)),
                   pl.BlockSpec((1, QK_A, GLA_DV), lambda b, t: (b, 0, 0))],
        out_shape=[jax.ShapeDtypeStruct((m, D_MODEL), F32),
                   jax.ShapeDtypeStruct((batch, QK_A, GLA_DV), F32)],
        scratch_shapes=[pltpu.VMEM((QK_A, GLA_DV), F32)],
        compiler_params=_params(2),
        name="gla_prompt",
    )(qkl, v, sg, norm_g, jnp.asarray(seg, BF16), jnp.asarray(lvl4), jnp.asarray(hm4))


GLA_SAMPLE_ROWS = 8


def _gla_sample_kernel(qkl_ref, v_ref, sg_ref, ng_ref, s0_ref, mix_ref, s1_ref):
    ones = jnp.ones((QK_A, GLA_DV), F32)
    r_i = lax.broadcasted_iota(jnp.int32, (QK_A, QK_A), 0)
    c_i = lax.broadcasted_iota(jnp.int32, (QK_A, QK_A), 1)
    eye = r_i == c_i
    norm_g = ng_ref[...]

    def col_bcast(row):
        diag = jnp.where(eye, jnp.broadcast_to(row, (QK_A, QK_A)), 0.0)
        return jnp.dot(diag, ones, precision=HI, preferred_element_type=F32)

    for r in range(GLA_SAMPLE_ROWS):
        q = col_bcast(qkl_ref[r:r + 1, 0:QK_A])
        k = col_bcast(qkl_ref[r:r + 1, QK_A:2 * QK_A])
        dec = jnp.exp(col_bcast(qkl_ref[r:r + 1, 2 * QK_A:3 * QK_A]))
        for h in range(GLA_HEADS):
            rows = slice(h * GLA_DK, (h + 1) * GLA_DK)
            cols = slice(h * GLA_DV, (h + 1) * GLA_DV)
            s_new = dec[rows] * s0_ref[r, rows, :] + k[rows] * v_ref[r:r + 1, cols]
            s1_ref[r, rows, :] = s_new
            o_h = jnp.sum(q[rows] * s_new, axis=0, keepdims=True)
            mix_ref[r:r + 1, cols] = _gla_finish(o_h, norm_g, sg_ref[r:r + 1, cols])


def _gla_sample(qkl, v, sg, norm_g, s0):
    m = qkl.shape[0]
    rb = GLA_SAMPLE_ROWS
    row = lambda w: pl.BlockSpec((rb, w), lambda i: (i, 0))
    st = pl.BlockSpec((rb, QK_A, GLA_DV), lambda i: (i, 0, 0))
    return pl.pallas_call(
        _gla_sample_kernel,
        grid=(m // rb,),
        in_specs=[row(3 * QK_A), row(V_A), row(V_A), _full((1, GLA_DV)), st],
        out_specs=[pl.BlockSpec((rb, V_A), lambda i: (i, 0)), st],
        out_shape=[jax.ShapeDtypeStruct((m, D_MODEL), F32),
                   jax.ShapeDtypeStruct((m, QK_A, GLA_DV), F32)],
        compiler_params=_params(1),
        name="gla_sample",
    )(qkl, v, sg, norm_g, s0)


def _pool_project(d_groups, wp_ref, scale_ref, mix_ref):
    for g in range(len(POOL_WINDOWS)):
        cols = slice(g * POOL_GROUP, (g + 1) * POOL_GROUP)
        mix_ref[:, cols] = _bdot(d_groups[g], wp_ref[g]) * scale_ref[:, cols]


def _pool_prompt_kernel(u_ref, prev_ref, wp_ref, scale_ref, mixin_ref, mix_ref):
    del mixin_ref
    t = pl.program_id(1)
    tm = u_ref.shape[0]
    hist = prev_ref.shape[0]
    prev = jnp.where(t > 0, prev_ref[...], 0.0)
    z = jnp.concatenate([prev, u_ref[...]], axis=0)
    pos = t * tm + lax.broadcasted_iota(jnp.int32, (tm, 1), 0)
    d_groups = []
    for g, w in enumerate(POOL_WINDOWS):
        cols = slice(g * POOL_GROUP, (g + 1) * POOL_GROUP)
        s = z[:, cols]
        shift = 1
        while shift < w:
            s = s + pltpu.roll(s, shift, 0)
            shift *= 2
        cnt = jnp.minimum(w, pos + 1).astype(F32)
        d_groups.append(s[hist:] / cnt - z[hist:, cols])
    _pool_project(d_groups, wp_ref, scale_ref, mix_ref)


def _pool_prompt(u, mix, w_pool, scale, batch, seq, tm):
    nt = seq // tm
    hist = 16
    assert hist > POOL_HIST and tm % hist == 0
    per = tm // hist
    return pl.pallas_call(
        _pool_prompt_kernel,
        grid=(batch, nt),
        in_specs=[pl.BlockSpec((tm, POOL_CH), lambda b, t: (b * nt + t, 0)),
                  pl.BlockSpec((hist, POOL_CH), lambda b, t: (jnp.maximum((b * nt + t) * per - 1, 0), 0)),
                  _full(w_pool.shape), _full((1, POOL_CH)),
                  pl.BlockSpec(memory_space=pl.ANY)],
        out_specs=pl.BlockSpec((tm, POOL_CH), lambda b, t: (b * nt + t, 1)),
        out_shape=jax.ShapeDtypeStruct(mix.shape, F32),
        input_output_aliases={4: 0},
        compiler_params=_params(2),
        name="pool_prompt",
    )(u, u, w_pool, scale, mix)


def _pool_sample_kernel(u_ref, st_ref, wp_ref, scale_ref, mixin_ref, mix_ref, st_out_ref):
    del mixin_ref
    u = u_ref[...]
    d_groups = []
    for g, w in enumerate(POOL_WINDOWS):
        cols = slice(g * POOL_GROUP, (g + 1) * POOL_GROUP)
        win = u[:, cols] + jnp.sum(st_ref[:, POOL_HIST - (w - 1):POOL_HIST, cols], axis=1)
        d_groups.append(win / float(w) - u[:, cols])
    _pool_project(d_groups, wp_ref, scale_ref, mix_ref)
    st_out_ref[:, 0:POOL_HIST - 1, :] = st_ref[:, 1:POOL_HIST, :]
    st_out_ref[:, POOL_HIST - 1:POOL_HIST, :] = u[:, None, :]


def _pool_sample(u, st, mix, w_pool, scale, pos0):
    m = u.shape[0]
    assert pos0 + 1 >= max(POOL_WINDOWS)
    return pl.pallas_call(
        _pool_sample_kernel,
        grid=(1,),
        in_specs=[_full((m, POOL_CH)), _full(st.shape), _full(w_pool.shape), _full((1, POOL_CH)),
                  pl.BlockSpec(memory_space=pl.ANY)],
        out_specs=[pl.BlockSpec((m, POOL_CH), lambda i: (0, 1)), _full(st.shape)],
        out_shape=[jax.ShapeDtypeStruct(mix.shape, F32), jax.ShapeDtypeStruct(st.shape, F32)],
        input_output_aliases={4: 0},
        compiler_params=_params(1),
        name="pool_sample",
    )(u, st, w_pool, scale, mix)


def _sub_tiles(tm):
    ts = SUB_TILE if tm % SUB_TILE == 0 else tm
    return [slice(s, s + ts) for s in range(0, tm, ts)]


def _mix_kernel(x_ref, mix_ref, w_ref, lng_ref, lnb_ref, wr_ref, br_ref, *rest):
    x1_ref, idx_ref, gate_ref, cnt_ref, w_bf = rest[-5:]

    @pl.when(pl.program_id(0) == 0)
    def _():
        w_bf[...] = w_ref[...].astype(BF16)
        cnt_ref[...] = jnp.zeros_like(cnt_ref)

    for rows in _sub_tiles(x_ref.shape[0]):
        h = jnp.dot(mix_ref[rows, :].astype(BF16), w_bf[...], preferred_element_type=F32)
        x1 = _layer_norm(DEEPNORM_ALPHA * x_ref[rows, :] + h, lng_ref[...], lnb_ref[...])
        x1_ref[rows, :] = x1
        logits = _bdot(x1, wr_ref[...]) + br_ref[...]
        ts = logits.shape[0]
        lane = lax.broadcasted_iota(jnp.int32, (ts, N_EXPERTS), 1)
        lane_k = lax.broadcasted_iota(jnp.int32, (ts, TOP_K), 1)
        idx_out = jnp.zeros((ts, TOP_K), jnp.int32)
        val_out = jnp.zeros((ts, TOP_K), F32)
        cur = logits
        chosen = jnp.zeros((ts, N_EXPERTS), F32)
        for kk in range(TOP_K):
            mval = jnp.max(cur, axis=1, keepdims=True)
            midx = jnp.min(jnp.where(cur == mval, lane.astype(F32), float(N_EXPERTS)), axis=1,
                           keepdims=True).astype(jnp.int32)
            idx_out = jnp.where(lane_k == kk, midx, idx_out)
            val_out = jnp.where(lane_k == kk, mval, val_out)
            picked = lane == midx
            cur = jnp.where(picked, -jnp.inf, cur)
            chosen = jnp.where(picked, 1.0, chosen)
        cnt_ref[...] += jnp.sum(chosen, axis=0, keepdims=True)
        ex = jnp.exp(val_out - val_out[:, 0:1])
        idx_ref[rows, :] = idx_out
        gate_ref[rows, :] = ex / jnp.sum(ex, axis=1, keepdims=True)


def _mix(x, mix, w_out, ln_g, ln_b, w_router, b_router, tm, x1_buf=None, n_total=None):
    m = x.shape[0]
    n_total = n_total if x1_buf is None else x1_buf.shape[0]
    off_blocks = 0 if x1_buf is None else (n_total - m) // tm
    row = lambda w: pl.BlockSpec((tm, w), lambda i: (i, 0))
    in_specs = [row(D_MODEL), row(D_MODEL), _full((D_MODEL, D_MODEL)), _full((1, D_MODEL)),
                _full((1, D_MODEL)), _full((D_MODEL, N_EXPERTS)), _full((1, N_EXPERTS))]
    args = [x, mix, w_out, ln_g, ln_b, w_router, b_router]
    aliases = {}
    if x1_buf is not None:
        assert (n_total - m) % tm == 0
        in_specs.append(pl.BlockSpec(memory_space=pl.ANY))
        args.append(x1_buf)
        aliases = {len(args) - 1: 0}
    return pl.pallas_call(
        _mix_kernel,
        grid=(m // tm,),
        in_specs=in_specs,
        out_specs=[pl.BlockSpec((tm, D_MODEL), lambda i: (i + off_blocks, 0)), row(TOP_K), row(TOP_K),
                   _full((1, N_EXPERTS))],
        out_shape=[jax.ShapeDtypeStruct((n_total, D_MODEL), F32),
                   jax.ShapeDtypeStruct((m, TOP_K), jnp.int32), jax.ShapeDtypeStruct((m, TOP_K), F32),
                   jax.ShapeDtypeStruct((1, N_EXPERTS), F32)],
        scratch_shapes=[pltpu.VMEM((D_MODEL, D_MODEL), BF16)],
        input_output_aliases=aliases,
        compiler_params=_params(1),
        name="mix",
    )(*args)


def _rank_kernel(idx_ref, pstart_ref, rank_ref, carry):
    i = pl.program_id(0)

    @pl.when(i == 0)
    def _():
        carry[...] = pstart_ref[...].astype(F32)

    idx = idx_ref[...]
    tm = idx.shape[0]
    lane = lax.broadcasted_iota(jnp.int32, (tm, N_EXPERTS), 1)
    onehots = [(idx[:, kk:kk + 1] == lane) for kk in range(TOP_K)]
    member = sum(jnp.where(o, 1.0, 0.0) for o in onehots)
    r_i = lax.broadcasted_iota(jnp.int32, (tm, tm), 0)
    c_i = lax.broadcasted_iota(jnp.int32, (tm, tm), 1)
    strict_lower = jnp.where(c_i < r_i, 1.0, 0.0).astype(BF16)
    before = jnp.dot(strict_lower, member.astype(BF16), preferred_element_type=F32) + carry[...]
    lane_k = lax.broadcasted_iota(jnp.int32, (tm, TOP_K), 1)
    rank = jnp.zeros((tm, TOP_K), F32)
    for kk in range(TOP_K):
        r_k = jnp.sum(jnp.where(onehots[kk], before, 0.0), axis=1, keepdims=True)
        rank = jnp.where(lane_k == kk, r_k, rank)
    rank_ref[...] = rank.astype(jnp.int32)
    carry[...] = carry[...] + jnp.sum(member, axis=0, keepdims=True)


def _rank(idx, pstart):
    n = idx.shape[0]
    tm = RANK_TILE
    return pl.pallas_call(
        _rank_kernel,
        grid=(n // tm,),
        in_specs=[pl.BlockSpec((tm, TOP_K), lambda i: (i, 0)), _full((1, N_EXPERTS))],
        out_specs=pl.BlockSpec((tm, TOP_K), lambda i: (i, 0)),
        out_shape=jax.ShapeDtypeStruct((n, TOP_K), jnp.int32),
        scratch_shapes=[pltpu.VMEM((1, N_EXPERTS), F32)],
        compiler_params=_params(1),
        name="moe_rank",
    )(idx, pstart.reshape(1, N_EXPERTS))


def _expert_kernel(be_ref, nu_ref, xs_ref, wgu_ref, bgu_ref, wd_ref, bd_ref, ys_ref, wgu_bf, wd_bf):
    blk = pl.program_id(0)
    prev = be_ref[jnp.maximum(blk - 1, 0)]
    fresh = jnp.logical_or(blk == 0, be_ref[blk] != prev)
    used = blk < nu_ref[0]

    @pl.when(jnp.logical_and(fresh, used))
    def _():
        wgu_bf[...] = wgu_ref[0, 0].astype(BF16)
        wd_bf[...] = wd_ref[0, 0].astype(BF16)

    @pl.when(used)
    def _():
        xb = xs_ref[...].astype(BF16)
        acc = jnp.zeros(ys_ref.shape, F32) + bd_ref[0, 0]
        half = D_EXPERT // 2
        for c in range(2):
            cg = slice(c * half, (c + 1) * half)
            cu = slice(D_EXPERT + c * half, D_EXPERT + (c + 1) * half)
            gate = jnp.dot(xb, wgu_bf[:, cg], preferred_element_type=F32) + bgu_ref[0, 0, :, cg]
            up = jnp.dot(xb, wgu_bf[:, cu], preferred_element_type=F32) + bgu_ref[0, 0, :, cu]
            gate = jnp.minimum(gate, SWIGLU_LIMIT)
            up = jnp.clip(up, -SWIGLU_LIMIT, SWIGLU_LIMIT)
            act = (up + 1.0) * gate * jax.nn.sigmoid(SWIGLU_ALPHA * gate)
            acc = acc + jnp.dot(act.astype(BF16), wd_bf[cg, :], preferred_element_type=F32)
        ys_ref[...] = acc

    @pl.when(jnp.logical_not(used))
    def _():
        ys_ref[...] = jnp.zeros_like(ys_ref)


def _experts(layer, block_e, n_used, xs, w_gu, b_gu, w_down, b_down):
    p = xs.shape[0]
    bm = MOE_BLOCK
    nb = p // bm
    depth = w_gu.shape[0]
    last = lambda blk, nu: jnp.minimum(blk, nu[0] - 1)
    grid_spec = pltpu.PrefetchScalarGridSpec(
        num_scalar_prefetch=2,
        grid=(nb,),
        in_specs=[pl.BlockSpec((bm, D_MODEL), lambda blk, be, nu: (last(blk, nu), 0)),
                  pl.BlockSpec((1, 1, D_MODEL, 2 * D_EXPERT), lambda blk, be, nu: (layer, be[blk], 0, 0)),
                  pl.BlockSpec((1, 1, 1, 2 * D_EXPERT), lambda blk, be, nu: (layer, be[blk], 0, 0)),
                  pl.BlockSpec((1, 1, D_EXPERT, D_MODEL), lambda blk, be, nu: (layer, be[blk], 0, 0)),
                  pl.BlockSpec((1, 1, 1, D_MODEL), lambda blk, be, nu: (layer, be[blk], 0, 0))],
        out_specs=pl.BlockSpec((bm, D_MODEL), lambda blk, be, nu: (blk, 0)),
        scratch_shapes=[pltpu.VMEM((D_MODEL, 2 * D_EXPERT), BF16), pltpu.VMEM((D_EXPERT, D_MODEL), BF16)],
    )
    return pl.pallas_call(
        _expert_kernel,
        grid_spec=grid_spec,
        out_shape=jax.ShapeDtypeStruct((p, D_MODEL), F32),
        compiler_params=_params(1),
        name="moe_experts",
    )(block_e, n_used, xs, w_gu, b_gu.reshape(depth, N_EXPERTS, 1, 2 * D_EXPERT), w_down,
      b_down.reshape(depth, N_EXPERTS, 1, D_MODEL))


def _finish_kernel(x1_ref, y0_ref, y1_ref, y2_ref, y3_ref, gate_ref, p_ref, lng_ref, lnb_ref,
                   wpg_ref, wpp_ref, out_ref, wpg_bf):
    @pl.when(pl.program_id(0) == 0)
    def _():
        wpg_bf[...] = wpg_ref[...].astype(BF16)

    for rows in _sub_tiles(out_ref.shape[0]):
        gates = gate_ref[rows, :]
        moe = y0_ref[rows, :] * gates[:, 0:1]
        for kk, y_ref in enumerate((y1_ref, y2_ref, y3_ref), start=1):
            moe = moe + y_ref[rows, :] * gates[:, kk:kk + 1]
        x2 = _layer_norm(DEEPNORM_ALPHA * x1_ref[rows, :] + moe, lng_ref[...], lnb_ref[...])
        pg = jax.nn.sigmoid(jnp.dot(x2.astype(BF16), wpg_bf[...], preferred_element_type=F32))
        pp = _bdot(p_ref[rows, :], wpp_ref[...])
        out_ref[rows, :] = x2 + pg * pp


def _finish(x1_all, row0, m, ysg, gates, p_all, layer, ln_g, ln_b, w_pg, w_pp, tm):
    nt = m // tm
    assert row0 % tm == 0
    row = lambda w: pl.BlockSpec((tm, w), lambda i: (i, 0))
    ple = p_all.shape[2]
    y_specs = [pl.BlockSpec((tm, D_MODEL), lambda i, kk=kk: (kk * nt + i, 0)) for kk in range(TOP_K)]
    x1_spec = pl.BlockSpec((tm, D_MODEL), lambda i: (i + row0 // tm, 0))
    p_spec = pl.BlockSpec((None, tm, ple), lambda i: (layer, i, 0))
    return pl.pallas_call(
        _finish_kernel,
        grid=(nt,),
        in_specs=[x1_spec] + y_specs + [row(TOP_K), p_spec, _full((1, D_MODEL)), _full((1, D_MODEL)),
                                        _full((D_MODEL, D_MODEL)), _full((ple, D_MODEL))],
        out_specs=row(D_MODEL),
        out_shape=jax.ShapeDtypeStruct((m, D_MODEL), F32),
        scratch_shapes=[pltpu.VMEM((D_MODEL, D_MODEL), BF16)],
        compiler_params=_params(1),
        name="finish",
    )(x1_all, ysg, ysg, ysg, ysg, gates, p_all, ln_g, ln_b, w_pg, w_pp)


def _qkv_kernel(x_ref, w_ref, q_ref, k_ref, v_ref, w_bf):
    @pl.when(pl.program_id(0) == 0)
    def _():
        w_bf[...] = w_ref[...].astype(BF16)

    xb = x_ref[...].astype(BF16)
    q_ref[...] = jnp.dot(xb, w_bf[:, 0:D_MODEL], preferred_element_type=F32) * (MOBA_DH ** -0.5)
    k_ref[...] = jnp.dot(xb, w_bf[:, D_MODEL:2 * D_MODEL], preferred_element_type=F32)
    v_ref[...] = jnp.dot(xb, w_bf[:, 2 * D_MODEL:3 * D_MODEL], preferred_element_type=F32)


def _qkv(x, w_qkv, tm):
    m = x.shape[0]
    row = pl.BlockSpec((tm, D_MODEL), lambda i: (i, 0))
    shp = jax.ShapeDtypeStruct((m, D_MODEL), F32)
    return pl.pallas_call(
        _qkv_kernel,
        grid=(m // tm,),
        in_specs=[row, _full((D_MODEL, 3 * D_MODEL))],
        out_specs=[row, row, row],
        out_shape=[shp, shp, shp],
        scratch_shapes=[pltpu.VMEM((D_MODEL, 3 * D_MODEL), BF16)],
        compiler_params=_params(1),
        name="qkv",
    )(x, w_qkv)


def _topk_mask(gate, n_valid, axis):
    nb = gate.shape[axis]
    pos_i = lax.broadcasted_iota(jnp.int32, gate.shape, axis)
    valid = pos_i < n_valid
    pos = pos_i.astype(F32)
    cur = jnp.where(valid, gate, -jnp.inf)
    sel = jnp.zeros(gate.shape, F32)
    for _ in range(min(MOBA_TOPK, nb)):
        best = jnp.max(cur, axis=axis, keepdims=True)
        first = jnp.min(jnp.where(cur == best, pos, float(nb)), axis=axis, keepdims=True)
        pick = jnp.logical_and(pos == first, valid)
        sel = jnp.where(pick, 1.0, sel)
        cur = jnp.where(pick, -jnp.inf, cur)
    return sel


MOBA_AUG_MASK0 = 8
MOBA_NEG = -1e30


def _split3(x):
    hi = x.astype(BF16).astype(F32)
    mid = (x - hi).astype(BF16).astype(F32)
    lo = (x - hi - mid).astype(BF16).astype(F32)
    return hi, mid, lo


def _moba_prompt_kernel(q_ref, k_ref, v_ref, slope_ref, o_ref, qt_scr, vtp_scr, ka_scr, km_scr, s_scr):
    hp = pl.program_id(1)
    qt = pl.program_id(2)
    nb = km_scr.shape[0]
    blk = MOBA_BLOCK
    qw = 2 * blk
    dh = MOBA_DH
    pair = 2 * dh

    @pl.when(qt == 0)
    def _():
        lane = lax.broadcasted_iota(jnp.int32, (blk, pair), 1)
        key_off = lax.broadcasted_iota(jnp.int32, (blk, pair), 0).astype(F32)
        for n in range(nb):
            rows = slice(n * blk, (n + 1) * blk)
            half = slice((n % 2) * blk, (n % 2 + 1) * blk)
            qt_scr[n // 2, :, half] = q_ref[rows, :].T.astype(BF16)
            vtp_scr[n // 2, :, half] = v_ref[rows, :].T.astype(BF16)
            kblk = k_ref[rows, :]
            km_scr[n:n + 1, :] = jnp.mean(kblk, axis=0, keepdims=True)
            for hh in range(2):
                a = lane - (1 - hh) * dh
                aug = jnp.where(a < 3, key_off,
                                jnp.where(a < 6, float(n * blk),
                                          jnp.where(a == MOBA_AUG_MASK0 + n, 1.0, 0.0)))
                in_head = jnp.logical_and(lane >= hh * dh, lane < (hh + 1) * dh)
                ka_scr[hh, rows, :] = jnp.where(in_head, kblk, aug).astype(BF16)

    key_i = lax.broadcasted_iota(jnp.int32, (qw, qw), 0)
    qry_i = lax.broadcasted_iota(jnp.int32, (qw, qw), 1)
    same_block = (key_i >= blk) == (qry_i >= blk)
    causal = jnp.logical_and(same_block, key_i <= qry_i)
    first_key_second_qry = jnp.logical_and(key_i < blk, qry_i >= blk)
    col = lax.broadcasted_iota(jnp.int32, (1, qw), 1)
    cur_blk = 2 * qt + jnp.where(col >= blk, 1, 0)
    blk_row = lax.broadcasted_iota(jnp.int32, (nb, qw), 0)

    own = pl.ds(pl.multiple_of(qt * qw, qw), qw)
    q_t = qt_scr[qt].astype(F32)
    v_t_d = vtp_scr[qt]
    lane_k = lax.broadcasted_iota(jnp.int32, (nb, pair), 1)
    r8 = lax.broadcasted_iota(jnp.int32, (8, qw), 0)
    is_hi = jnp.logical_or(r8 == 0, r8 == 3)
    is_mid = jnp.logical_or(r8 == 1, r8 == 4)
    pad_rows = jnp.zeros((dh - MOBA_AUG_MASK0 - nb, qw), F32)

    q_aug, init = [], []
    for hh in range(2):
        hrows = slice(hh * dh, (hh + 1) * dh)
        s_hi, s_mid, s_lo = _split3(slope_ref[pl.ds(2 * hp + hh, 1), :])
        slope_rows = jnp.where(is_hi, s_hi, jnp.where(is_mid, s_mid, s_lo))
        slope_rows = jnp.where(r8 < 6, slope_rows, 0.0)

        def with_aug(aug_rows, hh=hh, hrows=hrows):
            parts = [q_t[hrows], aug_rows] if hh == 0 else [aug_rows, q_t[hrows]]
            return jnp.concatenate(parts, axis=0).astype(BF16)

        q_diag = with_aug(jnp.concatenate([slope_rows, jnp.zeros((dh - 8, qw), F32)], axis=0))
        in_head = jnp.logical_and(lane_k >= hh * dh, lane_k < (hh + 1) * dh)
        km_h = jnp.where(in_head, km_scr[...], 0.0).astype(BF16)
        gate_t = jnp.dot(km_h, q_diag, preferred_element_type=F32)
        sel = _topk_mask(gate_t, cur_blk, 0)
        mask_rows = jnp.where(sel > 0.0, 0.0, MOBA_NEG)
        q_aug.append(with_aug(jnp.concatenate([slope_rows, mask_rows, pad_rows], axis=0)))

        first_sel = jnp.sum(jnp.where(blk_row == 2 * qt, sel, 0.0), axis=0, keepdims=True)
        allowed = jnp.logical_or(causal, jnp.logical_and(first_key_second_qry, first_sel > 0.0))
        s = jnp.dot(ka_scr[hh, own, :], q_diag, preferred_element_type=F32)
        s = jnp.where(allowed, s, -jnp.inf)
        m0 = jnp.max(s, axis=0, keepdims=True)
        p = jnp.exp(s - m0)
        l0 = jnp.sum(p, axis=0, keepdims=True)
        acc0 = jnp.dot(v_t_d[hrows, :], p.astype(BF16), preferred_element_type=F32)
        init += [m0, l0, acc0]

    def scores(pair_idx, slot):
        keys = pl.ds(pl.multiple_of(pair_idx * qw, qw), qw)
        col_max = []
        for hh in range(2):
            s = jnp.dot(ka_scr[hh, keys, :], q_aug[hh], preferred_element_type=F32)
            s_scr[slot, hh] = s
            col_max.append(jnp.max(s, axis=0, keepdims=True))
        return col_max

    n_pairs = qt

    def body(jj, carry):
        slot = jj % 2
        v_pair = vtp_scr[jj]
        out = []
        for hh in range(2):
            m, l, acc, mx = carry[4 * hh:4 * hh + 4]
            m_new = jnp.maximum(m, mx)
            alpha = jnp.exp(m - m_new)
            p = jnp.exp(s_scr[slot, hh] - m_new)
            l = alpha * l + jnp.sum(p, axis=0, keepdims=True)
            acc = alpha * acc + jnp.dot(v_pair[hh * dh:(hh + 1) * dh, :], p.astype(BF16),
                                        preferred_element_type=F32)
            out.append([m_new, l, acc])
        nxt = scores(jnp.minimum(jj + 1, nb // 2 - 1), 1 - slot)
        return tuple(out[0] + [nxt[0]] + out[1] + [nxt[1]])

    mx0 = scores(0, 0)
    res = lax.fori_loop(0, n_pairs, body, tuple(init[0:3] + [mx0[0]] + init[3:6] + [mx0[1]]))
    o_ref[...] = jnp.concatenate([res[2] / res[1], res[6] / res[5]], axis=0).T


def _moba_prompt(q, k, v, slopes, batch, seq):
    m = q.shape[0]
    nb = seq // MOBA_BLOCK
    assert nb % 2 == 0 and MOBA_AUG_MASK0 + nb <= MOBA_DH
    pair = 2 * MOBA_DH
    n_pairs = MOBA_HEADS // 2
    qw = 2 * MOBA_BLOCK
    nt = nb // 2
    slope_rows = jnp.broadcast_to(slopes[:, None], (MOBA_HEADS, qw))
    seq_spec = pl.BlockSpec((seq, pair), lambda b, hp, qt: (b, hp))
    return pl.pallas_call(
        _moba_prompt_kernel,
        grid=(batch, n_pairs, nt),
        in_specs=[seq_spec, seq_spec, seq_spec, _full((MOBA_HEADS, qw))],
        out_specs=pl.BlockSpec((qw, pair), lambda b, hp, qt: (b * nt + qt, hp)),
        out_shape=jax.ShapeDtypeStruct((m, D_MODEL), F32),
        scratch_shapes=[pltpu.VMEM((nt, pair, qw), BF16), pltpu.VMEM((nt, pair, qw), BF16),
                        pltpu.VMEM((2, seq, pair), BF16), pltpu.VMEM((nb, pair), F32),
                        pltpu.VMEM((2, 2, qw, qw), F32)],
        compiler_params=_params(3),
        name="moba_prompt",
    )(q, k, v, slope_rows)


def _moba_sample_kernel(n_pages, pos0, pt_ref, q_ref, kn_ref, vn_ref, slope_ref, *refs):
    del pt_ref
    k_pages = refs[:n_pages]
    v_pages = refs[n_pages:2 * n_pages]
    o_ref = refs[2 * n_pages]
    per_blk = MOBA_BLOCK // PAGE_SIZE
    n_blk = n_pages // per_blk
    h_i = lax.broadcasted_iota(jnp.int32, (MOBA_HEADS, D_MODEL), 0)
    d_i = lax.broadcasted_iota(jnp.int32, (MOBA_HEADS, D_MODEL), 1)
    own_head = d_i // MOBA_DH == h_i
    q_bd = jnp.where(own_head, q_ref[0], 0.0)
    q_bd16 = q_bd.astype(BF16)
    slope = slope_ref[...]

    lane_n = lax.broadcasted_iota(jnp.int32, (D_MODEL, n_blk), 1)
    means = jnp.zeros((D_MODEL, n_blk), F32)
    for n in range(n_blk):
        tot = jnp.sum(sum(k_pages[n * per_blk + i][...] for i in range(per_blk)), axis=1, keepdims=True)
        means = jnp.where(lane_n == n, tot / float(MOBA_BLOCK), means)
    gate = jnp.dot(q_bd16, means.astype(BF16), preferred_element_type=F32)
    sel = _topk_mask(gate, n_blk, 1)

    lane = lax.broadcasted_iota(jnp.int32, (1, PAGE_SIZE), 1)
    scores = []
    for pg in range(n_pages):
        s = jnp.dot(q_bd16, k_pages[pg][...].astype(BF16), preferred_element_type=F32)
        dist = (pos0 - pg * PAGE_SIZE - lane).astype(F32)
        s = s - slope * dist
        n = pg // per_blk
        scores.append(jnp.where(sel[:, n:n + 1] > 0.0, s, -jnp.inf))
    k_new = kn_ref[0].astype(BF16).astype(F32)
    s_new = jnp.sum(q_bd16.astype(F32) * k_new, axis=1, keepdims=True)
    m = s_new
    for s in scores:
        m = jnp.maximum(m, jnp.max(s, axis=1, keepdims=True))
    p_new = jnp.exp(s_new - m)
    l = p_new
    d_e = lax.broadcasted_iota(jnp.int32, (D_MODEL, MOBA_HEADS), 0)
    h_e = lax.broadcasted_iota(jnp.int32, (D_MODEL, MOBA_HEADS), 1)
    expand = jnp.where(d_e // MOBA_DH == h_e, 1.0, 0.0).astype(BF16)
    acc = jnp.zeros((D_MODEL, PAGE_SIZE), F32)
    for pg in range(n_pages):
        p = jnp.exp(scores[pg] - m)
        l = l + jnp.sum(p, axis=1, keepdims=True)
        acc = acc + jnp.dot(expand, p.astype(BF16), preferred_element_type=F32) * v_pages[pg][...]
    o_past = jnp.sum(acc.T, axis=0, keepdims=True)
    row_of = lambda col: jnp.sum(jnp.where(own_head, col, 0.0), axis=0, keepdims=True)
    o_new = row_of(p_new.astype(BF16).astype(F32)) * vn_ref[0]
    o_ref[0] = (o_past + o_new) / row_of(l)


def _moba_sample(q, k_new, v_new, cache_k, cache_v, page_table, slopes, pos0):
    m = q.shape[0]
    n_pages = page_table.shape[1]
    assert pos0 == n_pages * PAGE_SIZE and pos0 % MOBA_BLOCK == 0
    n_phys = cache_k.shape[0]
    ck = jnp.transpose(cache_k, (0, 2, 3, 1)).reshape(n_phys, D_MODEL, PAGE_SIZE)
    cv = jnp.transpose(cache_v, (0, 2, 3, 1)).reshape(n_phys, D_MODEL, PAGE_SIZE)
    vec = pl.BlockSpec((1, 1, D_MODEL), lambda b, pt: (b, 0, 0))
    page_specs = [pl.BlockSpec((None, D_MODEL, PAGE_SIZE), lambda b, pt, pg=pg: (pt[b, pg], 0, 0))
                  for pg in range(n_pages)]
    grid_spec = pltpu.PrefetchScalarGridSpec(
        num_scalar_prefetch=1,
        grid=(m,),
        in_specs=[vec, vec, vec, pl.BlockSpec((MOBA_HEADS, 1), lambda b, pt: (0, 0))] + page_specs + page_specs,
        out_specs=vec,
    )
    out = pl.pallas_call(
        functools.partial(_moba_sample_kernel, n_pages, pos0),
        grid_spec=grid_spec,
        out_shape=jax.ShapeDtypeStruct((m, 1, D_MODEL), F32),
        compiler_params=_params(1),
        name="moba_sample",
    )(page_table, q.reshape(m, 1, D_MODEL), k_new.reshape(m, 1, D_MODEL), v_new.reshape(m, 1, D_MODEL),
      slopes.reshape(MOBA_HEADS, 1), *([ck] * n_pages), *([cv] * n_pages))
    return out.reshape(m, D_MODEL)


def _layer_tail(layer, xp, xs, mix_p, mix_s, p_p, p_s, w_out, ln_g, ln_b, w_router, b_router,
                w_gu, b_gu, w_down, b_down, w_pg, w_pp):
    n_p, n_s = xp.shape[0], xs.shape[0]
    lg0, lb0 = ln_g[0:1], ln_b[0:1]
    lg1, lb1 = ln_g[1:2], ln_b[1:2]
    br = b_router.reshape(1, N_EXPERTS)
    x1, idxp, gatep, cnt_p = _mix(xp, mix_p, w_out, lg0, lb0, w_router, br, ROW_TILE, n_total=n_p + n_s)
    x1, idxs, gates, cnt_s = _mix(xs, mix_s, w_out, lg0, lb0, w_router, br, n_s, x1_buf=x1)

    idx = jnp.concatenate([idxp, idxs], axis=0)
    counts = (cnt_p + cnt_s)[0].astype(jnp.int32)
    bm = MOE_BLOCK
    n_tok = n_p + n_s
    n_blocks = (n_tok * TOP_K + N_EXPERTS * (bm - 1)) // bm
    padded = (counts + bm - 1) // bm * bm
    pend = jnp.cumsum(padded)
    pstart = pend - padded
    dest = _rank(idx, pstart)
    n_used = (pend[-1] // bm).astype(jnp.int32)
    blk_ids = jnp.minimum(jnp.arange(n_blocks, dtype=jnp.int32), n_used - 1)
    block_e = jnp.sum((pend[None, :] <= (blk_ids * bm)[:, None]).astype(jnp.int32), axis=1)
    block_e = jnp.minimum(block_e, N_EXPERTS - 1)
    nk = n_tok * TOP_K
    low_bits = (nk - 1).bit_length()
    assert (N_EXPERTS << low_bits) < 2 ** 31
    order = jnp.sort((idx.reshape(-1) << low_bits) + jnp.arange(nk, dtype=jnp.int32)) & ((1 << low_bits) - 1)
    start = jnp.cumsum(counts) - counts
    r_in_group = (blk_ids * bm - pstart[block_e])[:, None] + jnp.arange(bm, dtype=jnp.int32)[None, :]
    live = r_in_group < counts[block_e][:, None]
    src = jnp.where(live, start[block_e][:, None] + r_in_group, 0)
    row_tok = jnp.where(live, order.at[src].get(mode="promise_in_bounds") // TOP_K, 0).reshape(-1)
    gather = lambda src, rows: src.at[rows].get(mode="promise_in_bounds")
    xs_rows = gather(x1, row_tok)
    ys = _experts(layer, block_e, n_used.reshape(1), xs_rows, w_gu, b_gu, w_down, b_down)
    ysg_p = gather(ys, dest[:n_p].T.reshape(-1))
    ysg_s = gather(ys, dest[n_p:].T.reshape(-1))
    yp = _finish(x1, 0, n_p, ysg_p, gatep, p_p, layer, lg1, lb1, w_pg, w_pp, ROW_TILE)
    ys_out = _finish(x1, n_p, n_s, ysg_s, gates, p_s, layer, lg1, lb1, w_pg, w_pp, n_s)
    return yp, ys_out


def kernel(x_prompt, x_sample, state_gla, state_pool, cache_k, cache_v, page_table, p_prompt, p_sample, w_in_ab, gla_w_alpha, gla_b_alpha, gla_norm_g, pool_w, pool_scale, w_out_ab, w_qkv_c, w_out_c, ln_g, ln_b, moe_w_router, moe_b_router, moe_w_gu, moe_b_gu, moe_w_down, moe_b_down, ple_w_gate, ple_w_proj):
    batch, seq, _ = x_prompt.shape
    n_s = x_sample.shape[0]
    n_p = batch * seq
    pos0 = page_table.shape[1] * PAGE_SIZE
    xp = x_prompt.reshape(n_p, D_MODEL)
    xs = x_sample.reshape(n_s, D_MODEL)

    def tail(i, xp, xs, mix_p, mix_s, w_out):
        depth = p_prompt.shape[0]
        return _layer_tail(i, xp, xs, mix_p, mix_s, p_prompt.reshape(depth, n_p, -1), p_sample.reshape(depth, n_s, -1),
                           w_out, ln_g[i], ln_b[i], moe_w_router[i], moe_b_router[i], moe_w_gu,
                           moe_b_gu, moe_w_down, moe_b_down, ple_w_gate[i], ple_w_proj[i])

    w_in = w_in_ab[0]
    c_a = 2 * QK_A + 2 * V_A
    w_main = jnp.concatenate([w_in[:, :c_a], w_in[:, c_a + GLA_LOWRANK:]], axis=1)
    w_a = w_in[:, c_a:c_a + GLA_LOWRANK]
    b_alpha = gla_b_alpha[0].reshape(1, QK_A)
    norm_g = gla_norm_g[0].reshape(1, GLA_DV)
    scale = pool_scale[0].reshape(1, POOL_CH)
    qkl_p, v_p, sg_p, u_p = _proj_ab(xp, w_main, w_a, gla_w_alpha[0], b_alpha, ROW_TILE)
    qkl_s, v_s, sg_s, u_s = _proj_ab(xs, w_main, w_a, gla_w_alpha[0], b_alpha, n_s)
    mix_p, gla_p = _gla_prompt(qkl_p, v_p, sg_p, norm_g, batch, seq, ROW_TILE)
    mix_p = _pool_prompt(u_p, mix_p, pool_w[0], scale, batch, seq, ROW_TILE)
    mix_s, gla_s = _gla_sample(qkl_s, v_s, sg_s, norm_g, state_gla[0].reshape(n_s, QK_A, GLA_DV))
    mix_s, pool_s = _pool_sample(u_s, state_pool[0], mix_s, pool_w[0], scale, pos0)
    pool_p = u_p.reshape(batch, seq, POOL_CH)[:, seq - POOL_HIST:]
    xp, xs = tail(0, xp, xs, mix_p, mix_s, w_out_ab[0])

    slopes = jnp.exp2(-8.0 * jnp.arange(1, MOBA_HEADS + 1, dtype=F32) / MOBA_HEADS)
    q_p, k_p, v_p2 = _qkv(xp, w_qkv_c[0], ROW_TILE)
    q_s, k_s, v_s2 = _qkv(xs, w_qkv_c[0], n_s)
    o_p = _moba_prompt(q_p, k_p, v_p2, slopes, batch, seq)
    o_s = _moba_sample(q_s, k_s, v_s2, cache_k[0], cache_v[0], page_table, slopes, pos0)
    xp, xs = tail(1, xp, xs, o_p, o_s, w_out_c[0])

    hd = (MOBA_HEADS, MOBA_DH)
    return (xp.reshape(batch, seq, D_MODEL), xs.reshape(n_s, 1, D_MODEL),
            gla_p.reshape(1, batch, GLA_HEADS, GLA_DK, GLA_DV), gla_s.reshape(1, n_s, GLA_HEADS, GLA_DK, GLA_DV),
            pool_p[None], pool_s[None],
            k_p.reshape(1, batch, seq, *hd), v_p2.reshape(1, batch, seq, *hd),
            k_s.reshape(1, n_s, 1, *hd), v_s2.reshape(1, n_s, 1, *hd))
```

```python
import functools
import math

import numpy as np
import jax
import jax.numpy as jnp
from jax import lax
from jax.experimental import pallas as pl
from jax.experimental.pallas import tpu as pltpu

F32 = jnp.float32
BF16 = jnp.bfloat16
HI = lax.Precision.HIGHEST

D_MODEL = 1024
GLA_HEADS = 4
GLA_DK = 64
GLA_DV = 128
GLA_LOWRANK = 16
GLA_TAU = 16.0
QK_A = GLA_HEADS * GLA_DK
V_A = GLA_HEADS * GLA_DV
POOL_WINDOWS = (2, 4, 8, 16)
POOL_CH = 512
POOL_GROUP = 128
POOL_HIST = 15
MOBA_HEADS = 16
MOBA_DH = 64
MOBA_BLOCK = 256
MOBA_TOPK = 3
PAGE_SIZE = 128
N_EXPERTS = 32
TOP_K = 4
D_EXPERT = 1024
SWIGLU_LIMIT = 7.0
SWIGLU_ALPHA = 1.702
DEPTH = 2
DEEPNORM_ALPHA = (2 * DEPTH) ** 0.25
LN_EPS = 1e-5

ROW_TILE = 512
SUB_TILE = 256
GLA_CHUNK = 128
GLA_LEVELS = 7
MOE_BLOCK = 512
RANK_TILE = 384
VMEM_LIMIT = 56 * 1024 * 1024


def _params(n_axes, vmem=VMEM_LIMIT):
    return pltpu.CompilerParams(dimension_semantics=("arbitrary",) * n_axes, vmem_limit_bytes=vmem)


def _bdot(a, b):
    return jnp.dot(a.astype(BF16), b.astype(BF16), preferred_element_type=F32)


def _layer_norm(y, g, b):
    mu = jnp.mean(y, axis=-1, keepdims=True)
    yc = y - mu
    var = jnp.mean(yc * yc, axis=-1, keepdims=True)
    return yc * lax.rsqrt(var + LN_EPS) * g + b


def _full(shape):
    n = len(shape)
    return pl.BlockSpec(shape, lambda *_: (0,) * n)


def _proj_ab_kernel(x_ref, wm_ref, wa_ref, walpha_ref, balpha_ref,
                    qkl_ref, v_ref, sg_ref, u_ref, wm_bf, wa_bf):
    @pl.when(pl.program_id(0) == 0)
    def _():
        wm_bf[...] = wm_ref[...].astype(BF16)
        wa_bf[...] = wa_ref[...].astype(BF16)

    xb = x_ref[...].astype(BF16)
    qk = jnp.dot(xb, wm_bf[:, 0:2 * QK_A], preferred_element_type=F32)
    qkl_ref[:, 0:QK_A] = qk[:, 0:QK_A] * (GLA_DK ** -0.5)
    qkl_ref[:, QK_A:2 * QK_A] = qk[:, QK_A:2 * QK_A]
    a_lr = jnp.dot(xb, wa_bf[...], preferred_element_type=F32)
    z = _bdot(a_lr, walpha_ref[...]) + balpha_ref[...]
    log_sig = jnp.minimum(z, 0.0) - jnp.log1p(jnp.exp(-jnp.abs(z)))
    qkl_ref[:, 2 * QK_A:3 * QK_A] = log_sig / GLA_TAU
    c0 = 2 * QK_A
    v_ref[...] = jnp.dot(xb, wm_bf[:, c0:c0 + V_A], preferred_element_type=F32)
    g = jnp.dot(xb, wm_bf[:, c0 + V_A:c0 + 2 * V_A], preferred_element_type=F32)
    sg_ref[...] = g * jax.nn.sigmoid(g)
    u_ref[...] = jnp.dot(xb, wm_bf[:, c0 + 2 * V_A:c0 + 2 * V_A + POOL_CH], preferred_element_type=F32)


def _proj_ab(x, w_main, w_a, w_alpha, b_alpha, tm):
    m = x.shape[0]
    nmain = w_main.shape[1]
    row = lambda w: pl.BlockSpec((tm, w), lambda i: (i, 0))
    return pl.pallas_call(
        _proj_ab_kernel,
        grid=(m // tm,),
        in_specs=[row(D_MODEL), _full((D_MODEL, nmain)), _full((D_MODEL, GLA_LOWRANK)),
                  _full((GLA_LOWRANK, QK_A)), _full((1, QK_A))],
        out_specs=[row(3 * QK_A), row(V_A), row(V_A), row(POOL_CH)],
        out_shape=[jax.ShapeDtypeStruct((m, 3 * QK_A), F32), jax.ShapeDtypeStruct((m, V_A), F32),
                   jax.ShapeDtypeStruct((m, V_A), F32), jax.ShapeDtypeStruct((m, POOL_CH), F32)],
        scratch_shapes=[pltpu.VMEM((D_MODEL, nmain), BF16), pltpu.VMEM((D_MODEL, GLA_LOWRANK), BF16)],
        compiler_params=_params(1),
        name="proj_ab",
    )(x, w_main, w_a, w_alpha, b_alpha)


def _gla_tables():
    c = GLA_CHUNK
    i = np.arange(c)[:, None]
    s = np.arange(c)[None, :]
    mats = [(s <= i), (s > i)]
    for lev in range(GLA_LEVELS):
        p = GLA_LEVELS - 1 - lev
        half = 1 << p
        start = (i >> (p + 1)) << (p + 1)
        mid = start + half - 1
        upper = i >= start + half
        mats.append(np.where(upper, (s > mid) & (s <= i), (s > i) & (s <= mid)))
    seg = np.concatenate(mats, axis=0).astype(np.float32)
    j = np.arange(c)[None, :]
    x = i ^ j
    lvl = np.full((c, c), GLA_LEVELS + 1, np.int32)
    lvl[np.arange(c), np.arange(c)] = GLA_LEVELS
    for lev in range(GLA_LEVELS):
        p = GLA_LEVELS - 1 - lev
        lvl = np.where(((x >> p) == 1) & (((i >> p) & 1) == 1), lev, lvl)
    lvl4 = np.tile(lvl, (GLA_HEADS, 1)).astype(np.int32)
    lane_head = (np.arange(QK_A) // GLA_DK)[None, :]
    row_head = (np.arange(GLA_HEADS * c) // c)[:, None]
    hm4 = (lane_head == row_head).astype(np.float32)
    return seg, lvl4, hm4


def _gla_chunk(q, k, la, v, s_all, seg, lvl4, hm4):
    c = GLA_CHUNK
    e = sum(jnp.dot(seg, part.astype(BF16), preferred_element_type=F32) for part in _split3(la))
    w = jnp.exp(e)
    w_b = w[0:c]
    w_k = w[c:2 * c]

    def stack_heads(t):
        return (jnp.concatenate([t] * GLA_HEADS, axis=0) * hm4).astype(BF16)

    o_inter = jnp.dot(stack_heads(q * w_b), s_all.astype(BF16), preferred_element_type=F32)
    a = jnp.zeros((GLA_HEADS * c, c), F32)
    for lev in range(GLA_LEVELS + 1):
        if lev < GLA_LEVELS:
            w_l = w[(2 + lev) * c:(3 + lev) * c]
            ql, kl = q * w_l, k * w_l
        else:
            ql, kl = q, k
        p_l = lax.dot_general(stack_heads(ql), kl.astype(BF16), (((1,), (1,)), ((), ())),
                              preferred_element_type=F32)
        a = jnp.where(lvl4 == lev, p_l, a)
    a = a.astype(BF16)
    outs = []
    for h in range(GLA_HEADS):
        v_h = v[:, h * GLA_DV:(h + 1) * GLA_DV].astype(BF16)
        o_h = o_inter[h * c:(h + 1) * c] + jnp.dot(a[h * c:(h + 1) * c], v_h, preferred_element_type=F32)
        outs.append(o_h)
    ks_t = (k * w_k).T.astype(BF16)
    kv = jnp.dot(ks_t, v.astype(BF16), preferred_element_type=F32)
    dec = jnp.exp(jnp.sum(la.T, axis=1, keepdims=True))
    new_rows = []
    for h in range(GLA_HEADS):
        rows = slice(h * GLA_DK, (h + 1) * GLA_DK)
        new_rows.append(dec[rows] * s_all[rows] + kv[rows, h * GLA_DV:(h + 1) * GLA_DV])
    return outs, jnp.concatenate(new_rows, axis=0)


def _gla_finish(o_h, norm_g, sg_h):
    o_h = o_h * lax.rsqrt(jnp.mean(o_h * o_h, axis=-1, keepdims=True) + LN_EPS) * norm_g
    return o_h * sg_h


def _gla_prompt_kernel(qkl_ref, v_ref, sg_ref, ng_ref, seg_ref, lvl_ref, hm_ref,
                       mix_ref, state_ref, s_scr):
    t = pl.program_id(1)

    @pl.when(t == 0)
    def _():
        s_scr[...] = jnp.zeros_like(s_scr)

    mix_ref[:, V_A:] = jnp.zeros((mix_ref.shape[0], mix_ref.shape[1] - V_A), F32)
    seg = seg_ref[...]
    lvl4 = lvl_ref[...]
    hm4 = hm_ref[...]
    norm_g = ng_ref[...]
    n_chunks = qkl_ref.shape[0] // GLA_CHUNK

    s_all = s_scr[...]
    for ci in range(n_chunks):
        rows = pl.ds(ci * GLA_CHUNK, GLA_CHUNK)
        q = qkl_ref[rows, 0:QK_A]
        k = qkl_ref[rows, QK_A:2 * QK_A]
        la = qkl_ref[rows, 2 * QK_A:3 * QK_A]
        v = v_ref[rows, :]
        outs, s_all = _gla_chunk(q, k, la, v, s_all, seg, lvl4, hm4)
        if ci == n_chunks - 1:
            s_scr[...] = s_all
        for h in range(GLA_HEADS):
            cols = slice(h * GLA_DV, (h + 1) * GLA_DV)
            mix_ref[rows, cols] = _gla_finish(outs[h], norm_g, sg_ref[rows, cols])

    @pl.when(t == pl.num_programs(1) - 1)
    def _():
        state_ref[0] = s_scr[...]


def _gla_prompt(qkl, v, sg, norm_g, batch, seq, tm):
    m = qkl.shape[0]
    nt = seq // tm
    seg, lvl4, hm4 = _gla_tables()
    row = lambda w: pl.BlockSpec((tm, w), lambda b, t: (b * nt + t, 0))
    return pl.pallas_call(
        _gla_prompt_kernel,
        grid=(batch, nt),
        in_specs=[row(3 * QK_A), row(V_A), row(V_A), _full((1, GLA_DV)),
                  _full(seg.shape), _full(lvl4.shape), _full(hm4.shape)],
        out_specs=[pl.BlockSpec((tm, D_MODEL), lambda b, t: (b * nt + t, 0)),
                   pl.BlockSpec((1, QK_A, GLA_DV), lambda b, t: (b, 0, 0))],
        out_shape=[jax.ShapeDtypeStruct((m, D_MODEL), F32),
                   jax.ShapeDtypeStruct((batch, QK_A, GLA_DV), F32)],
        scratch_shapes=[pltpu.VMEM((QK_A, GLA_DV), F32)],
        compiler_params=_params(2),
        name="gla_prompt",
    )(qkl, v, sg, norm_g, jnp.asarray(seg, BF16), jnp.asarray(lvl4), jnp.asarray(hm4))


GLA_SAMPLE_ROWS = 8


def _gla_sample_kernel(qkl_ref, v_ref, sg_ref, ng_ref, s0_ref, mix_ref, s1_ref):
    mix_ref[:, V_A:] = jnp.zeros((mix_ref.shape[0], mix_ref.shape[1] - V_A), F32)
    ones = jnp.ones((QK_A, GLA_DV), F32)
    r_i = lax.broadcasted_iota(jnp.int32, (QK_A, QK_A), 0)
    c_i = lax.broadcasted_iota(jnp.int32, (QK_A, QK_A), 1)
    eye = r_i == c_i
    norm_g = ng_ref[...]

    def col_bcast(row):
        diag = jnp.where(eye, jnp.broadcast_to(row, (QK_A, QK_A)), 0.0)
        return jnp.dot(diag, ones, precision=HI, preferred_element_type=F32)

    for r in range(GLA_SAMPLE_ROWS):
        q = col_bcast(qkl_ref[r:r + 1, 0:QK_A])
        k = col_bcast(qkl_ref[r:r + 1, QK_A:2 * QK_A])
        dec = jnp.exp(col_bcast(qkl_ref[r:r + 1, 2 * QK_A:3 * QK_A]))
        for h in range(GLA_HEADS):
            rows = slice(h * GLA_DK, (h + 1) * GLA_DK)
            cols = slice(h * GLA_DV, (h + 1) * GLA_DV)
            s_new = dec[rows] * s0_ref[r, rows, :] + k[rows] * v_ref[r:r + 1, cols]
            s1_ref[r, rows, :] = s_new
            o_h = jnp.sum(q[rows] * s_new, axis=0, keepdims=True)
            mix_ref[r:r + 1, cols] = _gla_finish(o_h, norm_g, sg_ref[r:r + 1, cols])


def _gla_sample(qkl, v, sg, norm_g, s0):
    m = qkl.shape[0]
    rb = GLA_SAMPLE_ROWS
    row = lambda w: pl.BlockSpec((rb, w), lambda i: (i, 0))
    st = pl.BlockSpec((rb, QK_A, GLA_DV), lambda i: (i, 0, 0))
    return pl.pallas_call(
        _gla_sample_kernel,
        grid=(m // rb,),
        in_specs=[row(3 * QK_A), row(V_A), row(V_A), _full((1, GLA_DV)), st],
        out_specs=[pl.BlockSpec((rb, D_MODEL), lambda i: (i, 0)), st],
        out_shape=[jax.ShapeDtypeStruct((m, D_MODEL), F32),
                   jax.ShapeDtypeStruct((m, QK_A, GLA_DV), F32)],
        compiler_params=_params(1),
        name="gla_sample",
    )(qkl, v, sg, norm_g, s0)


def _pool_project(d_groups, wp_ref, scale_ref, mix_ref):
    for g in range(len(POOL_WINDOWS)):
        cols = slice(g * POOL_GROUP, (g + 1) * POOL_GROUP)
        mix_ref[:, cols] = _bdot(d_groups[g], wp_ref[g]) * scale_ref[:, cols]


def _pool_prompt_kernel(u_ref, prev_ref, wp_ref, scale_ref, mixin_ref, mix_ref):
    del mixin_ref
    t = pl.program_id(1)
    tm = u_ref.shape[0]
    hist = prev_ref.shape[0]
    prev = jnp.where(t > 0, prev_ref[...], 0.0)
    z = jnp.concatenate([prev, u_ref[...]], axis=0)
    pos = t * tm + lax.broadcasted_iota(jnp.int32, (tm, 1), 0)
    d_groups = []
    for g, w in enumerate(POOL_WINDOWS):
        cols = slice(g * POOL_GROUP, (g + 1) * POOL_GROUP)
        s = z[:, cols]
        shift = 1
        while shift < w:
            s = s + pltpu.roll(s, shift, 0)
            shift *= 2
        cnt = jnp.minimum(w, pos + 1).astype(F32)
        d_groups.append(s[hist:] / cnt - z[hist:, cols])
    _pool_project(d_groups, wp_ref, scale_ref, mix_ref)


def _pool_prompt(u, mix, w_pool, scale, batch, seq, tm):
    nt = seq // tm
    hist = 16
    assert hist > POOL_HIST and tm % hist == 0
    per = tm // hist
    return pl.pallas_call(
        _pool_prompt_kernel,
        grid=(batch, nt),
        in_specs=[pl.BlockSpec((tm, POOL_CH), lambda b, t: (b * nt + t, 0)),
                  pl.BlockSpec((hist, POOL_CH), lambda b, t: (jnp.maximum((b * nt + t) * per - 1, 0), 0)),
                  _full(w_pool.shape), _full((1, POOL_CH)),
                  pl.BlockSpec(memory_space=pl.ANY)],
        out_specs=pl.BlockSpec((tm, POOL_CH), lambda b, t: (b * nt + t, 1)),
        out_shape=jax.ShapeDtypeStruct(mix.shape, F32),
        input_output_aliases={4: 0},
        compiler_params=_params(2),
        name="pool_prompt",
    )(u, u, w_pool, scale, mix)


def _pool_sample_kernel(u_ref, st_ref, wp_ref, scale_ref, mixin_ref, mix_ref, st_out_ref):
    del mixin_ref
    u = u_ref[...]
    d_groups = []
    for g, w in enumerate(POOL_WINDOWS):
        cols = slice(g * POOL_GROUP, (g + 1) * POOL_GROUP)
        win = u[:, cols] + jnp.sum(st_ref[:, POOL_HIST - (w - 1):POOL_HIST, cols], axis=1)
        d_groups.append(win / float(w) - u[:, cols])
    _pool_project(d_groups, wp_ref, scale_ref, mix_ref)
    st_out_ref[:, 0:POOL_HIST - 1, :] = st_ref[:, 1:POOL_HIST, :]
    st_out_ref[:, POOL_HIST - 1:POOL_HIST, :] = u[:, None, :]


def _pool_sample(u, st, mix, w_pool, scale, pos0):
    m = u.shape[0]
    assert pos0 + 1 >= max(POOL_WINDOWS)
    return pl.pallas_call(
        _pool_sample_kernel,
        grid=(1,),
        in_specs=[_full((m, POOL_CH)), _full(st.shape), _full(w_pool.shape), _full((1, POOL_CH)),
                  pl.BlockSpec(memory_space=pl.ANY)],
        out_specs=[pl.BlockSpec((m, POOL_CH), lambda i: (0, 1)), _full(st.shape)],
        out_shape=[jax.ShapeDtypeStruct(mix.shape, F32), jax.ShapeDtypeStruct(st.shape, F32)],
        input_output_aliases={4: 0},
        compiler_params=_params(1),
        name="pool_sample",
    )(u, st, w_pool, scale, mix)


def _sub_tiles(tm):
    ts = SUB_TILE if tm % SUB_TILE == 0 else tm
    return [slice(s, s + ts) for s in range(0, tm, ts)]


def _mix_kernel(x_ref, mix_ref, w_ref, lng_ref, lnb_ref, wr_ref, br_ref, *rest):
    x1_ref, idx_ref, gate_ref, cnt_ref, w_bf = rest[-5:]

    @pl.when(pl.program_id(0) == 0)
    def _():
        w_bf[...] = w_ref[...].astype(BF16)
        cnt_ref[...] = jnp.zeros_like(cnt_ref)

    for rows in _sub_tiles(x_ref.shape[0]):
        h = jnp.dot(mix_ref[rows, :].astype(BF16), w_bf[...], preferred_element_type=F32)
        x1 = _layer_norm(DEEPNORM_ALPHA * x_ref[rows, :] + h, lng_ref[...], lnb_ref[...])
        x1_ref[rows, :] = x1
        logits = _bdot(x1, wr_ref[...]) + br_ref[...]
        ts = logits.shape[0]
        lane = lax.broadcasted_iota(jnp.int32, (ts, N_EXPERTS), 1)
        lane_k = lax.broadcasted_iota(jnp.int32, (ts, TOP_K), 1)
        idx_out = jnp.zeros((ts, TOP_K), jnp.int32)
        val_out = jnp.zeros((ts, TOP_K), F32)
        cur = logits
        chosen = jnp.zeros((ts, N_EXPERTS), F32)
        for kk in range(TOP_K):
            mval = jnp.max(cur, axis=1, keepdims=True)
            midx = jnp.min(jnp.where(cur == mval, lane.astype(F32), float(N_EXPERTS)), axis=1,
                           keepdims=True).astype(jnp.int32)
            idx_out = jnp.where(lane_k == kk, midx, idx_out)
            val_out = jnp.where(lane_k == kk, mval, val_out)
            picked = lane == midx
            cur = jnp.where(picked, -jnp.inf, cur)
            chosen = jnp.where(picked, 1.0, chosen)
        cnt_ref[...] += jnp.sum(chosen, axis=0, keepdims=True)
        ex = jnp.exp(val_out - val_out[:, 0:1])
        idx_ref[rows, :] = idx_out
        gate_ref[rows, :] = ex / jnp.sum(ex, axis=1, keepdims=True)


def _mix(x, mix, w_out, ln_g, ln_b, w_router, b_router, tm, x1_buf, row0):
    m = x.shape[0]
    n_total = x1_buf.shape[0]
    assert row0 % tm == 0 and row0 + m <= n_total
    off_blocks = row0 // tm
    row = lambda w: pl.BlockSpec((tm, w), lambda i: (i, 0))
    in_specs = [row(D_MODEL), row(D_MODEL), _full((D_MODEL, D_MODEL)), _full((1, D_MODEL)),
                _full((1, D_MODEL)), _full((D_MODEL, N_EXPERTS)), _full((1, N_EXPERTS)),
                pl.BlockSpec(memory_space=pl.ANY)]
    args = [x, mix, w_out, ln_g, ln_b, w_router, b_router, x1_buf]
    aliases = {len(args) - 1: 0}
    return pl.pallas_call(
        _mix_kernel,
        grid=(m // tm,),
        in_specs=in_specs,
        out_specs=[pl.BlockSpec((tm, D_MODEL), lambda i: (i + off_blocks, 0)), row(TOP_K), row(TOP_K),
                   _full((1, N_EXPERTS))],
        out_shape=[jax.ShapeDtypeStruct((n_total, D_MODEL), F32),
                   jax.ShapeDtypeStruct((m, TOP_K), jnp.int32), jax.ShapeDtypeStruct((m, TOP_K), F32),
                   jax.ShapeDtypeStruct((1, N_EXPERTS), F32)],
        scratch_shapes=[pltpu.VMEM((D_MODEL, D_MODEL), BF16)],
        input_output_aliases=aliases,
        compiler_params=_params(1),
        name="mix",
    )(*args)


def _rank_kernel(idx_ref, pstart_ref, rank_ref, carry):
    i = pl.program_id(0)

    @pl.when(i == 0)
    def _():
        carry[...] = pstart_ref[...].astype(F32)

    idx = idx_ref[...]
    tm = idx.shape[0]
    lane = lax.broadcasted_iota(jnp.int32, (tm, N_EXPERTS), 1)
    onehots = [(idx[:, kk:kk + 1] == lane) for kk in range(TOP_K)]
    member = sum(jnp.where(o, 1.0, 0.0) for o in onehots)
    r_i = lax.broadcasted_iota(jnp.int32, (tm, tm), 0)
    c_i = lax.broadcasted_iota(jnp.int32, (tm, tm), 1)
    strict_lower = jnp.where(c_i < r_i, 1.0, 0.0).astype(BF16)
    before = jnp.dot(strict_lower, member.astype(BF16), preferred_element_type=F32) + carry[...]
    lane_k = lax.broadcasted_iota(jnp.int32, (tm, TOP_K), 1)
    rank = jnp.zeros((tm, TOP_K), F32)
    for kk in range(TOP_K):
        r_k = jnp.sum(jnp.where(onehots[kk], before, 0.0), axis=1, keepdims=True)
        rank = jnp.where(lane_k == kk, r_k, rank)
    rank_ref[...] = rank.astype(jnp.int32)
    carry[...] = carry[...] + jnp.sum(member, axis=0, keepdims=True)


def _rank(idx, pstart):
    n = idx.shape[0]
    tm = RANK_TILE
    return pl.pallas_call(
        _rank_kernel,
        grid=(n // tm,),
        in_specs=[pl.BlockSpec((tm, TOP_K), lambda i: (i, 0)), _full((1, N_EXPERTS))],
        out_specs=pl.BlockSpec((tm, TOP_K), lambda i: (i, 0)),
        out_shape=jax.ShapeDtypeStruct((n, TOP_K), jnp.int32),
        scratch_shapes=[pltpu.VMEM((1, N_EXPERTS), F32)],
        compiler_params=_params(1),
        name="moe_rank",
    )(idx, pstart.reshape(1, N_EXPERTS))


def _expert_kernel(be_ref, nu_ref, xs_ref, wgu_ref, bgu_ref, wd_ref, bd_ref, ys_ref, wgu_bf, wd_bf):
    blk = pl.program_id(0)
    prev = be_ref[jnp.maximum(blk - 1, 0)]
    fresh = jnp.logical_or(blk == 0, be_ref[blk] != prev)
    used = blk < nu_ref[0]

    @pl.when(jnp.logical_and(fresh, used))
    def _():
        wgu_bf[...] = wgu_ref[0, 0].astype(BF16)
        wd_bf[...] = wd_ref[0, 0].astype(BF16)

    @pl.when(used)
    def _():
        xb = xs_ref[...].astype(BF16)
        acc = jnp.zeros(ys_ref.shape, F32) + bd_ref[0, 0]
        half = D_EXPERT // 2
        for c in range(2):
            cg = slice(c * half, (c + 1) * half)
            cu = slice(D_EXPERT + c * half, D_EXPERT + (c + 1) * half)
            gate = jnp.dot(xb, wgu_bf[:, cg], preferred_element_type=F32) + bgu_ref[0, 0, :, cg]
            up = jnp.dot(xb, wgu_bf[:, cu], preferred_element_type=F32) + bgu_ref[0, 0, :, cu]
            gate = jnp.minimum(gate, SWIGLU_LIMIT)
            up = jnp.clip(up, -SWIGLU_LIMIT, SWIGLU_LIMIT)
            act = (up + 1.0) * gate * jax.nn.sigmoid(SWIGLU_ALPHA * gate)
            acc = acc + jnp.dot(act.astype(BF16), wd_bf[cg, :], preferred_element_type=F32)
        ys_ref[...] = acc

    @pl.when(jnp.logical_not(used))
    def _():
        ys_ref[...] = jnp.zeros_like(ys_ref)


def _experts(layer, block_e, n_used, xs, w_gu, b_gu, w_down, b_down):
    p = xs.shape[0]
    bm = MOE_BLOCK
    nb = p // bm
    depth = w_gu.shape[0]
    last = lambda blk, nu: jnp.minimum(blk, nu[0] - 1)
    grid_spec = pltpu.PrefetchScalarGridSpec(
        num_scalar_prefetch=2,
        grid=(nb,),
        in_specs=[pl.BlockSpec((bm, D_MODEL), lambda blk, be, nu: (last(blk, nu), 0)),
                  pl.BlockSpec((1, 1, D_MODEL, 2 * D_EXPERT), lambda blk, be, nu: (layer, be[blk], 0, 0)),
                  pl.BlockSpec((1, 1, 1, 2 * D_EXPERT), lambda blk, be, nu: (layer, be[blk], 0, 0)),
                  pl.BlockSpec((1, 1, D_EXPERT, D_MODEL), lambda blk, be, nu: (layer, be[blk], 0, 0)),
                  pl.BlockSpec((1, 1, 1, D_MODEL), lambda blk, be, nu: (layer, be[blk], 0, 0))],
        out_specs=pl.BlockSpec((bm, D_MODEL), lambda blk, be, nu: (blk, 0)),
        scratch_shapes=[pltpu.VMEM((D_MODEL, 2 * D_EXPERT), BF16), pltpu.VMEM((D_EXPERT, D_MODEL), BF16)],
    )
    return pl.pallas_call(
        _expert_kernel,
        grid_spec=grid_spec,
        out_shape=jax.ShapeDtypeStruct((p, D_MODEL), F32),
        compiler_params=_params(1),
        name="moe_experts",
    )(block_e, n_used, xs, w_gu, b_gu.reshape(depth, N_EXPERTS, 1, 2 * D_EXPERT), w_down,
      b_down.reshape(depth, N_EXPERTS, 1, D_MODEL))


def _finish_kernel(x1_ref, y0_ref, y1_ref, y2_ref, y3_ref, gate_ref, p_ref, lng_ref, lnb_ref,
                   wpg_ref, wpp_ref, out_ref, wpg_bf):
    @pl.when(pl.program_id(0) == 0)
    def _():
        wpg_bf[...] = wpg_ref[...].astype(BF16)

    for rows in _sub_tiles(out_ref.shape[0]):
        gates = gate_ref[rows, :]
        moe = y0_ref[rows, :] * gates[:, 0:1]
        for kk, y_ref in enumerate((y1_ref, y2_ref, y3_ref), start=1):
            moe = moe + y_ref[rows, :] * gates[:, kk:kk + 1]
        x2 = _layer_norm(DEEPNORM_ALPHA * x1_ref[rows, :] + moe, lng_ref[...], lnb_ref[...])
        pg = jax.nn.sigmoid(jnp.dot(x2.astype(BF16), wpg_bf[...], preferred_element_type=F32))
        pp = _bdot(p_ref[rows, :], wpp_ref[...])
        out_ref[rows, :] = x2 + pg * pp


def _finish(x1_all, row0, m, ysg, gates, p_all, layer, ln_g, ln_b, w_pg, w_pp, tm):
    nt = m // tm
    assert row0 % tm == 0
    row = lambda w: pl.BlockSpec((tm, w), lambda i: (i, 0))
    ple = p_all.shape[2]
    y_specs = [pl.BlockSpec((tm, D_MODEL), lambda i, kk=kk: (kk * nt + i, 0)) for kk in range(TOP_K)]
    x1_spec = pl.BlockSpec((tm, D_MODEL), lambda i: (i + row0 // tm, 0))
    p_spec = pl.BlockSpec((None, tm, ple), lambda i: (layer, i, 0))
    return pl.pallas_call(
        _finish_kernel,
        grid=(nt,),
        in_specs=[x1_spec] + y_specs + [row(TOP_K), p_spec, _full((1, D_MODEL)), _full((1, D_MODEL)),
                                        _full((D_MODEL, D_MODEL)), _full((ple, D_MODEL))],
        out_specs=row(D_MODEL),
        out_shape=jax.ShapeDtypeStruct((m, D_MODEL), F32),
        scratch_shapes=[pltpu.VMEM((D_MODEL, D_MODEL), BF16)],
        compiler_params=_params(1),
        name="finish",
    )(x1_all, ysg, ysg, ysg, ysg, gates, p_all, ln_g, ln_b, w_pg, w_pp)


def _qkv_kernel(x_ref, w_ref, q_ref, k_ref, v_ref, w_bf):
    @pl.when(pl.program_id(0) == 0)
    def _():
        w_bf[...] = w_ref[...].astype(BF16)

    xb = x_ref[...].astype(BF16)
    q_ref[...] = jnp.dot(xb, w_bf[:, 0:D_MODEL], preferred_element_type=F32) * (MOBA_DH ** -0.5)
    k_ref[...] = jnp.dot(xb, w_bf[:, D_MODEL:2 * D_MODEL], preferred_element_type=F32)
    v_ref[...] = jnp.dot(xb, w_bf[:, 2 * D_MODEL:3 * D_MODEL], preferred_element_type=F32)


def _qkv(x, w_qkv, tm):
    m = x.shape[0]
    row = pl.BlockSpec((tm, D_MODEL), lambda i: (i, 0))
    shp = jax.ShapeDtypeStruct((m, D_MODEL), F32)
    return pl.pallas_call(
        _qkv_kernel,
        grid=(m // tm,),
        in_specs=[row, _full((D_MODEL, 3 * D_MODEL))],
        out_specs=[row, row, row],
        out_shape=[shp, shp, shp],
        scratch_shapes=[pltpu.VMEM((D_MODEL, 3 * D_MODEL), BF16)],
        compiler_params=_params(1),
        name="qkv",
    )(x, w_qkv)


def _topk_mask(gate, n_valid, axis):
    nb = gate.shape[axis]
    pos_i = lax.broadcasted_iota(jnp.int32, gate.shape, axis)
    valid = pos_i < n_valid
    pos = pos_i.astype(F32)
    cur = jnp.where(valid, gate, -jnp.inf)
    sel = jnp.zeros(gate.shape, F32)
    for _ in range(min(MOBA_TOPK, nb)):
        best = jnp.max(cur, axis=axis, keepdims=True)
        first = jnp.min(jnp.where(cur == best, pos, float(nb)), axis=axis, keepdims=True)
        pick = jnp.logical_and(pos == first, valid)
        sel = jnp.where(pick, 1.0, sel)
        cur = jnp.where(pick, -jnp.inf, cur)
    return sel


MOBA_AUG_MASK0 = 8
MOBA_NEG = -1e30


def _split3(x):
    hi = x.astype(BF16).astype(F32)
    mid = (x - hi).astype(BF16).astype(F32)
    lo = (x - hi - mid).astype(BF16).astype(F32)
    return hi, mid, lo


def _moba_prompt_kernel(q_ref, k_ref, v_ref, slope_ref, o_ref, qt_scr, vtp_scr, ka_scr, km_scr, s_scr):
    hp = pl.program_id(1)
    qt = pl.program_id(2)
    nb = km_scr.shape[0]
    blk = MOBA_BLOCK
    qw = 2 * blk
    dh = MOBA_DH
    pair = 2 * dh

    @pl.when(qt == 0)
    def _():
        lane = lax.broadcasted_iota(jnp.int32, (blk, pair), 1)
        key_off = lax.broadcasted_iota(jnp.int32, (blk, pair), 0).astype(F32)
        for n in range(nb):
            rows = slice(n * blk, (n + 1) * blk)
            half = slice((n % 2) * blk, (n % 2 + 1) * blk)
            qt_scr[n // 2, :, half] = q_ref[rows, :].T.astype(BF16)
            vtp_scr[n // 2, :, half] = v_ref[rows, :].T.astype(BF16)
            kblk = k_ref[rows, :]
            km_scr[n:n + 1, :] = jnp.mean(kblk, axis=0, keepdims=True)
            for hh in range(2):
                a = lane - (1 - hh) * dh
                aug = jnp.where(a < 3, key_off,
                                jnp.where(a < 6, float(n * blk),
                                          jnp.where(a == MOBA_AUG_MASK0 + n, 1.0, 0.0)))
                in_head = jnp.logical_and(lane >= hh * dh, lane < (hh + 1) * dh)
                ka_scr[hh, rows, :] = jnp.where(in_head, kblk, aug).astype(BF16)

    key_i = lax.broadcasted_iota(jnp.int32, (qw, qw), 0)
    qry_i = lax.broadcasted_iota(jnp.int32, (qw, qw), 1)
    same_block = (key_i >= blk) == (qry_i >= blk)
    causal = jnp.logical_and(same_block, key_i <= qry_i)
    first_key_second_qry = jnp.logical_and(key_i < blk, qry_i >= blk)
    col = lax.broadcasted_iota(jnp.int32, (1, qw), 1)
    cur_blk = 2 * qt + jnp.where(col >= blk, 1, 0)
    blk_row = lax.broadcasted_iota(jnp.int32, (nb, qw), 0)

    own = pl.ds(pl.multiple_of(qt * qw, qw), qw)
    q_t = qt_scr[qt].astype(F32)
    v_t_d = vtp_scr[qt]
    lane_k = lax.broadcasted_iota(jnp.int32, (nb, pair), 1)
    r8 = lax.broadcasted_iota(jnp.int32, (8, qw), 0)
    is_hi = jnp.logical_or(r8 == 0, r8 == 3)
    is_mid = jnp.logical_or(r8 == 1, r8 == 4)
    pad_rows = jnp.zeros((dh - MOBA_AUG_MASK0 - nb, qw), F32)

    q_aug, init = [], []
    for hh in range(2):
        hrows = slice(hh * dh, (hh + 1) * dh)
        s_hi, s_mid, s_lo = _split3(slope_ref[pl.ds(2 * hp + hh, 1), :])
        slope_rows = jnp.where(is_hi, s_hi, jnp.where(is_mid, s_mid, s_lo))
        slope_rows = jnp.where(r8 < 6, slope_rows, 0.0)

        def with_aug(aug_rows, hh=hh, hrows=hrows):
            parts = [q_t[hrows], aug_rows] if hh == 0 else [aug_rows, q_t[hrows]]
            return jnp.concatenate(parts, axis=0).astype(BF16)

        q_diag = with_aug(jnp.concatenate([slope_rows, jnp.zeros((dh - 8, qw), F32)], axis=0))
        in_head = jnp.logical_and(lane_k >= hh * dh, lane_k < (hh + 1) * dh)
        km_h = jnp.where(in_head, km_scr[...], 0.0).astype(BF16)
        gate_t = jnp.dot(km_h, q_diag, preferred_element_type=F32)
        sel = _topk_mask(gate_t, cur_blk, 0)
        mask_rows = jnp.where(sel > 0.0, 0.0, MOBA_NEG)
        q_aug.append(with_aug(jnp.concatenate([slope_rows, mask_rows, pad_rows], axis=0)))

        first_sel = jnp.sum(jnp.where(blk_row == 2 * qt, sel, 0.0), axis=0, keepdims=True)
        allowed = jnp.logical_or(causal, jnp.logical_and(first_key_second_qry, first_sel > 0.0))
        s = jnp.dot(ka_scr[hh, own, :], q_diag, preferred_element_type=F32)
        s = jnp.where(allowed, s, -jnp.inf)
        m0 = jnp.max(s, axis=0, keepdims=True)
        p = jnp.exp(s - m0)
        l0 = jnp.sum(p, axis=0, keepdims=True)
        acc0 = jnp.dot(v_t_d[hrows, :], p.astype(BF16), preferred_element_type=F32)
        init += [m0, l0, acc0]

    def scores(pair_idx, slot):
        keys = pl.ds(pl.multiple_of(pair_idx * qw, qw), qw)
        col_max = []
        for hh in range(2):
            s = jnp.dot(ka_scr[hh, keys, :], q_aug[hh], preferred_element_type=F32)
            s_scr[slot, hh] = s
            col_max.append(jnp.max(s, axis=0, keepdims=True))
        return col_max

    n_pairs = qt

    def body(jj, carry):
        slot = jj % 2
        v_pair = vtp_scr[jj]
        out = []
        for hh in range(2):
            m, l, acc, mx = carry[4 * hh:4 * hh + 4]
            m_new = jnp.maximum(m, mx)
            alpha = jnp.exp(m - m_new)
            p = jnp.exp(s_scr[slot, hh] - m_new)
            l = alpha * l + jnp.sum(p, axis=0, keepdims=True)
            acc = alpha * acc + jnp.dot(v_pair[hh * dh:(hh + 1) * dh, :], p.astype(BF16),
                                        preferred_element_type=F32)
            out.append([m_new, l, acc])
        nxt = scores(jnp.minimum(jj + 1, nb // 2 - 1), 1 - slot)
        return tuple(out[0] + [nxt[0]] + out[1] + [nxt[1]])

    mx0 = scores(0, 0)
    res = lax.fori_loop(0, n_pairs, body, tuple(init[0:3] + [mx0[0]] + init[3:6] + [mx0[1]]))
    o_ref[...] = jnp.concatenate([res[2] / res[1], res[6] / res[5]], axis=0).T


def _moba_prompt(q, k, v, slopes, batch, seq):
    m = q.shape[0]
    nb = seq // MOBA_BLOCK
    assert nb % 2 == 0 and MOBA_AUG_MASK0 + nb <= MOBA_DH
    pair = 2 * MOBA_DH
    n_pairs = MOBA_HEADS // 2
    qw = 2 * MOBA_BLOCK
    nt = nb // 2
    slope_rows = jnp.broadcast_to(slopes[:, None], (MOBA_HEADS, qw))
    seq_spec = pl.BlockSpec((seq, pair), lambda b, hp, qt: (b, hp))
    return pl.pallas_call(
        _moba_prompt_kernel,
        grid=(batch, n_pairs, nt),
        in_specs=[seq_spec, seq_spec, seq_spec, _full((MOBA_HEADS, qw))],
        out_specs=pl.BlockSpec((qw, pair), lambda b, hp, qt: (b * nt + qt, hp)),
        out_shape=jax.ShapeDtypeStruct((m, D_MODEL), F32),
        scratch_shapes=[pltpu.VMEM((nt, pair, qw), BF16), pltpu.VMEM((nt, pair, qw), BF16),
                        pltpu.VMEM((2, seq, pair), BF16), pltpu.VMEM((nb, pair), F32),
                        pltpu.VMEM((2, 2, qw, qw), F32)],
        compiler_params=_params(3),
        name="moba_prompt",
    )(q, k, v, slope_rows)


def _moba_sample_kernel(n_pages, pos0, pt_ref, q_ref, kn_ref, vn_ref, slope_ref, *refs):
    del pt_ref
    k_pages = refs[:n_pages]
    v_pages = refs[n_pages:2 * n_pages]
    o_ref = refs[2 * n_pages]
    per_blk = MOBA_BLOCK // PAGE_SIZE
    n_blk = n_pages // per_blk
    h_i = lax.broadcasted_iota(jnp.int32, (MOBA_HEADS, D_MODEL), 0)
    d_i = lax.broadcasted_iota(jnp.int32, (MOBA_HEADS, D_MODEL), 1)
    own_head = d_i // MOBA_DH == h_i
    q_bd = jnp.where(own_head, q_ref[0], 0.0)
    q_bd16 = q_bd.astype(BF16)
    slope = slope_ref[...]

    lane_n = lax.broadcasted_iota(jnp.int32, (D_MODEL, n_blk), 1)
    means = jnp.zeros((D_MODEL, n_blk), F32)
    for n in range(n_blk):
        tot = jnp.sum(sum(k_pages[n * per_blk + i][...] for i in range(per_blk)), axis=1, keepdims=True)
        means = jnp.where(lane_n == n, tot / float(MOBA_BLOCK), means)
    gate = jnp.dot(q_bd16, means.astype(BF16), preferred_element_type=F32)
    sel = _topk_mask(gate, n_blk, 1)

    lane = lax.broadcasted_iota(jnp.int32, (1, PAGE_SIZE), 1)
    scores = []
    for pg in range(n_pages):
        s = jnp.dot(q_bd16, k_pages[pg][...].astype(BF16), preferred_element_type=F32)
        dist = (pos0 - pg * PAGE_SIZE - lane).astype(F32)
        s = s - slope * dist
        n = pg // per_blk
        scores.append(jnp.where(sel[:, n:n + 1] > 0.0, s, -jnp.inf))
    k_new = kn_ref[0].astype(BF16).astype(F32)
    s_new = jnp.sum(q_bd16.astype(F32) * k_new, axis=1, keepdims=True)
    m = s_new
    for s in scores:
        m = jnp.maximum(m, jnp.max(s, axis=1, keepdims=True))
    p_new = jnp.exp(s_new - m)
    l = p_new
    d_e = lax.broadcasted_iota(jnp.int32, (D_MODEL, MOBA_HEADS), 0)
    h_e = lax.broadcasted_iota(jnp.int32, (D_MODEL, MOBA_HEADS), 1)
    expand = jnp.where(d_e // MOBA_DH == h_e, 1.0, 0.0).astype(BF16)
    acc = jnp.zeros((D_MODEL, PAGE_SIZE), F32)
    for pg in range(n_pages):
        p = jnp.exp(scores[pg] - m)
        l = l + jnp.sum(p, axis=1, keepdims=True)
        acc = acc + jnp.dot(expand, p.astype(BF16), preferred_element_type=F32) * v_pages[pg][...]
    o_past = jnp.sum(acc.T, axis=0, keepdims=True)
    row_of = lambda col: jnp.sum(jnp.where(own_head, col, 0.0), axis=0, keepdims=True)
    o_new = row_of(p_new.astype(BF16).astype(F32)) * vn_ref[0]
    o_ref[0] = (o_past + o_new) / row_of(l)


def _moba_sample(q, k_new, v_new, cache_k, cache_v, page_table, slopes, pos0):
    m = q.shape[0]
    n_pages = page_table.shape[1]
    assert pos0 == n_pages * PAGE_SIZE and pos0 % MOBA_BLOCK == 0
    n_phys = cache_k.shape[0]
    ck = jnp.transpose(cache_k, (0, 2, 3, 1)).reshape(n_phys, D_MODEL, PAGE_SIZE)
    cv = jnp.transpose(cache_v, (0, 2, 3, 1)).reshape(n_phys, D_MODEL, PAGE_SIZE)
    vec = pl.BlockSpec((1, 1, D_MODEL), lambda b, pt: (b, 0, 0))
    page_specs = [pl.BlockSpec((None, D_MODEL, PAGE_SIZE), lambda b, pt, pg=pg: (pt[b, pg], 0, 0))
                  for pg in range(n_pages)]
    grid_spec = pltpu.PrefetchScalarGridSpec(
        num_scalar_prefetch=1,
        grid=(m,),
        in_specs=[vec, vec, vec, pl.BlockSpec((MOBA_HEADS, 1), lambda b, pt: (0, 0))] + page_specs + page_specs,
        out_specs=vec,
    )
    out = pl.pallas_call(
        functools.partial(_moba_sample_kernel, n_pages, pos0),
        grid_spec=grid_spec,
        out_shape=jax.ShapeDtypeStruct((m, 1, D_MODEL), F32),
        compiler_params=_params(1),
        name="moba_sample",
    )(page_table, q.reshape(m, 1, D_MODEL), k_new.reshape(m, 1, D_MODEL), v_new.reshape(m, 1, D_MODEL),
      slopes.reshape(MOBA_HEADS, 1), *([ck] * n_pages), *([cv] * n_pages))
    return out.reshape(m, D_MODEL)


def _layer_tail(layer, xp, xs, mix_p, mix_s, p_p, p_s, w_out, ln_g, ln_b, w_router, b_router,
                w_gu, b_gu, w_down, b_down, w_pg, w_pp):
    n_p, n_s = xp.shape[0], xs.shape[0]
    lg0, lb0 = ln_g[0:1], ln_b[0:1]
    lg1, lb1 = ln_g[1:2], ln_b[1:2]
    br = b_router.reshape(1, N_EXPERTS)
    x1 = jnp.zeros((n_p + n_s, D_MODEL), F32)
    x1, idxp, gatep, cnt_p = _mix(xp, mix_p, w_out, lg0, lb0, w_router, br, ROW_TILE, x1, 0)
    x1, idxs, gates, cnt_s = _mix(xs, mix_s, w_out, lg0, lb0, w_router, br, n_s, x1, n_p)

    idx = jnp.concatenate([idxp, idxs], axis=0)
    counts = (cnt_p + cnt_s)[0].astype(jnp.int32)
    bm = MOE_BLOCK
    n_tok = n_p + n_s
    n_blocks = (n_tok * TOP_K + N_EXPERTS * (bm - 1)) // bm
    padded = (counts + bm - 1) // bm * bm
    pend = jnp.cumsum(padded)
    pstart = pend - padded
    dest = _rank(idx, pstart)
    n_used = (pend[-1] // bm).astype(jnp.int32)
    blk_ids = jnp.minimum(jnp.arange(n_blocks, dtype=jnp.int32), n_used - 1)
    block_e = jnp.sum((pend[None, :] <= (blk_ids * bm)[:, None]).astype(jnp.int32), axis=1)
    block_e = jnp.minimum(block_e, N_EXPERTS - 1)
    nk = n_tok * TOP_K
    low_bits = (nk - 1).bit_length()
    assert (N_EXPERTS << low_bits) < 2 ** 31
    order = jnp.sort((idx.reshape(-1) << low_bits) + jnp.arange(nk, dtype=jnp.int32)) & ((1 << low_bits) - 1)
    start = jnp.cumsum(counts) - counts
    r_in_group = (blk_ids * bm - pstart[block_e])[:, None] + jnp.arange(bm, dtype=jnp.int32)[None, :]
    live = r_in_group < counts[block_e][:, None]
    src = jnp.where(live, start[block_e][:, None] + r_in_group, 0)
    row_tok = jnp.where(live, order.at[src].get(mode="promise_in_bounds") // TOP_K, 0).reshape(-1)
    gather = lambda src, rows: src.at[rows].get(mode="promise_in_bounds")
    xs_rows = gather(x1, row_tok)
    ys = _experts(layer, block_e, n_used.reshape(1), xs_rows, w_gu, b_gu, w_down, b_down)
    ysg_p = gather(ys, dest[:n_p].T.reshape(-1))
    ysg_s = gather(ys, dest[n_p:].T.reshape(-1))
    yp = _finish(x1, 0, n_p, ysg_p, gatep, p_p, layer, lg1, lb1, w_pg, w_pp, ROW_TILE)
    ys_out = _finish(x1, n_p, n_s, ysg_s, gates, p_s, layer, lg1, lb1, w_pg, w_pp, n_s)
    return yp, ys_out


def kernel(x_prompt, x_sample, state_gla, state_pool, cache_k, cache_v, page_table, p_prompt, p_sample, w_in_ab, gla_w_alpha, gla_b_alpha, gla_norm_g, pool_w, pool_scale, w_out_ab, w_qkv_c, w_out_c, ln_g, ln_b, moe_w_router, moe_b_router, moe_w_gu, moe_b_gu, moe_w_down, moe_b_down, ple_w_gate, ple_w_proj):
    batch, seq, _ = x_prompt.shape
    n_s = x_sample.shape[0]
    n_p = batch * seq
    pos0 = page_table.shape[1] * PAGE_SIZE
    xp = x_prompt.reshape(n_p, D_MODEL)
    xs = x_sample.reshape(n_s, D_MODEL)

    def tail(i, xp, xs, mix_p, mix_s, w_out):
        depth = p_prompt.shape[0]
        return _layer_tail(i, xp, xs, mix_p, mix_s, p_prompt.reshape(depth, n_p, -1), p_sample.reshape(depth, n_s, -1),
                           w_out, ln_g[i], ln_b[i], moe_w_router[i], moe_b_router[i], moe_w_gu,
                           moe_b_gu, moe_w_down, moe_b_down, ple_w_gate[i], ple_w_proj[i])

    w_in = w_in_ab[0]
    c_a = 2 * QK_A + 2 * V_A
    w_main = jnp.concatenate([w_in[:, :c_a], w_in[:, c_a + GLA_LOWRANK:]], axis=1)
    w_a = w_in[:, c_a:c_a + GLA_LOWRANK]
    b_alpha = gla_b_alpha[0].reshape(1, QK_A)
    norm_g = gla_norm_g[0].reshape(1, GLA_DV)
    scale = pool_scale[0].reshape(1, POOL_CH)
    qkl_p, v_p, sg_p, u_p = _proj_ab(xp, w_main, w_a, gla_w_alpha[0], b_alpha, ROW_TILE)
    qkl_s, v_s, sg_s, u_s = _proj_ab(xs, w_main, w_a, gla_w_alpha[0], b_alpha, n_s)
    mix_p, gla_p = _gla_prompt(qkl_p, v_p, sg_p, norm_g, batch, seq, ROW_TILE)
    mix_p = _pool_prompt(u_p, mix_p, pool_w[0], scale, batch, seq, ROW_TILE)
    mix_s, gla_s = _gla_sample(qkl_s, v_s, sg_s, norm_g, state_gla[0].reshape(n_s, QK_A, GLA_DV))
    mix_s, pool_s = _pool_sample(u_s, state_pool[0], mix_s, pool_w[0], scale, pos0)
    pool_p = u_p.reshape(batch, seq, POOL_CH)[:, seq - POOL_HIST:]
    xp, xs = tail(0, xp, xs, mix_p, mix_s, w_out_ab[0])

    slopes = jnp.exp2(-8.0 * jnp.arange(1, MOBA_HEADS + 1, dtype=F32) / MOBA_HEADS)
    q_p, k_p, v_p2 = _qkv(xp, w_qkv_c[0], ROW_TILE)
    q_s, k_s, v_s2 = _qkv(xs, w_qkv_c[0], n_s)
    o_p = _moba_prompt(q_p, k_p, v_p2, slopes, batch, seq)
    o_s = _moba_sample(q_s, k_s, v_s2, cache_k[0], cache_v[0], page_table, slopes, pos0)
    xp, xs = tail(1, xp, xs, o_p, o_s, w_out_c[0])

    hd = (MOBA_HEADS, MOBA_DH)
    return (xp.reshape(batch, seq, D_MODEL), xs.reshape(n_s, 1, D_MODEL),
            gla_p.reshape(1, batch, GLA_HEADS, GLA_DK, GLA_DV), gla_s.reshape(1, n_s, GLA_HEADS, GLA_DK, GLA_DV),
            pool_p[None], pool_s[None],
            k_p.reshape(1, batch, seq, *hd), v_p2.reshape(1, batch, seq, *hd),
            k_s.reshape(1, n_s, 1, *hd), v_s2.reshape(1, n_s, 1, *hd))
```

```python
import functools

import numpy as np
import jax
import jax.numpy as jnp
from jax import lax
from jax.experimental import pallas as pl
from jax.experimental.pallas import tpu as pltpu

F32 = jnp.float32
BF16 = jnp.bfloat16

D_MODEL = 1024
GLA_HEADS = 4
GLA_DK = 64
GLA_DV = 128
GLA_LOWRANK = 16
GLA_TAU = 16.0
QK_A = GLA_HEADS * GLA_DK
V_A = GLA_HEADS * GLA_DV
POOL_WINDOWS = (2, 4, 8, 16)
POOL_CH = 512
POOL_GROUP = 128
POOL_HIST = 15
MOBA_HEADS = 16
MOBA_DH = 64
MOBA_BLOCK = 256
MOBA_TOPK = 3
PAGE_SIZE = 128
N_EXPERTS = 32
TOP_K = 4
D_EXPERT = 1024
SWIGLU_LIMIT = 7.0
SWIGLU_ALPHA = 1.702
DEPTH = 2
DEEPNORM_ALPHA = (2 * DEPTH) ** 0.25
LN_EPS = 1e-5

ROW_TILE = 512
SUB_TILE = 256
GLA_CHUNK = 128
GLA_LEVELS = 7
MOE_BLOCK = 512
RANK_TILE = 384
VMEM_LIMIT = 56 * 1024 * 1024


def _params(n_axes, vmem=VMEM_LIMIT):
    return pltpu.CompilerParams(dimension_semantics=("arbitrary",) * n_axes, vmem_limit_bytes=vmem)


def _bdot(a, b):
    return jnp.dot(a.astype(BF16), b.astype(BF16), preferred_element_type=F32)


def _layer_norm(y, g, b):
    mu = jnp.mean(y, axis=-1, keepdims=True)
    yc = y - mu
    var = jnp.mean(yc * yc, axis=-1, keepdims=True)
    return yc * lax.rsqrt(var + LN_EPS) * g + b


def _full(shape):
    n = len(shape)
    return pl.BlockSpec(shape, lambda *_: (0,) * n)


def _proj_ab_kernel(x_ref, wm_ref, wa_ref, walpha_ref, balpha_ref,
                    qkl_ref, v_ref, sg_ref, u_ref, wm_bf, wa_bf):
    @pl.when(pl.program_id(0) == 0)
    def _():
        wm_bf[...] = wm_ref[...].astype(BF16)
        wa_bf[...] = wa_ref[...].astype(BF16)

    c0 = 2 * QK_A
    for rows in _sub_tiles(x_ref.shape[0]):
        xb = x_ref[rows, :].astype(BF16)
        qk = jnp.dot(xb, wm_bf[:, 0:2 * QK_A], preferred_element_type=F32)
        qkl_ref[rows, 0:QK_A] = qk[:, 0:QK_A] * (GLA_DK ** -0.5)
        qkl_ref[rows, QK_A:2 * QK_A] = qk[:, QK_A:2 * QK_A]
        a_lr = jnp.dot(xb, wa_bf[...], preferred_element_type=F32)
        z = _bdot(a_lr, walpha_ref[...]) + balpha_ref[...]
        log_sig = jnp.minimum(z, 0.0) - jnp.log1p(jnp.exp(-jnp.abs(z)))
        qkl_ref[rows, 2 * QK_A:3 * QK_A] = log_sig / GLA_TAU
        v_ref[rows, :] = jnp.dot(xb, wm_bf[:, c0:c0 + V_A], preferred_element_type=F32)
        g = jnp.dot(xb, wm_bf[:, c0 + V_A:c0 + 2 * V_A], preferred_element_type=F32)
        sg_ref[rows, :] = g * jax.nn.sigmoid(g)
        u_ref[rows, :] = jnp.dot(xb, wm_bf[:, c0 + 2 * V_A:c0 + 2 * V_A + POOL_CH],
                                 preferred_element_type=F32)


def _proj_ab(x, w_main, w_a, w_alpha, b_alpha, tm):
    m = x.shape[0]
    nmain = w_main.shape[1]
    row = lambda w: pl.BlockSpec((tm, w), lambda i: (i, 0))
    return pl.pallas_call(
        _proj_ab_kernel,
        grid=(m // tm,),
        in_specs=[row(D_MODEL), _full((D_MODEL, nmain)), _full((D_MODEL, GLA_LOWRANK)),
                  _full((GLA_LOWRANK, QK_A)), _full((1, QK_A))],
        out_specs=[row(3 * QK_A), row(V_A), row(V_A), row(POOL_CH)],
        out_shape=[jax.ShapeDtypeStruct((m, 3 * QK_A), F32), jax.ShapeDtypeStruct((m, V_A), F32),
                   jax.ShapeDtypeStruct((m, V_A), F32), jax.ShapeDtypeStruct((m, POOL_CH), F32)],
        scratch_shapes=[pltpu.VMEM((D_MODEL, nmain), BF16), pltpu.VMEM((D_MODEL, GLA_LOWRANK), BF16)],
        compiler_params=_params(1),
        name="proj_ab",
    )(x, w_main, w_a, w_alpha, b_alpha)


def _gla_tables():
    c = GLA_CHUNK
    i = np.arange(c)[:, None]
    s = np.arange(c)[None, :]
    mats = [(s <= i), (s > i)]
    for lev in range(GLA_LEVELS):
        p = GLA_LEVELS - 1 - lev
        half = 1 << p
        start = (i >> (p + 1)) << (p + 1)
        mid = start + half - 1
        upper = i >= start + half
        mats.append(np.where(upper, (s > mid) & (s <= i), (s > i) & (s <= mid)))
    seg = np.concatenate(mats, axis=0).astype(np.float32)
    j = np.arange(c)[None, :]
    x = i ^ j
    lvl = np.full((c, c), GLA_LEVELS + 1, np.int32)
    lvl[np.arange(c), np.arange(c)] = GLA_LEVELS
    for lev in range(GLA_LEVELS):
        p = GLA_LEVELS - 1 - lev
        lvl = np.where(((x >> p) == 1) & (((i >> p) & 1) == 1), lev, lvl)
    lvl4 = np.tile(lvl, (GLA_HEADS, 1)).astype(np.int32)
    lane_head = (np.arange(QK_A) // GLA_DK)[None, :]
    row_head = (np.arange(GLA_HEADS * c) // c)[:, None]
    hm4 = (lane_head == row_head).astype(np.float32)
    return seg, lvl4, hm4


def _gla_chunk(q, k, la, v, s_all, seg, lvl4, hm4):
    c = GLA_CHUNK
    e = sum(jnp.dot(seg, part.astype(BF16), preferred_element_type=F32) for part in _split3(la))
    w = jnp.exp(e)
    w_b = w[0:c]
    w_k = w[c:2 * c]

    def stack_heads(t):
        return (jnp.concatenate([t] * GLA_HEADS, axis=0) * hm4).astype(BF16)

    o_inter = jnp.dot(stack_heads(q * w_b), s_all.astype(BF16), preferred_element_type=F32)
    a = jnp.zeros((GLA_HEADS * c, c), F32)
    for lev in range(GLA_LEVELS + 1):
        if lev < GLA_LEVELS:
            w_l = w[(2 + lev) * c:(3 + lev) * c]
            ql, kl = q * w_l, k * w_l
        else:
            ql, kl = q, k
        p_l = lax.dot_general(stack_heads(ql), kl.astype(BF16), (((1,), (1,)), ((), ())),
                              preferred_element_type=F32)
        a = jnp.where(lvl4 == lev, p_l, a)
    a = a.astype(BF16)
    outs = []
    for h in range(GLA_HEADS):
        v_h = v[:, h * GLA_DV:(h + 1) * GLA_DV].astype(BF16)
        o_h = o_inter[h * c:(h + 1) * c] + jnp.dot(a[h * c:(h + 1) * c], v_h, preferred_element_type=F32)
        outs.append(o_h)
    ks_t = (k * w_k).T.astype(BF16)
    kv = jnp.dot(ks_t, v.astype(BF16), preferred_element_type=F32)
    dec = jnp.exp(jnp.sum(la.T, axis=1, keepdims=True))
    new_rows = []
    for h in range(GLA_HEADS):
        rows = slice(h * GLA_DK, (h + 1) * GLA_DK)
        new_rows.append(dec[rows] * s_all[rows] + kv[rows, h * GLA_DV:(h + 1) * GLA_DV])
    return outs, jnp.concatenate(new_rows, axis=0)


def _gla_finish(o_h, norm_g, sg_h):
    o_h = o_h * lax.rsqrt(jnp.mean(o_h * o_h, axis=-1, keepdims=True) + LN_EPS) * norm_g
    return o_h * sg_h


def _gla_prompt_kernel(qkl_ref, v_ref, sg_ref, ng_ref, seg_ref, lvl_ref, hm_ref,
                       mix_ref, state_ref, s_scr):
    t = pl.program_id(1)

    @pl.when(t == 0)
    def _():
        s_scr[...] = jnp.zeros_like(s_scr)

    mix_ref[:, V_A:] = jnp.zeros((mix_ref.shape[0], mix_ref.shape[1] - V_A), F32)
    seg = seg_ref[...]
    lvl4 = lvl_ref[...]
    hm4 = hm_ref[...]
    norm_g = ng_ref[...]
    n_chunks = qkl_ref.shape[0] // GLA_CHUNK

    s_all = s_scr[...]
    for ci in range(n_chunks):
        rows = pl.ds(ci * GLA_CHUNK, GLA_CHUNK)
        q = qkl_ref[rows, 0:QK_A]
        k = qkl_ref[rows, QK_A:2 * QK_A]
        la = qkl_ref[rows, 2 * QK_A:3 * QK_A]
        v = v_ref[rows, :]
        outs, s_all = _gla_chunk(q, k, la, v, s_all, seg, lvl4, hm4)
        if ci == n_chunks - 1:
            s_scr[...] = s_all
        for h in range(GLA_HEADS):
            cols = slice(h * GLA_DV, (h + 1) * GLA_DV)
            mix_ref[rows, cols] = _gla_finish(outs[h], norm_g, sg_ref[rows, cols])

    @pl.when(t == pl.num_programs(1) - 1)
    def _():
        state_ref[0] = s_scr[...]


def _gla_prompt(qkl, v, sg, norm_g, batch, seq, tm):
    m = qkl.shape[0]
    nt = seq // tm
    seg, lvl4, hm4 = _gla_tables()
    row = lambda w: pl.BlockSpec((tm, w), lambda b, t: (b * nt + t, 0))
    return pl.pallas_call(
        _gla_prompt_kernel,
        grid=(batch, nt),
        in_specs=[row(3 * QK_A), row(V_A), row(V_A), _full((1, GLA_DV)),
                  _full(seg.shape), _full(lvl4.shape), _full(hm4.shape)],
        out_specs=[pl.BlockSpec((tm, D_MODEL), lambda b, t: (b * nt + t, 0)),
                   pl.BlockSpec((1, QK_A, GLA_DV), lambda b, t: (b, 0, 0))],
        out_shape=[jax.ShapeDtypeStruct((m, D_MODEL), F32),
                   jax.ShapeDtypeStruct((batch, QK_A, GLA_DV), F32)],
        scratch_shapes=[pltpu.VMEM((QK_A, GLA_DV), F32)],
        compiler_params=_params(2),
        name="gla_prompt",
    )(qkl, v, sg, norm_g, jnp.asarray(seg, BF16), jnp.asarray(lvl4), jnp.asarray(hm4))


GLA_SAMPLE_ROWS = 8


def _gla_sample_kernel(qkl_ref, v_ref, sg_ref, ng_ref, s0_ref, mix_ref, s1_ref):
    mix_ref[:, V_A:] = jnp.zeros((mix_ref.shape[0], mix_ref.shape[1] - V_A), F32)
    ones = jnp.ones((QK_A, GLA_DV), BF16)
    r_i = lax.broadcasted_iota(jnp.int32, (QK_A, QK_A), 0)
    c_i = lax.broadcasted_iota(jnp.int32, (QK_A, QK_A), 1)
    eye = r_i == c_i
    norm_g = ng_ref[...]

    def col_bcast(row):
        out = None
        for part in _split3(row):
            diag = jnp.where(eye, jnp.broadcast_to(part, (QK_A, QK_A)), 0.0).astype(BF16)
            term = jnp.dot(diag, ones, preferred_element_type=F32)
            out = term if out is None else out + term
        return out

    for r in range(GLA_SAMPLE_ROWS):
        q = col_bcast(qkl_ref[r:r + 1, 0:QK_A])
        k = col_bcast(qkl_ref[r:r + 1, QK_A:2 * QK_A])
        dec = jnp.exp(col_bcast(qkl_ref[r:r + 1, 2 * QK_A:3 * QK_A]))
        for h in range(GLA_HEADS):
            rows = slice(h * GLA_DK, (h + 1) * GLA_DK)
            cols = slice(h * GLA_DV, (h + 1) * GLA_DV)
            s_new = dec[rows] * s0_ref[r, rows, :] + k[rows] * v_ref[r:r + 1, cols]
            s1_ref[r, rows, :] = s_new
            o_h = jnp.sum(q[rows] * s_new, axis=0, keepdims=True)
            mix_ref[r:r + 1, cols] = _gla_finish(o_h, norm_g, sg_ref[r:r + 1, cols])


def _gla_sample(qkl, v, sg, norm_g, s0):
    m = qkl.shape[0]
    rb = GLA_SAMPLE_ROWS
    row = lambda w: pl.BlockSpec((rb, w), lambda i: (i, 0))
    st = pl.BlockSpec((rb, QK_A, GLA_DV), lambda i: (i, 0, 0))
    return pl.pallas_call(
        _gla_sample_kernel,
        grid=(m // rb,),
        in_specs=[row(3 * QK_A), row(V_A), row(V_A), _full((1, GLA_DV)), st],
        out_specs=[pl.BlockSpec((rb, D_MODEL), lambda i: (i, 0)), st],
        out_shape=[jax.ShapeDtypeStruct((m, D_MODEL), F32),
                   jax.ShapeDtypeStruct((m, QK_A, GLA_DV), F32)],
        compiler_params=_params(1),
        name="gla_sample",
    )(qkl, v, sg, norm_g, s0)


def _pool_project(d_groups, wp_ref, scale_ref, mix_ref):
    for g in range(len(POOL_WINDOWS)):
        cols = slice(g * POOL_GROUP, (g + 1) * POOL_GROUP)
        mix_ref[:, cols] = _bdot(d_groups[g], wp_ref[g]) * scale_ref[:, cols]


def _pool_prompt_kernel(u_ref, prev_ref, wp_ref, scale_ref, mixin_ref, mix_ref):
    del mixin_ref
    t = pl.program_id(1)
    tm = u_ref.shape[0]
    hist = prev_ref.shape[0]
    prev = jnp.where(t > 0, prev_ref[...], 0.0)
    z = jnp.concatenate([prev, u_ref[...]], axis=0)
    pos = t * tm + lax.broadcasted_iota(jnp.int32, (tm, 1), 0)
    d_groups = []
    for g, w in enumerate(POOL_WINDOWS):
        cols = slice(g * POOL_GROUP, (g + 1) * POOL_GROUP)
        s = z[:, cols]
        shift = 1
        while shift < w:
            s = s + pltpu.roll(s, shift, 0)
            shift *= 2
        cnt = jnp.minimum(w, pos + 1).astype(F32)
        d_groups.append(s[hist:] / cnt - z[hist:, cols])
    _pool_project(d_groups, wp_ref, scale_ref, mix_ref)


def _pool_prompt(u, mix, w_pool, scale, batch, seq, tm):
    nt = seq // tm
    hist = 16
    assert hist > POOL_HIST and tm % hist == 0
    per = tm // hist
    return pl.pallas_call(
        _pool_prompt_kernel,
        grid=(batch, nt),
        in_specs=[pl.BlockSpec((tm, POOL_CH), lambda b, t: (b * nt + t, 0)),
                  pl.BlockSpec((hist, POOL_CH), lambda b, t: (jnp.maximum((b * nt + t) * per - 1, 0), 0)),
                  _full(w_pool.shape), _full((1, POOL_CH)),
                  pl.BlockSpec(memory_space=pl.ANY)],
        out_specs=pl.BlockSpec((tm, POOL_CH), lambda b, t: (b * nt + t, 1)),
        out_shape=jax.ShapeDtypeStruct(mix.shape, F32),
        input_output_aliases={4: 0},
        compiler_params=_params(2),
        name="pool_prompt",
    )(u, u, w_pool, scale, mix)


def _pool_sample_kernel(u_ref, st_ref, wp_ref, scale_ref, mixin_ref, mix_ref, st_out_ref):
    del mixin_ref
    u = u_ref[...]
    d_groups = []
    for g, w in enumerate(POOL_WINDOWS):
        cols = slice(g * POOL_GROUP, (g + 1) * POOL_GROUP)
        win = u[:, cols] + jnp.sum(st_ref[:, POOL_HIST - (w - 1):POOL_HIST, cols], axis=1)
        d_groups.append(win / float(w) - u[:, cols])
    _pool_project(d_groups, wp_ref, scale_ref, mix_ref)
    st_out_ref[:, 0:POOL_HIST - 1, :] = st_ref[:, 1:POOL_HIST, :]
    st_out_ref[:, POOL_HIST - 1:POOL_HIST, :] = u[:, None, :]


def _pool_sample(u, st, mix, w_pool, scale, pos0):
    m = u.shape[0]
    assert pos0 + 1 >= max(POOL_WINDOWS)
    return pl.pallas_call(
        _pool_sample_kernel,
        grid=(1,),
        in_specs=[_full((m, POOL_CH)), _full(st.shape), _full(w_pool.shape), _full((1, POOL_CH)),
                  pl.BlockSpec(memory_space=pl.ANY)],
        out_specs=[pl.BlockSpec((m, POOL_CH), lambda i: (0, 1)), _full(st.shape)],
        out_shape=[jax.ShapeDtypeStruct(mix.shape, F32), jax.ShapeDtypeStruct(st.shape, F32)],
        input_output_aliases={4: 0},
        compiler_params=_params(1),
        name="pool_sample",
    )(u, st, w_pool, scale, mix)


def _sub_tiles(tm):
    ts = SUB_TILE if tm % SUB_TILE == 0 else tm
    return [slice(s, s + ts) for s in range(0, tm, ts)]


def _mix_kernel(x_ref, mix_ref, w_ref, lng_ref, lnb_ref, wr_ref, br_ref, *rest):
    x1_ref, idx_ref, gate_ref, cnt_ref, w_bf = rest[-5:]

    @pl.when(pl.program_id(0) == 0)
    def _():
        w_bf[...] = w_ref[...].astype(BF16)
        cnt_ref[...] = jnp.zeros_like(cnt_ref)

    for rows in _sub_tiles(x_ref.shape[0]):
        h = jnp.dot(mix_ref[rows, :].astype(BF16), w_bf[...], preferred_element_type=F32)
        x1 = _layer_norm(DEEPNORM_ALPHA * x_ref[rows, :] + h, lng_ref[...], lnb_ref[...])
        x1_ref[rows, :] = x1
        logits = _bdot(x1, wr_ref[...]) + br_ref[...]
        ts = logits.shape[0]
        lane = lax.broadcasted_iota(jnp.int32, (ts, N_EXPERTS), 1)
        lane_k = lax.broadcasted_iota(jnp.int32, (ts, TOP_K), 1)
        idx_out = jnp.zeros((ts, TOP_K), jnp.int32)
        val_out = jnp.zeros((ts, TOP_K), F32)
        cur = logits
        chosen = jnp.zeros((ts, N_EXPERTS), F32)
        for kk in range(TOP_K):
            mval = jnp.max(cur, axis=1, keepdims=True)
            midx = jnp.min(jnp.where(cur == mval, lane.astype(F32), float(N_EXPERTS)), axis=1,
                           keepdims=True).astype(jnp.int32)
            idx_out = jnp.where(lane_k == kk, midx, idx_out)
            val_out = jnp.where(lane_k == kk, mval, val_out)
            picked = lane == midx
            cur = jnp.where(picked, -jnp.inf, cur)
            chosen = jnp.where(picked, 1.0, chosen)
        cnt_ref[...] += jnp.sum(chosen, axis=0, keepdims=True)
        ex = jnp.exp(val_out - val_out[:, 0:1])
        idx_ref[rows, :] = idx_out
        gate_ref[rows, :] = ex / jnp.sum(ex, axis=1, keepdims=True)


def _mix(x, mix, w_out, ln_g, ln_b, w_router, b_router, tm, x1_buf, row0):
    m = x.shape[0]
    n_total = x1_buf.shape[0]
    assert row0 % tm == 0 and row0 + m <= n_total
    off_blocks = row0 // tm
    row = lambda w: pl.BlockSpec((tm, w), lambda i: (i, 0))
    in_specs = [row(D_MODEL), row(D_MODEL), _full((D_MODEL, D_MODEL)), _full((1, D_MODEL)),
                _full((1, D_MODEL)), _full((D_MODEL, N_EXPERTS)), _full((1, N_EXPERTS)),
                pl.BlockSpec(memory_space=pl.ANY)]
    args = [x, mix, w_out, ln_g, ln_b, w_router, b_router, x1_buf]
    aliases = {len(args) - 1: 0}
    return pl.pallas_call(
        _mix_kernel,
        grid=(m // tm,),
        in_specs=in_specs,
        out_specs=[pl.BlockSpec((tm, D_MODEL), lambda i: (i + off_blocks, 0)), row(TOP_K), row(TOP_K),
                   _full((1, N_EXPERTS))],
        out_shape=[jax.ShapeDtypeStruct((n_total, D_MODEL), F32),
                   jax.ShapeDtypeStruct((m, TOP_K), jnp.int32), jax.ShapeDtypeStruct((m, TOP_K), F32),
                   jax.ShapeDtypeStruct((1, N_EXPERTS), F32)],
        scratch_shapes=[pltpu.VMEM((D_MODEL, D_MODEL), BF16)],
        input_output_aliases=aliases,
        compiler_params=_params(1),
        name="mix",
    )(*args)


def _rank_kernel(idx_ref, pstart_ref, rank_ref, carry):
    i = pl.program_id(0)

    @pl.when(i == 0)
    def _():
        carry[...] = pstart_ref[...].astype(F32)

    idx = idx_ref[...]
    tm = idx.shape[0]
    lane = lax.broadcasted_iota(jnp.int32, (tm, N_EXPERTS), 1)
    onehots = [(idx[:, kk:kk + 1] == lane) for kk in range(TOP_K)]
    member = sum(jnp.where(o, 1.0, 0.0) for o in onehots)
    r_i = lax.broadcasted_iota(jnp.int32, (tm, tm), 0)
    c_i = lax.broadcasted_iota(jnp.int32, (tm, tm), 1)
    strict_lower = jnp.where(c_i < r_i, 1.0, 0.0).astype(BF16)
    before = jnp.dot(strict_lower, member.astype(BF16), preferred_element_type=F32) + carry[...]
    lane_k = lax.broadcasted_iota(jnp.int32, (tm, TOP_K), 1)
    rank = jnp.zeros((tm, TOP_K), F32)
    for kk in range(TOP_K):
        r_k = jnp.sum(jnp.where(onehots[kk], before, 0.0), axis=1, keepdims=True)
        rank = jnp.where(lane_k == kk, r_k, rank)
    rank_ref[...] = rank.astype(jnp.int32)
    carry[...] = carry[...] + jnp.sum(member, axis=0, keepdims=True)


def _rank(idx, pstart):
    n = idx.shape[0]
    tm = RANK_TILE
    return pl.pallas_call(
        _rank_kernel,
        grid=(n // tm,),
        in_specs=[pl.BlockSpec((tm, TOP_K), lambda i: (i, 0)), _full((1, N_EXPERTS))],
        out_specs=pl.BlockSpec((tm, TOP_K), lambda i: (i, 0)),
        out_shape=jax.ShapeDtypeStruct((n, TOP_K), jnp.int32),
        scratch_shapes=[pltpu.VMEM((1, N_EXPERTS), F32)],
        compiler_params=_params(1),
        name="moe_rank",
    )(idx, pstart.reshape(1, N_EXPERTS))


def _expert_kernel(be_ref, nu_ref, xs_ref, wgu_ref, bgu_ref, wd_ref, bd_ref, ys_ref, wgu_bf, wd_bf):
    blk = pl.program_id(0)
    prev = be_ref[jnp.maximum(blk - 1, 0)]
    fresh = jnp.logical_or(blk == 0, be_ref[blk] != prev)
    used = blk < nu_ref[0]

    @pl.when(jnp.logical_and(fresh, used))
    def _():
        wgu_bf[...] = wgu_ref[0, 0].astype(BF16)
        wd_bf[...] = wd_ref[0, 0].astype(BF16)

    @pl.when(used)
    def _():
        xb = xs_ref[...].astype(BF16)
        acc = jnp.zeros(ys_ref.shape, F32) + bd_ref[0, 0]
        half = D_EXPERT // 2
        for c in range(2):
            cg = slice(c * half, (c + 1) * half)
            cu = slice(D_EXPERT + c * half, D_EXPERT + (c + 1) * half)
            gate = jnp.dot(xb, wgu_bf[:, cg], preferred_element_type=F32) + bgu_ref[0, 0, :, cg]
            up = jnp.dot(xb, wgu_bf[:, cu], preferred_element_type=F32) + bgu_ref[0, 0, :, cu]
            gate = jnp.minimum(gate, SWIGLU_LIMIT)
            up = jnp.clip(up, -SWIGLU_LIMIT, SWIGLU_LIMIT)
            act = (up + 1.0) * gate * jax.nn.sigmoid(SWIGLU_ALPHA * gate)
            acc = acc + jnp.dot(act.astype(BF16), wd_bf[cg, :], preferred_element_type=F32)
        ys_ref[...] = acc

    @pl.when(jnp.logical_not(used))
    def _():
        ys_ref[...] = jnp.zeros_like(ys_ref)


def _experts(layer, block_e, n_used, xs, w_gu, b_gu, w_down, b_down):
    p = xs.shape[0]
    bm = MOE_BLOCK
    nb = p // bm
    depth = w_gu.shape[0]
    last = lambda blk, nu: jnp.minimum(blk, nu[0] - 1)
    grid_spec = pltpu.PrefetchScalarGridSpec(
        num_scalar_prefetch=2,
        grid=(nb,),
        in_specs=[pl.BlockSpec((bm, D_MODEL), lambda blk, be, nu: (last(blk, nu), 0)),
                  pl.BlockSpec((1, 1, D_MODEL, 2 * D_EXPERT), lambda blk, be, nu: (layer, be[blk], 0, 0)),
                  pl.BlockSpec((1, 1, 1, 2 * D_EXPERT), lambda blk, be, nu: (layer, be[blk], 0, 0)),
                  pl.BlockSpec((1, 1, D_EXPERT, D_MODEL), lambda blk, be, nu: (layer, be[blk], 0, 0)),
                  pl.BlockSpec((1, 1, 1, D_MODEL), lambda blk, be, nu: (layer, be[blk], 0, 0))],
        out_specs=pl.BlockSpec((bm, D_MODEL), lambda blk, be, nu: (blk, 0)),
        scratch_shapes=[pltpu.VMEM((D_MODEL, 2 * D_EXPERT), BF16), pltpu.VMEM((D_EXPERT, D_MODEL), BF16)],
    )
    return pl.pallas_call(
        _expert_kernel,
        grid_spec=grid_spec,
        out_shape=jax.ShapeDtypeStruct((p, D_MODEL), F32),
        compiler_params=_params(1),
        name="moe_experts",
    )(block_e, n_used, xs, w_gu, b_gu.reshape(depth, N_EXPERTS, 1, 2 * D_EXPERT), w_down,
      b_down.reshape(depth, N_EXPERTS, 1, D_MODEL))


def _finish_kernel(x1_ref, y0_ref, y1_ref, y2_ref, y3_ref, gate_ref, p_ref, lng_ref, lnb_ref,
                   wpg_ref, wpp_ref, out_ref, wpg_bf):
    @pl.when(pl.program_id(0) == 0)
    def _():
        wpg_bf[...] = wpg_ref[...].astype(BF16)

    for rows in _sub_tiles(out_ref.shape[0]):
        gates = gate_ref[rows, :]
        moe = y0_ref[rows, :] * gates[:, 0:1]
        for kk, y_ref in enumerate((y1_ref, y2_ref, y3_ref), start=1):
            moe = moe + y_ref[rows, :] * gates[:, kk:kk + 1]
        x2 = _layer_norm(DEEPNORM_ALPHA * x1_ref[rows, :] + moe, lng_ref[...], lnb_ref[...])
        pg = jax.nn.sigmoid(jnp.dot(x2.astype(BF16), wpg_bf[...], preferred_element_type=F32))
        pp = _bdot(p_ref[rows, :], wpp_ref[...])
        out_ref[rows, :] = x2 + pg * pp


def _finish(x1_all, row0, m, ysg, gates, p_all, layer, ln_g, ln_b, w_pg, w_pp, tm):
    nt = m // tm
    assert row0 % tm == 0
    row = lambda w: pl.BlockSpec((tm, w), lambda i: (i, 0))
    ple = p_all.shape[2]
    y_specs = [pl.BlockSpec((tm, D_MODEL), lambda i, kk=kk: (kk * nt + i, 0)) for kk in range(TOP_K)]
    x1_spec = pl.BlockSpec((tm, D_MODEL), lambda i: (i + row0 // tm, 0))
    p_spec = pl.BlockSpec((None, tm, ple), lambda i: (layer, i, 0))
    return pl.pallas_call(
        _finish_kernel,
        grid=(nt,),
        in_specs=[x1_spec] + y_specs + [row(TOP_K), p_spec, _full((1, D_MODEL)), _full((1, D_MODEL)),
                                        _full((D_MODEL, D_MODEL)), _full((ple, D_MODEL))],
        out_specs=row(D_MODEL),
        out_shape=jax.ShapeDtypeStruct((m, D_MODEL), F32),
        scratch_shapes=[pltpu.VMEM((D_MODEL, D_MODEL), BF16)],
        compiler_params=_params(1),
        name="finish",
    )(x1_all, ysg, ysg, ysg, ysg, gates, p_all, ln_g, ln_b, w_pg, w_pp)


def _qkv_kernel(x_ref, w_ref, q_ref, k_ref, v_ref, w_bf):
    @pl.when(pl.program_id(0) == 0)
    def _():
        w_bf[...] = w_ref[...].astype(BF16)

    xb = x_ref[...].astype(BF16)
    q_ref[...] = jnp.dot(xb, w_bf[:, 0:D_MODEL], preferred_element_type=F32) * (MOBA_DH ** -0.5)
    k_ref[...] = jnp.dot(xb, w_bf[:, D_MODEL:2 * D_MODEL], preferred_element_type=F32)
    v_ref[...] = jnp.dot(xb, w_bf[:, 2 * D_MODEL:3 * D_MODEL], preferred_element_type=F32)


def _qkv(x, w_qkv, tm):
    m = x.shape[0]
    row = pl.BlockSpec((tm, D_MODEL), lambda i: (i, 0))
    shp = jax.ShapeDtypeStruct((m, D_MODEL), F32)
    return pl.pallas_call(
        _qkv_kernel,
        grid=(m // tm,),
        in_specs=[row, _full((D_MODEL, 3 * D_MODEL))],
        out_specs=[row, row, row],
        out_shape=[shp, shp, shp],
        scratch_shapes=[pltpu.VMEM((D_MODEL, 3 * D_MODEL), BF16)],
        compiler_params=_params(1),
        name="qkv",
    )(x, w_qkv)


def _topk_mask(gate, n_valid, axis):
    nb = gate.shape[axis]
    pos_i = lax.broadcasted_iota(jnp.int32, gate.shape, axis)
    valid = pos_i < n_valid
    pos = pos_i.astype(F32)
    cur = jnp.where(valid, gate, -jnp.inf)
    sel = jnp.zeros(gate.shape, F32)
    for _ in range(min(MOBA_TOPK, nb)):
        best = jnp.max(cur, axis=axis, keepdims=True)
        first = jnp.min(jnp.where(cur == best, pos, float(nb)), axis=axis, keepdims=True)
        pick = jnp.logical_and(pos == first, valid)
        sel = jnp.where(pick, 1.0, sel)
        cur = jnp.where(pick, -jnp.inf, cur)
    return sel


MOBA_AUG_MASK0 = 8
MOBA_NEG = -1e30


def _split3(x):
    hi = x.astype(BF16).astype(F32)
    mid = (x - hi).astype(BF16).astype(F32)
    lo = (x - hi - mid).astype(BF16).astype(F32)
    return hi, mid, lo


def _moba_prompt_kernel(q_ref, k_ref, v_ref, slope_ref, o_ref, qt_scr, vtp_scr, ka_scr, km_scr, s_scr):
    hp = pl.program_id(1)
    qt = pl.program_id(2)
    nb = km_scr.shape[0]
    blk = MOBA_BLOCK
    qw = 2 * blk
    dh = MOBA_DH
    pair = 2 * dh

    @pl.when(qt == 0)
    def _():
        lane = lax.broadcasted_iota(jnp.int32, (blk, pair), 1)
        key_off = lax.broadcasted_iota(jnp.int32, (blk, pair), 0).astype(F32)
        for n in range(nb):
            rows = slice(n * blk, (n + 1) * blk)
            half = slice((n % 2) * blk, (n % 2 + 1) * blk)
            qt_scr[n // 2, :, half] = q_ref[rows, :].T.astype(BF16)
            vtp_scr[n // 2, :, half] = v_ref[rows, :].T.astype(BF16)
            kblk = k_ref[rows, :]
            km_scr[n:n + 1, :] = jnp.mean(kblk, axis=0, keepdims=True)
            for hh in range(2):
                a = lane - (1 - hh) * dh
                aug = jnp.where(a < 3, key_off,
                                jnp.where(a < 6, float(n * blk),
                                          jnp.where(a == MOBA_AUG_MASK0 + n, 1.0, 0.0)))
                in_head = jnp.logical_and(lane >= hh * dh, lane < (hh + 1) * dh)
                ka_scr[hh, rows, :] = jnp.where(in_head, kblk, aug).astype(BF16)

    key_i = lax.broadcasted_iota(jnp.int32, (qw, qw), 0)
    qry_i = lax.broadcasted_iota(jnp.int32, (qw, qw), 1)
    same_block = (key_i >= blk) == (qry_i >= blk)
    causal = jnp.logical_and(same_block, key_i <= qry_i)
    first_key_second_qry = jnp.logical_and(key_i < blk, qry_i >= blk)
    col = lax.broadcasted_iota(jnp.int32, (1, qw), 1)
    cur_blk = 2 * qt + jnp.where(col >= blk, 1, 0)
    blk_row = lax.broadcasted_iota(jnp.int32, (nb, qw), 0)

    own = pl.ds(pl.multiple_of(qt * qw, qw), qw)
    q_t = qt_scr[qt].astype(F32)
    v_t_d = vtp_scr[qt]
    lane_k = lax.broadcasted_iota(jnp.int32, (nb, pair), 1)
    r8 = lax.broadcasted_iota(jnp.int32, (8, qw), 0)
    is_hi = jnp.logical_or(r8 == 0, r8 == 3)
    is_mid = jnp.logical_or(r8 == 1, r8 == 4)
    pad_rows = jnp.zeros((dh - MOBA_AUG_MASK0 - nb, qw), F32)

    q_aug, init = [], []
    for hh in range(2):
        hrows = slice(hh * dh, (hh + 1) * dh)
        s_hi, s_mid, s_lo = _split3(slope_ref[pl.ds(2 * hp + hh, 1), :])
        slope_rows = jnp.where(is_hi, s_hi, jnp.where(is_mid, s_mid, s_lo))
        slope_rows = jnp.where(r8 < 6, slope_rows, 0.0)

        def with_aug(aug_rows, hh=hh, hrows=hrows):
            parts = [q_t[hrows], aug_rows] if hh == 0 else [aug_rows, q_t[hrows]]
            return jnp.concatenate(parts, axis=0).astype(BF16)

        q_diag = with_aug(jnp.concatenate([slope_rows, jnp.zeros((dh - 8, qw), F32)], axis=0))
        in_head = jnp.logical_and(lane_k >= hh * dh, lane_k < (hh + 1) * dh)
        km_h = jnp.where(in_head, km_scr[...], 0.0).astype(BF16)
        gate_t = jnp.dot(km_h, q_diag, preferred_element_type=F32)
        sel = _topk_mask(gate_t, cur_blk, 0)
        mask_rows = jnp.where(sel > 0.0, 0.0, MOBA_NEG)
        q_aug.append(with_aug(jnp.concatenate([slope_rows, mask_rows, pad_rows], axis=0)))

        first_sel = jnp.sum(jnp.where(blk_row == 2 * qt, sel, 0.0), axis=0, keepdims=True)
        allowed = jnp.logical_or(causal, jnp.logical_and(first_key_second_qry, first_sel > 0.0))
        s = jnp.dot(ka_scr[hh, own, :], q_diag, preferred_element_type=F32)
        s = jnp.where(allowed, s, -jnp.inf)
        m0 = jnp.max(s, axis=0, keepdims=True)
        p = jnp.exp(s - m0)
        l0 = jnp.sum(p, axis=0, keepdims=True)
        acc0 = jnp.dot(v_t_d[hrows, :], p.astype(BF16), preferred_element_type=F32)
        init += [m0, l0, acc0]

    def scores(pair_idx, slot):
        keys = pl.ds(pl.multiple_of(pair_idx * qw, qw), qw)
        col_max = []
        for hh in range(2):
            s = jnp.dot(ka_scr[hh, keys, :], q_aug[hh], preferred_element_type=F32)
            s_scr[slot, hh] = s
            col_max.append(jnp.max(s, axis=0, keepdims=True))
        return col_max

    n_pairs = qt

    def body(jj, carry):
        slot = jj % 2
        v_pair = vtp_scr[jj]
        out = []
        for hh in range(2):
            m, l, acc, mx = carry[4 * hh:4 * hh + 4]
            m_new = jnp.maximum(m, mx)
            alpha = jnp.exp(m - m_new)
            p = jnp.exp(s_scr[slot, hh] - m_new)
            l = alpha * l + jnp.sum(p, axis=0, keepdims=True)
            acc = alpha * acc + jnp.dot(v_pair[hh * dh:(hh + 1) * dh, :], p.astype(BF16),
                                        preferred_element_type=F32)
            out.append([m_new, l, acc])
        nxt = scores(jnp.minimum(jj + 1, nb // 2 - 1), 1 - slot)
        return tuple(out[0] + [nxt[0]] + out[1] + [nxt[1]])

    mx0 = scores(0, 0)
    res = lax.fori_loop(0, n_pairs, body, tuple(init[0:3] + [mx0[0]] + init[3:6] + [mx0[1]]))
    o_ref[...] = jnp.concatenate([res[2] / res[1], res[6] / res[5]], axis=0).T


def _moba_prompt(q, k, v, slopes, batch, seq):
    m = q.shape[0]
    nb = seq // MOBA_BLOCK
    assert nb % 2 == 0 and MOBA_AUG_MASK0 + nb <= MOBA_DH
    pair = 2 * MOBA_DH
    n_pairs = MOBA_HEADS // 2
    qw = 2 * MOBA_BLOCK
    nt = nb // 2
    slope_rows = jnp.broadcast_to(slopes[:, None], (MOBA_HEADS, qw))
    seq_spec = pl.BlockSpec((seq, pair), lambda b, hp, qt: (b, hp))
    return pl.pallas_call(
        _moba_prompt_kernel,
        grid=(batch, n_pairs, nt),
        in_specs=[seq_spec, seq_spec, seq_spec, _full((MOBA_HEADS, qw))],
        out_specs=pl.BlockSpec((qw, pair), lambda b, hp, qt: (b * nt + qt, hp)),
        out_shape=jax.ShapeDtypeStruct((m, D_MODEL), F32),
        scratch_shapes=[pltpu.VMEM((nt, pair, qw), BF16), pltpu.VMEM((nt, pair, qw), BF16),
                        pltpu.VMEM((2, seq, pair), BF16), pltpu.VMEM((nb, pair), F32),
                        pltpu.VMEM((2, 2, qw, qw), F32)],
        compiler_params=_params(3),
        name="moba_prompt",
    )(q, k, v, slope_rows)


def _moba_sample_kernel(n_pages, pos0, pt_ref, q_ref, kn_ref, vn_ref, slope_ref, *refs):
    del pt_ref
    k_pages = refs[:n_pages]
    v_pages = refs[n_pages:2 * n_pages]
    o_ref = refs[2 * n_pages]
    per_blk = MOBA_BLOCK // PAGE_SIZE
    n_blk = n_pages // per_blk
    h_i = lax.broadcasted_iota(jnp.int32, (MOBA_HEADS, D_MODEL), 0)
    d_i = lax.broadcasted_iota(jnp.int32, (MOBA_HEADS, D_MODEL), 1)
    own_head = d_i // MOBA_DH == h_i
    q_bd = jnp.where(own_head, q_ref[0], 0.0)
    q_bd16 = q_bd.astype(BF16)
    slope = slope_ref[...]

    lane_n = lax.broadcasted_iota(jnp.int32, (D_MODEL, n_blk), 1)
    means = jnp.zeros((D_MODEL, n_blk), F32)
    for n in range(n_blk):
        tot = jnp.sum(sum(k_pages[n * per_blk + i][...] for i in range(per_blk)), axis=1, keepdims=True)
        means = jnp.where(lane_n == n, tot / float(MOBA_BLOCK), means)
    gate = jnp.dot(q_bd16, means.astype(BF16), preferred_element_type=F32)
    sel = _topk_mask(gate, n_blk, 1)

    lane = lax.broadcasted_iota(jnp.int32, (1, PAGE_SIZE), 1)
    scores = []
    for pg in range(n_pages):
        s = jnp.dot(q_bd16, k_pages[pg][...].astype(BF16), preferred_element_type=F32)
        dist = (pos0 - pg * PAGE_SIZE - lane).astype(F32)
        s = s - slope * dist
        n = pg // per_blk
        scores.append(jnp.where(sel[:, n:n + 1] > 0.0, s, -jnp.inf))
    k_new = kn_ref[0].astype(BF16).astype(F32)
    s_new = jnp.sum(q_bd16.astype(F32) * k_new, axis=1, keepdims=True)
    m = s_new
    for s in scores:
        m = jnp.maximum(m, jnp.max(s, axis=1, keepdims=True))
    p_new = jnp.exp(s_new - m)
    l = p_new
    d_e = lax.broadcasted_iota(jnp.int32, (D_MODEL, MOBA_HEADS), 0)
    h_e = lax.broadcasted_iota(jnp.int32, (D_MODEL, MOBA_HEADS), 1)
    expand = jnp.where(d_e // MOBA_DH == h_e, 1.0, 0.0).astype(BF16)
    acc = jnp.zeros((D_MODEL, PAGE_SIZE), F32)
    for pg in range(n_pages):
        p = jnp.exp(scores[pg] - m)
        l = l + jnp.sum(p, axis=1, keepdims=True)
        acc = acc + jnp.dot(expand, p.astype(BF16), preferred_element_type=F32) * v_pages[pg][...]
    o_past = jnp.sum(acc.T, axis=0, keepdims=True)
    row_of = lambda col: jnp.sum(jnp.where(own_head, col, 0.0), axis=0, keepdims=True)
    o_new = row_of(p_new.astype(BF16).astype(F32)) * vn_ref[0]
    o_ref[0] = (o_past + o_new) / row_of(l)


def _moba_sample(q, k_new, v_new, cache_k, cache_v, page_table, slopes, pos0):
    m = q.shape[0]
    n_pages = page_table.shape[1]
    assert pos0 == n_pages * PAGE_SIZE and pos0 % MOBA_BLOCK == 0
    n_phys = cache_k.shape[0]
    ck = jnp.transpose(cache_k, (0, 2, 3, 1)).reshape(n_phys, D_MODEL, PAGE_SIZE)
    cv = jnp.transpose(cache_v, (0, 2, 3, 1)).reshape(n_phys, D_MODEL, PAGE_SIZE)
    vec = pl.BlockSpec((1, 1, D_MODEL), lambda b, pt: (b, 0, 0))
    page_specs = [pl.BlockSpec((None, D_MODEL, PAGE_SIZE), lambda b, pt, pg=pg: (pt[b, pg], 0, 0))
                  for pg in range(n_pages)]
    grid_spec = pltpu.PrefetchScalarGridSpec(
        num_scalar_prefetch=1,
        grid=(m,),
        in_specs=[vec, vec, vec, pl.BlockSpec((MOBA_HEADS, 1), lambda b, pt: (0, 0))] + page_specs + page_specs,
        out_specs=vec,
    )
    out = pl.pallas_call(
        functools.partial(_moba_sample_kernel, n_pages, pos0),
        grid_spec=grid_spec,
        out_shape=jax.ShapeDtypeStruct((m, 1, D_MODEL), F32),
        compiler_params=_params(1),
        name="moba_sample",
    )(page_table, q.reshape(m, 1, D_MODEL), k_new.reshape(m, 1, D_MODEL), v_new.reshape(m, 1, D_MODEL),
      slopes.reshape(MOBA_HEADS, 1), *([ck] * n_pages), *([cv] * n_pages))
    return out.reshape(m, D_MODEL)


def _layer_tail(layer, xp, xs, mix_p, mix_s, p_p, p_s, w_out, ln_g, ln_b, w_router, b_router,
                w_gu, b_gu, w_down, b_down, w_pg, w_pp):
    n_p, n_s = xp.shape[0], xs.shape[0]
    lg0, lb0 = ln_g[0:1], ln_b[0:1]
    lg1, lb1 = ln_g[1:2], ln_b[1:2]
    br = b_router.reshape(1, N_EXPERTS)
    x1 = jnp.zeros((n_p + n_s, D_MODEL), F32)
    x1, idxp, gatep, cnt_p = _mix(xp, mix_p, w_out, lg0, lb0, w_router, br, ROW_TILE, x1, 0)
    x1, idxs, gates, cnt_s = _mix(xs, mix_s, w_out, lg0, lb0, w_router, br, n_s, x1, n_p)

    idx = jnp.concatenate([idxp, idxs], axis=0)
    counts = (cnt_p + cnt_s)[0].astype(jnp.int32)
    bm = MOE_BLOCK
    n_tok = n_p + n_s
    n_blocks = (n_tok * TOP_K + N_EXPERTS * (bm - 1)) // bm
    padded = (counts + bm - 1) // bm * bm
    pend = jnp.cumsum(padded)
    pstart = pend - padded
    dest = _rank(idx, pstart)
    n_used = (pend[-1] // bm).astype(jnp.int32)
    blk_ids = jnp.minimum(jnp.arange(n_blocks, dtype=jnp.int32), n_used - 1)
    block_e = jnp.sum((pend[None, :] <= (blk_ids * bm)[:, None]).astype(jnp.int32), axis=1)
    block_e = jnp.minimum(block_e, N_EXPERTS - 1)
    nk = n_tok * TOP_K
    low_bits = (nk - 1).bit_length()
    assert (N_EXPERTS << low_bits) < 2 ** 31
    order = jnp.sort((idx.reshape(-1) << low_bits) + jnp.arange(nk, dtype=jnp.int32)) & ((1 << low_bits) - 1)
    start = jnp.cumsum(counts) - counts
    r_in_group = (blk_ids * bm - pstart[block_e])[:, None] + jnp.arange(bm, dtype=jnp.int32)[None, :]
    live = r_in_group < counts[block_e][:, None]
    src = jnp.where(live, start[block_e][:, None] + r_in_group, 0)
    row_tok = jnp.where(live, order.at[src].get(mode="promise_in_bounds") // TOP_K, 0).reshape(-1)
    gather = lambda src, rows: src.at[rows].get(mode="promise_in_bounds")
    xs_rows = gather(x1, row_tok)
    ys = _experts(layer, block_e, n_used.reshape(1), xs_rows, w_gu, b_gu, w_down, b_down)
    ysg_p = gather(ys, dest[:n_p].T.reshape(-1))
    ysg_s = gather(ys, dest[n_p:].T.reshape(-1))
    yp = _finish(x1, 0, n_p, ysg_p, gatep, p_p, layer, lg1, lb1, w_pg, w_pp, ROW_TILE)
    ys_out = _finish(x1, n_p, n_s, ysg_s, gates, p_s, layer, lg1, lb1, w_pg, w_pp, n_s)
    return yp, ys_out


def kernel(x_prompt, x_sample, state_gla, state_pool, cache_k, cache_v, page_table, p_prompt, p_sample, w_in_ab, gla_w_alpha, gla_b_alpha, gla_norm_g, pool_w, pool_scale, w_out_ab, w_qkv_c, w_out_c, ln_g, ln_b, moe_w_router, moe_b_router, moe_w_gu, moe_b_gu, moe_w_down, moe_b_down, ple_w_gate, ple_w_proj):
    batch, seq, _ = x_prompt.shape
    n_s = x_sample.shape[0]
    n_p = batch * seq
    pos0 = page_table.shape[1] * PAGE_SIZE
    xp = x_prompt.reshape(n_p, D_MODEL)
    xs = x_sample.reshape(n_s, D_MODEL)

    def tail(i, xp, xs, mix_p, mix_s, w_out):
        depth = p_prompt.shape[0]
        return _layer_tail(i, xp, xs, mix_p, mix_s, p_prompt.reshape(depth, n_p, -1), p_sample.reshape(depth, n_s, -1),
                           w_out, ln_g[i], ln_b[i], moe_w_router[i], moe_b_router[i], moe_w_gu,
                           moe_b_gu, moe_w_down, moe_b_down, ple_w_gate[i], ple_w_proj[i])

    w_in = w_in_ab[0]
    c_a = 2 * QK_A + 2 * V_A
    w_main = jnp.concatenate([w_in[:, :c_a], w_in[:, c_a + GLA_LOWRANK:]], axis=1)
    w_a = w_in[:, c_a:c_a + GLA_LOWRANK]
    b_alpha = gla_b_alpha[0].reshape(1, QK_A)
    norm_g = gla_norm_g[0].reshape(1, GLA_DV)
    scale = pool_scale[0].reshape(1, POOL_CH)
    qkl_p, v_p, sg_p, u_p = _proj_ab(xp, w_main, w_a, gla_w_alpha[0], b_alpha, ROW_TILE)
    qkl_s, v_s, sg_s, u_s = _proj_ab(xs, w_main, w_a, gla_w_alpha[0], b_alpha, n_s)
    mix_p, gla_p = _gla_prompt(qkl_p, v_p, sg_p, norm_g, batch, seq, ROW_TILE)
    mix_p = _pool_prompt(u_p, mix_p, pool_w[0], scale, batch, seq, ROW_TILE)
    mix_s, gla_s = _gla_sample(qkl_s, v_s, sg_s, norm_g, state_gla[0].reshape(n_s, QK_A, GLA_DV))
    mix_s, pool_s = _pool_sample(u_s, state_pool[0], mix_s, pool_w[0], scale, pos0)
    pool_p = u_p.reshape(batch, seq, POOL_CH)[:, seq - POOL_HIST:]
    xp, xs = tail(0, xp, xs, mix_p, mix_s, w_out_ab[0])

    slopes = jnp.exp2(-8.0 * jnp.arange(1, MOBA_HEADS + 1, dtype=F32) / MOBA_HEADS)
    q_p, k_p, v_p2 = _qkv(xp, w_qkv_c[0], ROW_TILE)
    q_s, k_s, v_s2 = _qkv(xs, w_qkv_c[0], n_s)
    o_p = _moba_prompt(q_p, k_p, v_p2, slopes, batch, seq)
    o_s = _moba_sample(q_s, k_s, v_s2, cache_k[0], cache_v[0], page_table, slopes, pos0)
    xp, xs = tail(1, xp, xs, o_p, o_s, w_out_c[0])

    hd = (MOBA_HEADS, MOBA_DH)
    return (xp.reshape(batch, seq, D_MODEL), xs.reshape(n_s, 1, D_MODEL),
            gla_p.reshape(1, batch, GLA_HEADS, GLA_DK, GLA_DV), gla_s.reshape(1, n_s, GLA_HEADS, GLA_DK, GLA_DV),
            pool_p[None], pool_s[None],
            k_p.reshape(1, batch, seq, *hd), v_p2.reshape(1, batch, seq, *hd),
            k_s.reshape(1, n_s, 1, *hd), v_s2.reshape(1, n_s, 1, *hd))
```

```python
import functools

import numpy as np
import jax
import jax.numpy as jnp
from jax import lax
from jax.experimental import pallas as pl
from jax.experimental.pallas import tpu as pltpu

F32 = jnp.float32
BF16 = jnp.bfloat16

D_MODEL = 1024
GLA_HEADS = 4
GLA_DK = 64
GLA_DV = 128
GLA_LOWRANK = 16
GLA_TAU = 16.0
QK_A = GLA_HEADS * GLA_DK
V_A = GLA_HEADS * GLA_DV
POOL_WINDOWS = (2, 4, 8, 16)
POOL_CH = 512
POOL_GROUP = 128
POOL_HIST = 15
MOBA_HEADS = 16
MOBA_DH = 64
MOBA_BLOCK = 256
MOBA_TOPK = 3
PAGE_SIZE = 128
N_EXPERTS = 32
TOP_K = 4
D_EXPERT = 1024
SWIGLU_LIMIT = 7.0
SWIGLU_ALPHA = 1.702
DEPTH = 2
DEEPNORM_ALPHA = (2 * DEPTH) ** 0.25
LN_EPS = 1e-5

ROW_TILE = 512
SUB_TILE = 256
GLA_CHUNK = 128
GLA_LEVELS = 7
MOE_BLOCK = 512
RANK_TILE = 384
VMEM_LIMIT = 56 * 1024 * 1024


def _params(n_axes, vmem=VMEM_LIMIT):
    return pltpu.CompilerParams(dimension_semantics=("arbitrary",) * n_axes, vmem_limit_bytes=vmem)


def _bdot(a, b):
    return jnp.dot(a.astype(BF16), b.astype(BF16), preferred_element_type=F32)


def _layer_norm(y, g, b):
    mu = jnp.mean(y, axis=-1, keepdims=True)
    yc = y - mu
    var = jnp.mean(yc * yc, axis=-1, keepdims=True)
    return yc * lax.rsqrt(var + LN_EPS) * g + b


def _full(shape):
    n = len(shape)
    return pl.BlockSpec(shape, lambda *_: (0,) * n)


def _proj_ab_kernel(x_ref, wm_ref, wa_ref, walpha_ref, balpha_ref,
                    qkl_ref, v_ref, sg_ref, u_ref, wm_bf, wa_bf):
    @pl.when(pl.program_id(0) == 0)
    def _():
        wm_bf[...] = wm_ref[...].astype(BF16)
        wa_bf[...] = wa_ref[...].astype(BF16)

    c0 = 2 * QK_A
    for rows in _sub_tiles(x_ref.shape[0]):
        xb = x_ref[rows, :].astype(BF16)
        qk = jnp.dot(xb, wm_bf[:, 0:2 * QK_A], preferred_element_type=F32)
        qkl_ref[rows, 0:QK_A] = qk[:, 0:QK_A] * (GLA_DK ** -0.5)
        qkl_ref[rows, QK_A:2 * QK_A] = qk[:, QK_A:2 * QK_A]
        a_lr = jnp.dot(xb, wa_bf[...], preferred_element_type=F32)
        z = _bdot(a_lr, walpha_ref[...]) + balpha_ref[...]
        log_sig = jnp.minimum(z, 0.0) - jnp.log1p(jnp.exp(-jnp.abs(z)))
        qkl_ref[rows, 2 * QK_A:3 * QK_A] = log_sig / GLA_TAU
        v_ref[rows, :] = jnp.dot(xb, wm_bf[:, c0:c0 + V_A], preferred_element_type=F32)
        g = jnp.dot(xb, wm_bf[:, c0 + V_A:c0 + 2 * V_A], preferred_element_type=F32)
        sg_ref[rows, :] = g * jax.nn.sigmoid(g)
        u_ref[rows, :] = jnp.dot(xb, wm_bf[:, c0 + 2 * V_A:c0 + 2 * V_A + POOL_CH],
                                 preferred_element_type=F32)


def _proj_ab(x, w_main, w_a, w_alpha, b_alpha, tm):
    m = x.shape[0]
    nmain = w_main.shape[1]
    row = lambda w: pl.BlockSpec((tm, w), lambda i: (i, 0))
    return pl.pallas_call(
        _proj_ab_kernel,
        grid=(m // tm,),
        in_specs=[row(D_MODEL), _full((D_MODEL, nmain)), _full((D_MODEL, GLA_LOWRANK)),
                  _full((GLA_LOWRANK, QK_A)), _full((1, QK_A))],
        out_specs=[row(3 * QK_A), row(V_A), row(V_A), row(POOL_CH)],
        out_shape=[jax.ShapeDtypeStruct((m, 3 * QK_A), F32), jax.ShapeDtypeStruct((m, V_A), F32),
                   jax.ShapeDtypeStruct((m, V_A), F32), jax.ShapeDtypeStruct((m, POOL_CH), F32)],
        scratch_shapes=[pltpu.VMEM((D_MODEL, nmain), BF16), pltpu.VMEM((D_MODEL, GLA_LOWRANK), BF16)],
        compiler_params=_params(1),
        name="proj_ab",
    )(x, w_main, w_a, w_alpha, b_alpha)


def _gla_tables():
    c = GLA_CHUNK
    i = np.arange(c)[:, None]
    s = np.arange(c)[None, :]
    mats = [(s <= i), (s > i)]
    for lev in range(GLA_LEVELS):
        p = GLA_LEVELS - 1 - lev
        half = 1 << p
        start = (i >> (p + 1)) << (p + 1)
        mid = start + half - 1
        upper = i >= start + half
        mats.append(np.where(upper, (s > mid) & (s <= i), (s > i) & (s <= mid)))
    seg = np.concatenate(mats, axis=0).astype(np.float32)
    j = np.arange(c)[None, :]
    x = i ^ j
    lvl = np.full((c, c), GLA_LEVELS + 1, np.int32)
    lvl[np.arange(c), np.arange(c)] = GLA_LEVELS
    for lev in range(GLA_LEVELS):
        p = GLA_LEVELS - 1 - lev
        lvl = np.where(((x >> p) == 1) & (((i >> p) & 1) == 1), lev, lvl)
    lvl4 = np.tile(lvl, (GLA_HEADS, 1)).astype(np.int32)
    lane_head = (np.arange(QK_A) // GLA_DK)[None, :]
    row_head = (np.arange(GLA_HEADS * c) // c)[:, None]
    hm4 = (lane_head == row_head).astype(np.float32)
    return seg, lvl4, hm4


def _gla_chunk(q, k, la, v, s_all, seg, lvl4, hm4):
    c = GLA_CHUNK
    e = sum(jnp.dot(seg, part.astype(BF16), preferred_element_type=F32) for part in _split3(la))
    w = jnp.exp(e)
    w_b = w[0:c]
    w_k = w[c:2 * c]

    def stack_heads(t):
        return (jnp.concatenate([t] * GLA_HEADS, axis=0) * hm4).astype(BF16)

    o_inter = jnp.dot(stack_heads(q * w_b), s_all.astype(BF16), preferred_element_type=F32)
    a = jnp.zeros((GLA_HEADS * c, c), F32)
    for lev in range(GLA_LEVELS + 1):
        if lev < GLA_LEVELS:
            w_l = w[(2 + lev) * c:(3 + lev) * c]
            ql, kl = q * w_l, k * w_l
        else:
            ql, kl = q, k
        p_l = lax.dot_general(stack_heads(ql), kl.astype(BF16), (((1,), (1,)), ((), ())),
                              preferred_element_type=F32)
        a = jnp.where(lvl4 == lev, p_l, a)
    a = a.astype(BF16)
    outs = []
    for h in range(GLA_HEADS):
        v_h = v[:, h * GLA_DV:(h + 1) * GLA_DV].astype(BF16)
        o_h = o_inter[h * c:(h + 1) * c] + jnp.dot(a[h * c:(h + 1) * c], v_h, preferred_element_type=F32)
        outs.append(o_h)
    ks_t = (k * w_k).T.astype(BF16)
    kv = jnp.dot(ks_t, v.astype(BF16), preferred_element_type=F32)
    dec = jnp.exp(jnp.sum(la.T, axis=1, keepdims=True))
    new_rows = []
    for h in range(GLA_HEADS):
        rows = slice(h * GLA_DK, (h + 1) * GLA_DK)
        new_rows.append(dec[rows] * s_all[rows] + kv[rows, h * GLA_DV:(h + 1) * GLA_DV])
    return outs, jnp.concatenate(new_rows, axis=0)


def _gla_finish(o_h, norm_g, sg_h):
    o_h = o_h * lax.rsqrt(jnp.mean(o_h * o_h, axis=-1, keepdims=True) + LN_EPS) * norm_g
    return o_h * sg_h


def _gla_prompt_kernel(qkl_ref, v_ref, sg_ref, ng_ref, seg_ref, lvl_ref, hm_ref,
                       mix_ref, state_ref, s_scr):
    t = pl.program_id(1)

    @pl.when(t == 0)
    def _():
        s_scr[...] = jnp.zeros_like(s_scr)

    mix_ref[:, V_A:] = jnp.zeros((mix_ref.shape[0], mix_ref.shape[1] - V_A), F32)
    seg = seg_ref[...]
    lvl4 = lvl_ref[...]
    hm4 = hm_ref[...]
    norm_g = ng_ref[...]
    n_chunks = qkl_ref.shape[0] // GLA_CHUNK

    s_all = s_scr[...]
    for ci in range(n_chunks):
        rows = pl.ds(ci * GLA_CHUNK, GLA_CHUNK)
        q = qkl_ref[rows, 0:QK_A]
        k = qkl_ref[rows, QK_A:2 * QK_A]
        la = qkl_ref[rows, 2 * QK_A:3 * QK_A]
        v = v_ref[rows, :]
        outs, s_all = _gla_chunk(q, k, la, v, s_all, seg, lvl4, hm4)
        if ci == n_chunks - 1:
            s_scr[...] = s_all
        for h in range(GLA_HEADS):
            cols = slice(h * GLA_DV, (h + 1) * GLA_DV)
            mix_ref[rows, cols] = _gla_finish(outs[h], norm_g, sg_ref[rows, cols])

    @pl.when(t == pl.num_programs(1) - 1)
    def _():
        state_ref[0] = s_scr[...]


def _gla_prompt(qkl, v, sg, norm_g, batch, seq, tm):
    m = qkl.shape[0]
    nt = seq // tm
    seg, lvl4, hm4 = _gla_tables()
    row = lambda w: pl.BlockSpec((tm, w), lambda b, t: (b * nt + t, 0))
    return pl.pallas_call(
        _gla_prompt_kernel,
        grid=(batch, nt),
        in_specs=[row(3 * QK_A), row(V_A), row(V_A), _full((1, GLA_DV)),
                  _full(seg.shape), _full(lvl4.shape), _full(hm4.shape)],
        out_specs=[pl.BlockSpec((tm, D_MODEL), lambda b, t: (b * nt + t, 0)),
                   pl.BlockSpec((1, QK_A, GLA_DV), lambda b, t: (b, 0, 0))],
        out_shape=[jax.ShapeDtypeStruct((m, D_MODEL), F32),
                   jax.ShapeDtypeStruct((batch, QK_A, GLA_DV), F32)],
        scratch_shapes=[pltpu.VMEM((QK_A, GLA_DV), F32)],
        compiler_params=_params(2),
        name="gla_prompt",
    )(qkl, v, sg, norm_g, jnp.asarray(seg, BF16), jnp.asarray(lvl4), jnp.asarray(hm4))


GLA_SAMPLE_ROWS = 8


def _gla_sample_kernel(qkl_ref, v_ref, sg_ref, ng_ref, s0_ref, mix_ref, s1_ref):
    mix_ref[:, V_A:] = jnp.zeros((mix_ref.shape[0], mix_ref.shape[1] - V_A), F32)
    ones = jnp.ones((QK_A, GLA_DV), BF16)
    r_i = lax.broadcasted_iota(jnp.int32, (QK_A, QK_A), 0)
    c_i = lax.broadcasted_iota(jnp.int32, (QK_A, QK_A), 1)
    eye = r_i == c_i
    norm_g = ng_ref[...]

    def col_bcast(row):
        out = None
        for part in _split3(row):
            diag = jnp.where(eye, jnp.broadcast_to(part, (QK_A, QK_A)), 0.0).astype(BF16)
            term = jnp.dot(diag, ones, preferred_element_type=F32)
            out = term if out is None else out + term
        return out

    for r in range(GLA_SAMPLE_ROWS):
        q = col_bcast(qkl_ref[r:r + 1, 0:QK_A])
        k = col_bcast(qkl_ref[r:r + 1, QK_A:2 * QK_A])
        dec = jnp.exp(col_bcast(qkl_ref[r:r + 1, 2 * QK_A:3 * QK_A]))
        for h in range(GLA_HEADS):
            rows = slice(h * GLA_DK, (h + 1) * GLA_DK)
            cols = slice(h * GLA_DV, (h + 1) * GLA_DV)
            s_new = dec[rows] * s0_ref[r, rows, :] + k[rows] * v_ref[r:r + 1, cols]
            s1_ref[r, rows, :] = s_new
            o_h = jnp.sum(q[rows] * s_new, axis=0, keepdims=True)
            mix_ref[r:r + 1, cols] = _gla_finish(o_h, norm_g, sg_ref[r:r + 1, cols])


def _gla_sample(qkl, v, sg, norm_g, s0):
    m = qkl.shape[0]
    rb = GLA_SAMPLE_ROWS
    row = lambda w: pl.BlockSpec((rb, w), lambda i: (i, 0))
    st = pl.BlockSpec((rb, QK_A, GLA_DV), lambda i: (i, 0, 0))
    return pl.pallas_call(
        _gla_sample_kernel,
        grid=(m // rb,),
        in_specs=[row(3 * QK_A), row(V_A), row(V_A), _full((1, GLA_DV)), st],
        out_specs=[pl.BlockSpec((rb, D_MODEL), lambda i: (i, 0)), st],
        out_shape=[jax.ShapeDtypeStruct((m, D_MODEL), F32),
                   jax.ShapeDtypeStruct((m, QK_A, GLA_DV), F32)],
        compiler_params=_params(1),
        name="gla_sample",
    )(qkl, v, sg, norm_g, s0)


def _pool_project(d_groups, wp_ref, scale_ref, mix_ref):
    for g in range(len(POOL_WINDOWS)):
        cols = slice(g * POOL_GROUP, (g + 1) * POOL_GROUP)
        mix_ref[:, cols] = _bdot(d_groups[g], wp_ref[g]) * scale_ref[:, cols]


def _pool_prompt_kernel(u_ref, prev_ref, wp_ref, scale_ref, mixin_ref, mix_ref):
    del mixin_ref
    t = pl.program_id(1)
    tm = u_ref.shape[0]
    hist = prev_ref.shape[0]
    prev = jnp.where(t > 0, prev_ref[...], 0.0)
    z = jnp.concatenate([prev, u_ref[...]], axis=0)
    pos = t * tm + lax.broadcasted_iota(jnp.int32, (tm, 1), 0)
    d_groups = []
    for g, w in enumerate(POOL_WINDOWS):
        cols = slice(g * POOL_GROUP, (g + 1) * POOL_GROUP)
        s = z[:, cols]
        shift = 1
        while shift < w:
            s = s + pltpu.roll(s, shift, 0)
            shift *= 2
        cnt = jnp.minimum(w, pos + 1).astype(F32)
        d_groups.append(s[hist:] / cnt - z[hist:, cols])
    _pool_project(d_groups, wp_ref, scale_ref, mix_ref)


def _pool_prompt(u, mix, w_pool, scale, batch, seq, tm):
    nt = seq // tm
    hist = 16
    assert hist > POOL_HIST and tm % hist == 0
    per = tm // hist
    return pl.pallas_call(
        _pool_prompt_kernel,
        grid=(batch, nt),
        in_specs=[pl.BlockSpec((tm, POOL_CH), lambda b, t: (b * nt + t, 0)),
                  pl.BlockSpec((hist, POOL_CH), lambda b, t: (jnp.maximum((b * nt + t) * per - 1, 0), 0)),
                  _full(w_pool.shape), _full((1, POOL_CH)),
                  pl.BlockSpec(memory_space=pl.ANY)],
        out_specs=pl.BlockSpec((tm, POOL_CH), lambda b, t: (b * nt + t, 1)),
        out_shape=jax.ShapeDtypeStruct(mix.shape, F32),
        input_output_aliases={4: 0},
        compiler_params=_params(2),
        name="pool_prompt",
    )(u, u, w_pool, scale, mix)


def _pool_sample_kernel(u_ref, st_ref, wp_ref, scale_ref, mixin_ref, mix_ref, st_out_ref):
    del mixin_ref
    u = u_ref[...]
    d_groups = []
    for g, w in enumerate(POOL_WINDOWS):
        cols = slice(g * POOL_GROUP, (g + 1) * POOL_GROUP)
        win = u[:, cols] + jnp.sum(st_ref[:, POOL_HIST - (w - 1):POOL_HIST, cols], axis=1)
        d_groups.append(win / float(w) - u[:, cols])
    _pool_project(d_groups, wp_ref, scale_ref, mix_ref)
    st_out_ref[:, 0:POOL_HIST - 1, :] = st_ref[:, 1:POOL_HIST, :]
    st_out_ref[:, POOL_HIST - 1:POOL_HIST, :] = u[:, None, :]


def _pool_sample(u, st, mix, w_pool, scale, pos0):
    m = u.shape[0]
    assert pos0 + 1 >= max(POOL_WINDOWS)
    return pl.pallas_call(
        _pool_sample_kernel,
        grid=(1,),
        in_specs=[_full((m, POOL_CH)), _full(st.shape), _full(w_pool.shape), _full((1, POOL_CH)),
                  pl.BlockSpec(memory_space=pl.ANY)],
        out_specs=[pl.BlockSpec((m, POOL_CH), lambda i: (0, 1)), _full(st.shape)],
        out_shape=[jax.ShapeDtypeStruct(mix.shape, F32), jax.ShapeDtypeStruct(st.shape, F32)],
        input_output_aliases={4: 0},
        compiler_params=_params(1),
        name="pool_sample",
    )(u, st, w_pool, scale, mix)


def _sub_tiles(tm):
    ts = SUB_TILE if tm % SUB_TILE == 0 else tm
    return [slice(s, s + ts) for s in range(0, tm, ts)]


def _mix_kernel(x_ref, mix_ref, w_ref, lng_ref, lnb_ref, wr_ref, br_ref, *rest):
    x1_ref, xpk_ref, idx_ref, gate_ref, cnt_ref, w_bf = rest[-6:]

    @pl.when(pl.program_id(0) == 0)
    def _():
        w_bf[...] = w_ref[...].astype(BF16)
        cnt_ref[...] = jnp.zeros_like(cnt_ref)

    for rows in _sub_tiles(x_ref.shape[0]):
        h = jnp.dot(mix_ref[rows, :].astype(BF16), w_bf[...], preferred_element_type=F32)
        x1 = _layer_norm(DEEPNORM_ALPHA * x_ref[rows, :] + h, lng_ref[...], lnb_ref[...])
        x1_ref[rows, :] = x1
        xpk_ref[rows, :] = _pack_bf16_pairs(x1)
        logits = _bdot(x1, wr_ref[...]) + br_ref[...]
        ts = logits.shape[0]
        lane = lax.broadcasted_iota(jnp.int32, (ts, N_EXPERTS), 1)
        lane_k = lax.broadcasted_iota(jnp.int32, (ts, TOP_K), 1)
        idx_out = jnp.zeros((ts, TOP_K), jnp.int32)
        val_out = jnp.zeros((ts, TOP_K), F32)
        cur = logits
        chosen = jnp.zeros((ts, N_EXPERTS), F32)
        for kk in range(TOP_K):
            mval = jnp.max(cur, axis=1, keepdims=True)
            midx = jnp.min(jnp.where(cur == mval, lane.astype(F32), float(N_EXPERTS)), axis=1,
                           keepdims=True).astype(jnp.int32)
            idx_out = jnp.where(lane_k == kk, midx, idx_out)
            val_out = jnp.where(lane_k == kk, mval, val_out)
            picked = lane == midx
            cur = jnp.where(picked, -jnp.inf, cur)
            chosen = jnp.where(picked, 1.0, chosen)
        cnt_ref[...] += jnp.sum(chosen, axis=0, keepdims=True)
        ex = jnp.exp(val_out - val_out[:, 0:1])
        idx_ref[rows, :] = idx_out
        gate_ref[rows, :] = ex / jnp.sum(ex, axis=1, keepdims=True)


def _pack_bf16_pairs(x):
    w = x.shape[1] // 2
    u = lax.bitcast_convert_type(x, jnp.int32)
    r = u + 0x7FFF + (lax.shift_right_logical(u, 16) & 1)
    word = (r[:, :w] & -65536) | lax.shift_right_logical(r[:, w:], 16)
    return lax.bitcast_convert_type(word, F32)


def _unpack_bf16_pairs(words):
    u = lax.bitcast_convert_type(words, jnp.int32)
    hi = lax.bitcast_convert_type(u & -65536, F32)
    lo = lax.bitcast_convert_type(u << 16, F32)
    return jnp.concatenate([hi, lo], axis=1).astype(BF16)


def _mix(x, mix, w_out, ln_g, ln_b, w_router, b_router, tm, x1_buf, xpk_buf, row0):
    m = x.shape[0]
    n_total = x1_buf.shape[0]
    assert row0 % tm == 0 and row0 + m <= n_total
    off_blocks = row0 // tm
    row = lambda w: pl.BlockSpec((tm, w), lambda i: (i, 0))
    in_specs = [row(D_MODEL), row(D_MODEL), _full((D_MODEL, D_MODEL)), _full((1, D_MODEL)),
                _full((1, D_MODEL)), _full((D_MODEL, N_EXPERTS)), _full((1, N_EXPERTS)),
                pl.BlockSpec(memory_space=pl.ANY), pl.BlockSpec(memory_space=pl.ANY)]
    args = [x, mix, w_out, ln_g, ln_b, w_router, b_router, x1_buf, xpk_buf]
    aliases = {len(args) - 2: 0, len(args) - 1: 1}
    return pl.pallas_call(
        _mix_kernel,
        grid=(m // tm,),
        in_specs=in_specs,
        out_specs=[pl.BlockSpec((tm, D_MODEL), lambda i: (i + off_blocks, 0)),
                   pl.BlockSpec((tm, D_MODEL // 2), lambda i: (i + off_blocks, 0)), row(TOP_K), row(TOP_K),
                   _full((1, N_EXPERTS))],
        out_shape=[jax.ShapeDtypeStruct((n_total, D_MODEL), F32),
                   jax.ShapeDtypeStruct((n_total, D_MODEL // 2), F32),
                   jax.ShapeDtypeStruct((m, TOP_K), jnp.int32), jax.ShapeDtypeStruct((m, TOP_K), F32),
                   jax.ShapeDtypeStruct((1, N_EXPERTS), F32)],
        scratch_shapes=[pltpu.VMEM((D_MODEL, D_MODEL), BF16)],
        input_output_aliases=aliases,
        compiler_params=_params(1),
        name="mix",
    )(*args)


def _rank_kernel(idx_ref, pstart_ref, rank_ref, carry):
    i = pl.program_id(0)

    @pl.when(i == 0)
    def _():
        carry[...] = pstart_ref[...].astype(F32)

    idx = idx_ref[...]
    tm = idx.shape[0]
    lane = lax.broadcasted_iota(jnp.int32, (tm, N_EXPERTS), 1)
    onehots = [(idx[:, kk:kk + 1] == lane) for kk in range(TOP_K)]
    member = sum(jnp.where(o, 1.0, 0.0) for o in onehots)
    r_i = lax.broadcasted_iota(jnp.int32, (tm, tm), 0)
    c_i = lax.broadcasted_iota(jnp.int32, (tm, tm), 1)
    strict_lower = jnp.where(c_i < r_i, 1.0, 0.0).astype(BF16)
    before = jnp.dot(strict_lower, member.astype(BF16), preferred_element_type=F32) + carry[...]
    lane_k = lax.broadcasted_iota(jnp.int32, (tm, TOP_K), 1)
    rank = jnp.zeros((tm, TOP_K), F32)
    for kk in range(TOP_K):
        r_k = jnp.sum(jnp.where(onehots[kk], before, 0.0), axis=1, keepdims=True)
        rank = jnp.where(lane_k == kk, r_k, rank)
    rank_ref[...] = rank.astype(jnp.int32)
    carry[...] = carry[...] + jnp.sum(member, axis=0, keepdims=True)


def _rank(idx, pstart):
    n = idx.shape[0]
    tm = RANK_TILE
    return pl.pallas_call(
        _rank_kernel,
        grid=(n // tm,),
        in_specs=[pl.BlockSpec((tm, TOP_K), lambda i: (i, 0)), _full((1, N_EXPERTS))],
        out_specs=pl.BlockSpec((tm, TOP_K), lambda i: (i, 0)),
        out_shape=jax.ShapeDtypeStruct((n, TOP_K), jnp.int32),
        scratch_shapes=[pltpu.VMEM((1, N_EXPERTS), F32)],
        compiler_params=_params(1),
        name="moe_rank",
    )(idx, pstart.reshape(1, N_EXPERTS))


def _expert_kernel(be_ref, nu_ref, xs_ref, wgu_ref, bgu_ref, wd_ref, bd_ref, ys_ref, wgu_bf, wd_bf):
    blk = pl.program_id(0)
    prev = be_ref[jnp.maximum(blk - 1, 0)]
    fresh = jnp.logical_or(blk == 0, be_ref[blk] != prev)
    used = blk < nu_ref[0]

    @pl.when(jnp.logical_and(fresh, used))
    def _():
        wgu_bf[...] = wgu_ref[0, 0].astype(BF16)
        wd_bf[...] = wd_ref[0, 0].astype(BF16)

    @pl.when(used)
    def _():
        xb = _unpack_bf16_pairs(xs_ref[...])
        acc = jnp.zeros(ys_ref.shape, F32) + bd_ref[0, 0]
        half = D_EXPERT // 2
        for c in range(2):
            cg = slice(c * half, (c + 1) * half)
            cu = slice(D_EXPERT + c * half, D_EXPERT + (c + 1) * half)
            gate = jnp.dot(xb, wgu_bf[:, cg], preferred_element_type=F32) + bgu_ref[0, 0, :, cg]
            up = jnp.dot(xb, wgu_bf[:, cu], preferred_element_type=F32) + bgu_ref[0, 0, :, cu]
            gate = jnp.minimum(gate, SWIGLU_LIMIT)
            up = jnp.clip(up, -SWIGLU_LIMIT, SWIGLU_LIMIT)
            act = (up + 1.0) * gate * jax.nn.sigmoid(SWIGLU_ALPHA * gate)
            acc = acc + jnp.dot(act.astype(BF16), wd_bf[cg, :], preferred_element_type=F32)
        ys_ref[...] = acc

    @pl.when(jnp.logical_not(used))
    def _():
        ys_ref[...] = jnp.zeros_like(ys_ref)


def _experts(layer, block_e, n_used, xs, w_gu, b_gu, w_down, b_down):
    p = xs.shape[0]
    bm = MOE_BLOCK
    nb = p // bm
    depth = w_gu.shape[0]
    last = lambda blk, nu: jnp.minimum(blk, nu[0] - 1)
    grid_spec = pltpu.PrefetchScalarGridSpec(
        num_scalar_prefetch=2,
        grid=(nb,),
        in_specs=[pl.BlockSpec((bm, D_MODEL // 2), lambda blk, be, nu: (last(blk, nu), 0)),
                  pl.BlockSpec((1, 1, D_MODEL, 2 * D_EXPERT), lambda blk, be, nu: (layer, be[blk], 0, 0)),
                  pl.BlockSpec((1, 1, 1, 2 * D_EXPERT), lambda blk, be, nu: (layer, be[blk], 0, 0)),
                  pl.BlockSpec((1, 1, D_EXPERT, D_MODEL), lambda blk, be, nu: (layer, be[blk], 0, 0)),
                  pl.BlockSpec((1, 1, 1, D_MODEL), lambda blk, be, nu: (layer, be[blk], 0, 0))],
        out_specs=pl.BlockSpec((bm, D_MODEL), lambda blk, be, nu: (blk, 0)),
        scratch_shapes=[pltpu.VMEM((D_MODEL, 2 * D_EXPERT), BF16), pltpu.VMEM((D_EXPERT, D_MODEL), BF16)],
    )
    return pl.pallas_call(
        _expert_kernel,
        grid_spec=grid_spec,
        out_shape=jax.ShapeDtypeStruct((p, D_MODEL), F32),
        compiler_params=_params(1),
        name="moe_experts",
    )(block_e, n_used, xs, w_gu, b_gu.reshape(depth, N_EXPERTS, 1, 2 * D_EXPERT), w_down,
      b_down.reshape(depth, N_EXPERTS, 1, D_MODEL))


def _finish_kernel(x1_ref, y0_ref, y1_ref, y2_ref, y3_ref, gate_ref, p_ref, lng_ref, lnb_ref,
                   wpg_ref, wpp_ref, out_ref, wpg_bf):
    @pl.when(pl.program_id(0) == 0)
    def _():
        wpg_bf[...] = wpg_ref[...].astype(BF16)

    for rows in _sub_tiles(out_ref.shape[0]):
        gates = gate_ref[rows, :]
        moe = y0_ref[rows, :] * gates[:, 0:1]
        for kk, y_ref in enumerate((y1_ref, y2_ref, y3_ref), start=1):
            moe = moe + y_ref[rows, :] * gates[:, kk:kk + 1]
        x2 = _layer_norm(DEEPNORM_ALPHA * x1_ref[rows, :] + moe, lng_ref[...], lnb_ref[...])
        pg = jax.nn.sigmoid(jnp.dot(x2.astype(BF16), wpg_bf[...], preferred_element_type=F32))
        pp = _bdot(p_ref[rows, :], wpp_ref[...])
        out_ref[rows, :] = x2 + pg * pp


def _finish(x1_all, row0, m, ysg, gates, p_all, layer, ln_g, ln_b, w_pg, w_pp, tm):
    nt = m // tm
    assert row0 % tm == 0
    row = lambda w: pl.BlockSpec((tm, w), lambda i: (i, 0))
    ple = p_all.shape[2]
    y_specs = [pl.BlockSpec((tm, D_MODEL), lambda i, kk=kk: (kk * nt + i, 0)) for kk in range(TOP_K)]
    x1_spec = pl.BlockSpec((tm, D_MODEL), lambda i: (i + row0 // tm, 0))
    p_spec = pl.BlockSpec((None, tm, ple), lambda i: (layer, i, 0))
    return pl.pallas_call(
        _finish_kernel,
        grid=(nt,),
        in_specs=[x1_spec] + y_specs + [row(TOP_K), p_spec, _full((1, D_MODEL)), _full((1, D_MODEL)),
                                        _full((D_MODEL, D_MODEL)), _full((ple, D_MODEL))],
        out_specs=row(D_MODEL),
        out_shape=jax.ShapeDtypeStruct((m, D_MODEL), F32),
        scratch_shapes=[pltpu.VMEM((D_MODEL, D_MODEL), BF16)],
        compiler_params=_params(1),
        name="finish",
    )(x1_all, ysg, ysg, ysg, ysg, gates, p_all, ln_g, ln_b, w_pg, w_pp)


def _qkv_kernel(x_ref, w_ref, q_ref, k_ref, v_ref, w_bf):
    @pl.when(pl.program_id(0) == 0)
    def _():
        w_bf[...] = w_ref[...].astype(BF16)

    xb = x_ref[...].astype(BF16)
    q_ref[...] = jnp.dot(xb, w_bf[:, 0:D_MODEL], preferred_element_type=F32) * (MOBA_DH ** -0.5)
    k_ref[...] = jnp.dot(xb, w_bf[:, D_MODEL:2 * D_MODEL], preferred_element_type=F32)
    v_ref[...] = jnp.dot(xb, w_bf[:, 2 * D_MODEL:3 * D_MODEL], preferred_element_type=F32)


def _qkv(x, w_qkv, tm):
    m = x.shape[0]
    row = pl.BlockSpec((tm, D_MODEL), lambda i: (i, 0))
    shp = jax.ShapeDtypeStruct((m, D_MODEL), F32)
    return pl.pallas_call(
        _qkv_kernel,
        grid=(m // tm,),
        in_specs=[row, _full((D_MODEL, 3 * D_MODEL))],
        out_specs=[row, row, row],
        out_shape=[shp, shp, shp],
        scratch_shapes=[pltpu.VMEM((D_MODEL, 3 * D_MODEL), BF16)],
        compiler_params=_params(1),
        name="qkv",
    )(x, w_qkv)


def _topk_mask(gate, n_valid, axis):
    nb = gate.shape[axis]
    pos_i = lax.broadcasted_iota(jnp.int32, gate.shape, axis)
    valid = pos_i < n_valid
    pos = pos_i.astype(F32)
    cur = jnp.where(valid, gate, -jnp.inf)
    sel = jnp.zeros(gate.shape, F32)
    for _ in range(min(MOBA_TOPK, nb)):
        best = jnp.max(cur, axis=axis, keepdims=True)
        first = jnp.min(jnp.where(cur == best, pos, float(nb)), axis=axis, keepdims=True)
        pick = jnp.logical_and(pos == first, valid)
        sel = jnp.where(pick, 1.0, sel)
        cur = jnp.where(pick, -jnp.inf, cur)
    return sel


MOBA_AUG_MASK0 = 8
MOBA_NEG = -1e30


def _split3(x):
    hi = x.astype(BF16).astype(F32)
    mid = (x - hi).astype(BF16).astype(F32)
    lo = (x - hi - mid).astype(BF16).astype(F32)
    return hi, mid, lo


def _moba_prompt_kernel(q_ref, k_ref, v_ref, slope_ref, o_ref, qt_scr, vtp_scr, ka_scr, km_scr, s_scr):
    hp = pl.program_id(1)
    qt = pl.program_id(2)
    nb = km_scr.shape[0]
    blk = MOBA_BLOCK
    qw = 2 * blk
    dh = MOBA_DH
    pair = 2 * dh

    @pl.when(qt == 0)
    def _():
        lane = lax.broadcasted_iota(jnp.int32, (blk, pair), 1)
        key_off = lax.broadcasted_iota(jnp.int32, (blk, pair), 0).astype(F32)
        for n in range(nb):
            rows = slice(n * blk, (n + 1) * blk)
            half = slice((n % 2) * blk, (n % 2 + 1) * blk)
            qt_scr[n // 2, :, half] = q_ref[rows, :].T.astype(BF16)
            vtp_scr[n // 2, :, half] = v_ref[rows, :].T.astype(BF16)
            kblk = k_ref[rows, :]
            km_scr[n:n + 1, :] = jnp.mean(kblk, axis=0, keepdims=True)
            for hh in range(2):
                a = lane - (1 - hh) * dh
                aug = jnp.where(a < 3, key_off,
                                jnp.where(a < 6, float(n * blk),
                                          jnp.where(a == MOBA_AUG_MASK0 + n, 1.0, 0.0)))
                in_head = jnp.logical_and(lane >= hh * dh, lane < (hh + 1) * dh)
                ka_scr[hh, rows, :] = jnp.where(in_head, kblk, aug).astype(BF16)

    key_i = lax.broadcasted_iota(jnp.int32, (qw, qw), 0)
    qry_i = lax.broadcasted_iota(jnp.int32, (qw, qw), 1)
    same_block = (key_i >= blk) == (qry_i >= blk)
    causal = jnp.logical_and(same_block, key_i <= qry_i)
    first_key_second_qry = jnp.logical_and(key_i < blk, qry_i >= blk)
    col = lax.broadcasted_iota(jnp.int32, (1, qw), 1)
    cur_blk = 2 * qt + jnp.where(col >= blk, 1, 0)
    blk_row = lax.broadcasted_iota(jnp.int32, (nb, qw), 0)

    own = pl.ds(pl.multiple_of(qt * qw, qw), qw)
    q_t = qt_scr[qt].astype(F32)
    v_t_d = vtp_scr[qt]
    lane_k = lax.broadcasted_iota(jnp.int32, (nb, pair), 1)
    r8 = lax.broadcasted_iota(jnp.int32, (8, qw), 0)
    is_hi = jnp.logical_or(r8 == 0, r8 == 3)
    is_mid = jnp.logical_or(r8 == 1, r8 == 4)
    pad_rows = jnp.zeros((dh - MOBA_AUG_MASK0 - nb, qw), F32)

    q_aug, init = [], []
    for hh in range(2):
        hrows = slice(hh * dh, (hh + 1) * dh)
        s_hi, s_mid, s_lo = _split3(slope_ref[pl.ds(2 * hp + hh, 1), :])
        slope_rows = jnp.where(is_hi, s_hi, jnp.where(is_mid, s_mid, s_lo))
        slope_rows = jnp.where(r8 < 6, slope_rows, 0.0)

        def with_aug(aug_rows, hh=hh, hrows=hrows):
            parts = [q_t[hrows], aug_rows] if hh == 0 else [aug_rows, q_t[hrows]]
            return jnp.concatenate(parts, axis=0).astype(BF16)

        q_diag = with_aug(jnp.concatenate([slope_rows, jnp.zeros((dh - 8, qw), F32)], axis=0))
        in_head = jnp.logical_and(lane_k >= hh * dh, lane_k < (hh + 1) * dh)
        km_h = jnp.where(in_head, km_scr[...], 0.0).astype(BF16)
        gate_t = jnp.dot(km_h, q_diag, preferred_element_type=F32)
        sel = _topk_mask(gate_t, cur_blk, 0)
        mask_rows = jnp.where(sel > 0.0, 0.0, MOBA_NEG)
        q_aug.append(with_aug(jnp.concatenate([slope_rows, mask_rows, pad_rows], axis=0)))

        first_sel = jnp.sum(jnp.where(blk_row == 2 * qt, sel, 0.0), axis=0, keepdims=True)
        allowed = jnp.logical_or(causal, jnp.logical_and(first_key_second_qry, first_sel > 0.0))
        s = jnp.dot(ka_scr[hh, own, :], q_diag, preferred_element_type=F32)
        s = jnp.where(allowed, s, -jnp.inf)
        m0 = jnp.max(s, axis=0, keepdims=True)
        p = jnp.exp(s - m0)
        l0 = jnp.sum(p, axis=0, keepdims=True)
        acc0 = jnp.dot(v_t_d[hrows, :], p.astype(BF16), preferred_element_type=F32)
        init += [m0, l0, acc0]

    def scores(pair_idx, slot):
        keys = pl.ds(pl.multiple_of(pair_idx * qw, qw), qw)
        col_max = []
        for hh in range(2):
            s = jnp.dot(ka_scr[hh, keys, :], q_aug[hh], preferred_element_type=F32)
            s_scr[slot, hh] = s
            col_max.append(jnp.max(s, axis=0, keepdims=True))
        return col_max

    n_pairs = qt

    def body(jj, carry):
        slot = jj % 2
        v_pair = vtp_scr[jj]
        out = []
        for hh in range(2):
            m, l, acc, mx = carry[4 * hh:4 * hh + 4]
            m_new = jnp.maximum(m, mx)
            alpha = jnp.exp(m - m_new)
            p = jnp.exp(s_scr[slot, hh] - m_new)
            l = alpha * l + jnp.sum(p, axis=0, keepdims=True)
            acc = alpha * acc + jnp.dot(v_pair[hh * dh:(hh + 1) * dh, :], p.astype(BF16),
                                        preferred_element_type=F32)
            out.append([m_new, l, acc])
        nxt = scores(jnp.minimum(jj + 1, nb // 2 - 1), 1 - slot)
        return tuple(out[0] + [nxt[0]] + out[1] + [nxt[1]])

    mx0 = scores(0, 0)
    res = lax.fori_loop(0, n_pairs, body, tuple(init[0:3] + [mx0[0]] + init[3:6] + [mx0[1]]))
    o_ref[...] = jnp.concatenate([res[2] / res[1], res[6] / res[5]], axis=0).T


def _moba_prompt(q, k, v, slopes, batch, seq):
    m = q.shape[0]
    nb = seq // MOBA_BLOCK
    assert nb % 2 == 0 and MOBA_AUG_MASK0 + nb <= MOBA_DH
    pair = 2 * MOBA_DH
    n_pairs = MOBA_HEADS // 2
    qw = 2 * MOBA_BLOCK
    nt = nb // 2
    slope_rows = jnp.broadcast_to(slopes[:, None], (MOBA_HEADS, qw))
    seq_spec = pl.BlockSpec((seq, pair), lambda b, hp, qt: (b, hp))
    return pl.pallas_call(
        _moba_prompt_kernel,
        grid=(batch, n_pairs, nt),
        in_specs=[seq_spec, seq_spec, seq_spec, _full((MOBA_HEADS, qw))],
        out_specs=pl.BlockSpec((qw, pair), lambda b, hp, qt: (b * nt + qt, hp)),
        out_shape=jax.ShapeDtypeStruct((m, D_MODEL), F32),
        scratch_shapes=[pltpu.VMEM((nt, pair, qw), BF16), pltpu.VMEM((nt, pair, qw), BF16),
                        pltpu.VMEM((2, seq, pair), BF16), pltpu.VMEM((nb, pair), F32),
                        pltpu.VMEM((2, 2, qw, qw), F32)],
        compiler_params=_params(3),
        name="moba_prompt",
    )(q, k, v, slope_rows)


def _moba_sample_kernel(n_pages, pos0, pt_ref, q_ref, kn_ref, vn_ref, slope_ref, *refs):
    del pt_ref
    k_pages = refs[:n_pages]
    v_pages = refs[n_pages:2 * n_pages]
    o_ref = refs[2 * n_pages]
    per_blk = MOBA_BLOCK // PAGE_SIZE
    n_blk = n_pages // per_blk
    h_i = lax.broadcasted_iota(jnp.int32, (MOBA_HEADS, D_MODEL), 0)
    d_i = lax.broadcasted_iota(jnp.int32, (MOBA_HEADS, D_MODEL), 1)
    own_head = d_i // MOBA_DH == h_i
    q_bd = jnp.where(own_head, q_ref[0], 0.0)
    q_bd16 = q_bd.astype(BF16)
    slope = slope_ref[...]

    lane_n = lax.broadcasted_iota(jnp.int32, (D_MODEL, n_blk), 1)
    means = jnp.zeros((D_MODEL, n_blk), F32)
    for n in range(n_blk):
        tot = jnp.sum(sum(k_pages[n * per_blk + i][...] for i in range(per_blk)), axis=1, keepdims=True)
        means = jnp.where(lane_n == n, tot / float(MOBA_BLOCK), means)
    gate = jnp.dot(q_bd16, means.astype(BF16), preferred_element_type=F32)
    sel = _topk_mask(gate, n_blk, 1)

    lane = lax.broadcasted_iota(jnp.int32, (1, PAGE_SIZE), 1)
    scores = []
    for pg in range(n_pages):
        s = jnp.dot(q_bd16, k_pages[pg][...].astype(BF16), preferred_element_type=F32)
        dist = (pos0 - pg * PAGE_SIZE - lane).astype(F32)
        s = s - slope * dist
        n = pg // per_blk
        scores.append(jnp.where(sel[:, n:n + 1] > 0.0, s, -jnp.inf))
    k_new = kn_ref[0].astype(BF16).astype(F32)
    s_new = jnp.sum(q_bd16.astype(F32) * k_new, axis=1, keepdims=True)
    m = s_new
    for s in scores:
        m = jnp.maximum(m, jnp.max(s, axis=1, keepdims=True))
    p_new = jnp.exp(s_new - m)
    l = p_new
    d_e = lax.broadcasted_iota(jnp.int32, (D_MODEL, MOBA_HEADS), 0)
    h_e = lax.broadcasted_iota(jnp.int32, (D_MODEL, MOBA_HEADS), 1)
    expand = jnp.where(d_e // MOBA_DH == h_e, 1.0, 0.0).astype(BF16)
    acc = jnp.zeros((D_MODEL, PAGE_SIZE), F32)
    for pg in range(n_pages):
        p = jnp.exp(scores[pg] - m)
        l = l + jnp.sum(p, axis=1, keepdims=True)
        acc = acc + jnp.dot(expand, p.astype(BF16), preferred_element_type=F32) * v_pages[pg][...]
    o_past = jnp.sum(acc.T, axis=0, keepdims=True)
    row_of = lambda col: jnp.sum(jnp.where(own_head, col, 0.0), axis=0, keepdims=True)
    o_new = row_of(p_new.astype(BF16).astype(F32)) * vn_ref[0]
    o_ref[0] = (o_past + o_new) / row_of(l)


def _moba_sample(q, k_new, v_new, cache_k, cache_v, page_table, slopes, pos0):
    m = q.shape[0]
    n_pages = page_table.shape[1]
    assert pos0 == n_pages * PAGE_SIZE and pos0 % MOBA_BLOCK == 0
    n_phys = cache_k.shape[0]
    ck = jnp.transpose(cache_k, (0, 2, 3, 1)).reshape(n_phys, D_MODEL, PAGE_SIZE)
    cv = jnp.transpose(cache_v, (0, 2, 3, 1)).reshape(n_phys, D_MODEL, PAGE_SIZE)
    vec = pl.BlockSpec((1, 1, D_MODEL), lambda b, pt: (b, 0, 0))
    page_specs = [pl.BlockSpec((None, D_MODEL, PAGE_SIZE), lambda b, pt, pg=pg: (pt[b, pg], 0, 0))
                  for pg in range(n_pages)]
    grid_spec = pltpu.PrefetchScalarGridSpec(
        num_scalar_prefetch=1,
        grid=(m,),
        in_specs=[vec, vec, vec, pl.BlockSpec((MOBA_HEADS, 1), lambda b, pt: (0, 0))] + page_specs + page_specs,
        out_specs=vec,
    )
    out = pl.pallas_call(
        functools.partial(_moba_sample_kernel, n_pages, pos0),
        grid_spec=grid_spec,
        out_shape=jax.ShapeDtypeStruct((m, 1, D_MODEL), F32),
        compiler_params=_params(1),
        name="moba_sample",
    )(page_table, q.reshape(m, 1, D_MODEL), k_new.reshape(m, 1, D_MODEL), v_new.reshape(m, 1, D_MODEL),
      slopes.reshape(MOBA_HEADS, 1), *([ck] * n_pages), *([cv] * n_pages))
    return out.reshape(m, D_MODEL)


def _layer_tail(layer, xp, xs, mix_p, mix_s, p_p, p_s, w_out, ln_g, ln_b, w_router, b_router,
                w_gu, b_gu, w_down, b_down, w_pg, w_pp):
    n_p, n_s = xp.shape[0], xs.shape[0]
    lg0, lb0 = ln_g[0:1], ln_b[0:1]
    lg1, lb1 = ln_g[1:2], ln_b[1:2]
    br = b_router.reshape(1, N_EXPERTS)
    x1 = jnp.zeros((n_p + n_s, D_MODEL), F32)
    xpk = jnp.zeros((n_p + n_s, D_MODEL // 2), F32)
    x1, xpk, idxp, gatep, cnt_p = _mix(xp, mix_p, w_out, lg0, lb0, w_router, br, ROW_TILE, x1, xpk, 0)
    x1, xpk, idxs, gates, cnt_s = _mix(xs, mix_s, w_out, lg0, lb0, w_router, br, n_s, x1, xpk, n_p)

    idx = jnp.concatenate([idxp, idxs], axis=0)
    counts = (cnt_p + cnt_s)[0].astype(jnp.int32)
    bm = MOE_BLOCK
    n_tok = n_p + n_s
    n_blocks = (n_tok * TOP_K + N_EXPERTS * (bm - 1)) // bm
    padded = (counts + bm - 1) // bm * bm
    pend = jnp.cumsum(padded)
    pstart = pend - padded
    dest = _rank(idx, pstart)
    n_used = (pend[-1] // bm).astype(jnp.int32)
    blk_ids = jnp.minimum(jnp.arange(n_blocks, dtype=jnp.int32), n_used - 1)
    block_e = jnp.sum((pend[None, :] <= (blk_ids * bm)[:, None]).astype(jnp.int32), axis=1)
    block_e = jnp.minimum(block_e, N_EXPERTS - 1)
    nk = n_tok * TOP_K
    low_bits = (nk - 1).bit_length()
    assert (N_EXPERTS << low_bits) < 2 ** 31
    order = jnp.sort((idx.reshape(-1) << low_bits) + jnp.arange(nk, dtype=jnp.int32)) & ((1 << low_bits) - 1)
    start = jnp.cumsum(counts) - counts
    r_in_group = (blk_ids * bm - pstart[block_e])[:, None] + jnp.arange(bm, dtype=jnp.int32)[None, :]
    live = r_in_group < counts[block_e][:, None]
    src = jnp.where(live, start[block_e][:, None] + r_in_group, 0)
    row_tok = jnp.where(live, order.at[src].get(mode="promise_in_bounds") // TOP_K, 0).reshape(-1)
    gather = lambda src, rows: src.at[rows].get(mode="promise_in_bounds")
    xs_rows = gather(xpk, row_tok)
    ys = _experts(layer, block_e, n_used.reshape(1), xs_rows, w_gu, b_gu, w_down, b_down)
    ysg_p = gather(ys, dest[:n_p].T.reshape(-1))
    ysg_s = gather(ys, dest[n_p:].T.reshape(-1))
    yp = _finish(x1, 0, n_p, ysg_p, gatep, p_p, layer, lg1, lb1, w_pg, w_pp, ROW_TILE)
    ys_out = _finish(x1, n_p, n_s, ysg_s, gates, p_s, layer, lg1, lb1, w_pg, w_pp, n_s)
    return yp, ys_out


def kernel(x_prompt, x_sample, state_gla, state_pool, cache_k, cache_v, page_table, p_prompt, p_sample, w_in_ab, gla_w_alpha, gla_b_alpha, gla_norm_g, pool_w, pool_scale, w_out_ab, w_qkv_c, w_out_c, ln_g, ln_b, moe_w_router, moe_b_router, moe_w_gu, moe_b_gu, moe_w_down, moe_b_down, ple_w_gate, ple_w_proj):
    batch, seq, _ = x_prompt.shape
    n_s = x_sample.shape[0]
    n_p = batch * seq
    pos0 = page_table.shape[1] * PAGE_SIZE
    xp = x_prompt.reshape(n_p, D_MODEL)
    xs = x_sample.reshape(n_s, D_MODEL)

    def tail(i, xp, xs, mix_p, mix_s, w_out):
        depth = p_prompt.shape[0]
        return _layer_tail(i, xp, xs, mix_p, mix_s, p_prompt.reshape(depth, n_p, -1), p_sample.reshape(depth, n_s, -1),
                           w_out, ln_g[i], ln_b[i], moe_w_router[i], moe_b_router[i], moe_w_gu,
                           moe_b_gu, moe_w_down, moe_b_down, ple_w_gate[i], ple_w_proj[i])

    w_in = w_in_ab[0]
    c_a = 2 * QK_A + 2 * V_A
    w_main = jnp.concatenate([w_in[:, :c_a], w_in[:, c_a + GLA_LOWRANK:]], axis=1)
    w_a = w_in[:, c_a:c_a + GLA_LOWRANK]
    b_alpha = gla_b_alpha[0].reshape(1, QK_A)
    norm_g = gla_norm_g[0].reshape(1, GLA_DV)
    scale = pool_scale[0].reshape(1, POOL_CH)
    qkl_p, v_p, sg_p, u_p = _proj_ab(xp, w_main, w_a, gla_w_alpha[0], b_alpha, ROW_TILE)
    qkl_s, v_s, sg_s, u_s = _proj_ab(xs, w_main, w_a, gla_w_alpha[0], b_alpha, n_s)
    mix_p, gla_p = _gla_prompt(qkl_p, v_p, sg_p, norm_g, batch, seq, ROW_TILE)
    mix_p = _pool_prompt(u_p, mix_p, pool_w[0], scale, batch, seq, ROW_TILE)
    mix_s, gla_s = _gla_sample(qkl_s, v_s, sg_s, norm_g, state_gla[0].reshape(n_s, QK_A, GLA_DV))
    mix_s, pool_s = _pool_sample(u_s, state_pool[0], mix_s, pool_w[0], scale, pos0)
    pool_p = u_p.reshape(batch, seq, POOL_CH)[:, seq - POOL_HIST:]
    xp, xs = tail(0, xp, xs, mix_p, mix_s, w_out_ab[0])

    slopes = jnp.exp2(-8.0 * jnp.arange(1, MOBA_HEADS + 1, dtype=F32) / MOBA_HEADS)
    q_p, k_p, v_p2 = _qkv(xp, w_qkv_c[0], ROW_TILE)
    q_s, k_s, v_s2 = _qkv(xs, w_qkv_c[0], n_s)
    o_p = _moba_prompt(q_p, k_p, v_p2, slopes, batch, seq)
    o_s = _moba_sample(q_s, k_s, v_s2, cache_k[0], cache_v[0], page_table, slopes, pos0)
    xp, xs = tail(1, xp, xs, o_p, o_s, w_out_c[0])

    hd = (MOBA_HEADS, MOBA_DH)
    return (xp.reshape(batch, seq, D_MODEL), xs.reshape(n_s, 1, D_MODEL),
            gla_p.reshape(1, batch, GLA_HEADS, GLA_DK, GLA_DV), gla_s.reshape(1, n_s, GLA_HEADS, GLA_DK, GLA_DV),
            pool_p[None], pool_s[None],
            k_p.reshape(1, batch, seq, *hd), v_p2.reshape(1, batch, seq, *hd),
            k_s.reshape(1, n_s, 1, *hd), v_s2.reshape(1, n_s, 1, *hd))
```
